```python
import math
import jax, jax.numpy as jnp
from jax import lax
import numpy as np

D_MODEL = 2048
BATCH = 2
SEQ = 16384
DEPTH = 2

N_MIXERS = 4
GROUP_W = D_MODEL // N_MIXERS
HEAD_DIM = 64
N_HEADS = GROUP_W // HEAD_DIM
N_KV_HEADS = 2
GQA_GROUP = N_HEADS // N_KV_HEADS
KV_W = N_KV_HEADS * HEAD_DIM
SHORT_CONV = 3
GDN_CHUNK = 64
HY_ORDER = 2
HY_EMB = 33
HY_BANDS = (HY_EMB - 1) // 2
HY_FFN = 64
HY_DECAY_TARGET = 1e-2
HY_FAST_DECAY = 0.3
HY_SLOW_DECAY = 1.5
WINDOW = 128
BLOCK = 128
GRID_W = 64
ROPE_THETA = 10000.0
N_EXPERT_GROUPS = 4
EXPERTS_PER_GROUP = 8
N_EXPERTS = N_EXPERT_GROUPS * EXPERTS_PER_GROUP
TOP_K = 2
D_EXPERT = 512
MOE_BLOCK = 128
EPS = 1e-6
IN_SPLIT_SIZES = (3 * GROUP_W, GROUP_W, 2 * N_HEADS, 2 * N_HEADS, 3 * GROUP_W,
                  GROUP_W, 2 * KV_W, GROUP_W, 2 * KV_W)
D_IN = sum(IN_SPLIT_SIZES)

kernel_name = 'hybrid_parallel_heads_encoder'


def _split_points():
    return [int(v) for v in np.cumsum(IN_SPLIT_SIZES)[:-1]]


def rmsnorm(x, g):
    xf = x.astype(jnp.float32)
    y = xf * lax.rsqrt(jnp.mean(xf * xf, axis=-1, keepdims=True) + EPS)
    return (y * g.astype(jnp.float32)).astype(x.dtype)


def l2norm(x):
    return x * lax.rsqrt(jnp.sum(x * x, axis=-1, keepdims=True) + EPS)


def short_conv(u, w):
    k_w = w.shape[0]
    p = k_w // 2
    s_len = u.shape[1]
    up = jnp.pad(u, ((0, 0), (p, p), (0, 0)))
    out = up[:, 0:s_len] * w[0]
    for j in range(1, k_w):
        out = out + up[:, j:j + s_len] * w[j]
    return out


def gdn_chunked(q, k, v, beta, g):
    b_, s_, h_, dk = q.shape
    dv = v.shape[-1]
    c = GDN_CHUNK
    n = s_ // c

    def chunks(t):
        return t.reshape(b_, n, c, h_, -1).transpose(0, 3, 1, 2, 4)

    q = chunks(q) * (dk ** -0.5)
    k = chunks(k)
    v = chunks(v)
    beta = beta.reshape(b_, n, c, h_).transpose(0, 3, 1, 2)
    gc = jnp.cumsum(g.reshape(b_, n, c, h_).transpose(0, 3, 1, 2), axis=-1)
    idx = jnp.arange(c)
    incl = idx[:, None] >= idx[None, :]
    strict = idx[:, None] > idx[None, :]
    decay = jnp.exp(jnp.where(incl, gc[..., :, None] - gc[..., None, :], -jnp.inf))
    kb = k * beta[..., None]
    a_strict = jnp.where(strict, jnp.einsum('bhnid,bhnjd->bhnij', kb, k) * decay, 0.0)
    rhs = jnp.concatenate([v * beta[..., None], kb * jnp.exp(gc)[..., None]], axis=-1)
    sol = lax.linalg.triangular_solve(a_strict, rhs, left_side=True, lower=True,
                                      unit_diagonal=True)
    value, k_cumdecay = sol[..., :dv], sol[..., dv:]
    attn = jnp.einsum('bhnid,bhnjd->bhnij', q, k) * decay
    q_dec = q * jnp.exp(gc)[..., None]
    k_dec = k * jnp.exp(gc[..., -1:] - gc)[..., None]
    chunk_dec = jnp.exp(gc[..., -1])

    def step(state, xs):
        value_n, kcd_n, attn_n, qd_n, kd_n, cd_n = xs
        v_new = value_n - jnp.einsum('bhck,bhkv->bhcv', kcd_n, state)
        o = (jnp.einsum('bhck,bhkv->bhcv', qd_n, state)
             + jnp.einsum('bhij,bhjv->bhiv', attn_n, v_new))
        state = state * cd_n[..., None, None] + jnp.einsum('bhck,bhcv->bhkv', kd_n, v_new)
        return state, o

    xs = tuple(jnp.moveaxis(t, 2, 0) for t in (value, k_cumdecay, attn, q_dec, k_dec, chunk_dec))
    state0 = jnp.zeros((b_, h_, dk, dv), jnp.float32)
    _, o = lax.scan(step, state0, xs)
    return o.transpose(1, 0, 3, 2, 4).reshape(b_, s_, h_, dv)


def gdn_mixer(qkv, z, b_in, a_in, conv_w, a_log, dt_bias, norm_g):
    b_, s_, _ = qkv.shape
    dtype = qkv.dtype
    qkv = jax.nn.silu(short_conv(qkv, conv_w)).astype(jnp.float32)
    q, k, v = jnp.split(qkv, 3, axis=-1)
    q = l2norm(q.reshape(b_, s_, N_HEADS, HEAD_DIM))
    k = l2norm(k.reshape(b_, s_, N_HEADS, HEAD_DIM))
    v = v.reshape(b_, s_, N_HEADS, HEAD_DIM)
    b_in = b_in.astype(jnp.float32).reshape(b_, s_, 2, N_HEADS)
    a_in = a_in.astype(jnp.float32).reshape(b_, s_, 2, N_HEADS)
    beta = jax.nn.sigmoid(b_in)
    g = -jnp.exp(a_log.astype(jnp.float32)) * jax.nn.softplus(a_in + dt_bias.astype(jnp.float32))
    flip = lambda t: jnp.flip(t, axis=1)
    o_fwd = gdn_chunked(q, k, v, beta[:, :, 0], g[:, :, 0])
    o_bwd = flip(gdn_chunked(flip(q), flip(k), flip(v), flip(beta[:, :, 1]), flip(g[:, :, 1])))
    o = o_fwd + o_bwd
    zg = jax.nn.silu(z.astype(jnp.float32)).reshape(b_, s_, N_HEADS, HEAD_DIM)
    o = rmsnorm(o, norm_g) * zg
    return o.reshape(b_, s_, GROUP_W).astype(dtype)


def hyena_pos_features(length):
    t = jnp.linspace(0.0, 1.0, length, dtype=jnp.float32)[:, None]
    w = 2.0 * math.pi * jnp.arange(length, dtype=jnp.float32) / length
    f = jnp.linspace(1e-4, HY_BANDS - 1, HY_BANDS, dtype=jnp.float32)
    fw = w[:, None] * f[None, :]
    return jnp.concatenate([t, jnp.cos(fw), -jnp.sin(fw)], axis=-1)


def hyena_filters_f(z, w1, b1, f1, w2, b2, f2, w3, deltas):
    f32 = jnp.float32
    length = z.shape[0]
    t = z[:, :1]
    h = jnp.sin(f1.astype(f32) * (z @ w1.astype(f32) + b1.astype(f32)))
    h = jnp.sin(f2.astype(f32) * (h @ w2.astype(f32) + b2.astype(f32)))
    h = (h @ w3.astype(f32)) * jnp.exp(-t * jnp.abs(deltas.astype(f32)))
    h = h.reshape(length, HY_ORDER, 2, GROUP_W)
    kern = jnp.concatenate([h[:, :, 0], jnp.zeros((1, HY_ORDER, GROUP_W), f32),
                            h[:0:-1, :, 1]], axis=0)
    kern = kern / (jnp.sum(jnp.abs(kern), axis=0, keepdims=True) + EPS)
    return jnp.fft.rfft(kern, axis=0)


def fft_conv(u, kf, bias):
    length = u.shape[1]
    uf = jnp.fft.rfft(u, n=2 * length, axis=1)
    y = jnp.fft.irfft(uf * kf[None], n=2 * length, axis=1)[:, :length]
    return y + u * bias


def hyena_mixer(u, conv_w, kf, bias, norm_g):
    dtype = u.dtype
    u = short_conv(u, conv_w).astype(jnp.float32)
    x1, x2, v = jnp.split(u, 3, axis=-1)
    bias = bias.astype(jnp.float32)
    y = x1 * fft_conv(v, kf[:, 0], bias[0])
    y = x2 * fft_conv(y, kf[:, 1], bias[1])
    return rmsnorm(y.astype(dtype), norm_g)


def alibi_slopes(n):
    return 2.0 ** (-8.0 * jnp.arange(1, n + 1, dtype=jnp.float32) / n)


def window_mixer(c_q, c_kv, sink, slopes, norm_g):
    b_, s_, _ = c_q.shape
    nb = s_ // BLOCK
    q = c_q.reshape(b_, nb, BLOCK, N_KV_HEADS, GQA_GROUP, HEAD_DIM)
    k, v = jnp.split(c_kv, 2, axis=-1)

    def band(t):
        tp = jnp.pad(t.reshape(b_, s_, N_KV_HEADS, HEAD_DIM), ((0, 0), (BLOCK, BLOCK), (0, 0), (0, 0)))
        tp = tp.reshape(b_, nb + 2, BLOCK, N_KV_HEADS, HEAD_DIM)
        return jnp.concatenate([tp[:, :-2], tp[:, 1:-1], tp[:, 2:]], axis=2)

    kb, vb = band(k), band(v)
    s = jnp.einsum('bnqkgd,bnskd->bnkgqs', q, kb).astype(jnp.float32) * (HEAD_DIM ** -0.5)
    rel = BLOCK + jnp.arange(BLOCK)[:, None] - jnp.arange(3 * BLOCK)[None, :]
    key_pos = jnp.arange(nb)[:, None] * BLOCK - BLOCK + jnp.arange(3 * BLOCK)[None, :]
    valid = (jnp.abs(rel) <= WINDOW)[None] & ((key_pos >= 0) & (key_pos < s_))[:, None, :]
    dist = jnp.abs(rel).astype(jnp.float32)
    logits = s - slopes.reshape(N_KV_HEADS, GQA_GROUP)[None, None, :, :, None, None] * dist
    logits = jnp.where(valid[None, :, None, None], logits, -jnp.inf)
    sink_col = jnp.broadcast_to(
        sink.astype(jnp.float32).reshape(N_KV_HEADS, GQA_GROUP)[None, None, :, :, None, None],
        logits.shape[:-1] + (1,))
    p = jax.nn.softmax(jnp.concatenate([logits, sink_col], axis=-1), axis=-1)[..., :-1]
    o = jnp.einsum('bnkgqs,bnskd->bnqkgd', p.astype(c_q.dtype), vb)
    return rmsnorm(o.reshape(b_, s_, GROUP_W), norm_g)


def rope_1d(x, pos):
    d = x.shape[-1]
    inv = ROPE_THETA ** (-jnp.arange(0, d, 2, dtype=jnp.float32) / d)
    ang = pos.astype(jnp.float32)[:, None] * inv[None, :]
    cos = jnp.cos(ang)[None, :, None, :]
    sin = jnp.sin(ang)[None, :, None, :]
    xf = x.astype(jnp.float32)
    x1, x2 = xf[..., :d // 2], xf[..., d // 2:]
    return jnp.concatenate([x1 * cos - x2 * sin, x2 * cos + x1 * sin], axis=-1).astype(x.dtype)


def axial_rope(x, row_idx, col_idx):
    half = HEAD_DIM // 2
    return jnp.concatenate([rope_1d(x[..., :half], row_idx), rope_1d(x[..., half:], col_idx)], axis=-1)


def global_mixer(d_q, d_kv, q_norm_g, k_norm_g, row_idx, col_idx, norm_g):
    b_, s_, _ = d_q.shape
    nb = s_ // BLOCK
    q = d_q.reshape(b_, s_, N_HEADS, HEAD_DIM)
    k, v = jnp.split(d_kv, 2, axis=-1)
    k = k.reshape(b_, s_, N_KV_HEADS, HEAD_DIM)
    v = v.reshape(b_, s_, N_KV_HEADS, HEAD_DIM)
    q = axial_rope(rmsnorm(q, q_norm_g), row_idx, col_idx)
    k = axial_rope(rmsnorm(k, k_norm_g), row_idx, col_idx)
    qb = q.reshape(b_, nb, BLOCK, N_KV_HEADS, GQA_GROUP, HEAD_DIM).transpose(1, 0, 3, 4, 2, 5)
    kt = k.transpose(0, 2, 1, 3)
    vt = v.transpose(0, 2, 1, 3)

    def one_block(q_blk):
        s = jnp.einsum('bkgqd,bksd->bkgqs', q_blk, kt).astype(jnp.float32) * (HEAD_DIM ** -0.5)
        p = jax.nn.softmax(s, axis=-1)
        return jnp.einsum('bkgqs,bksd->bkgqd', p.astype(vt.dtype), vt)

    o = lax.map(one_block, qb)
    o = o.transpose(1, 0, 4, 2, 3, 5).reshape(b_, s_, GROUP_W)
    return rmsnorm(o, norm_g)


def hier_moe(x, w_group, b_group, w_expert, b_expert, w_gate, w_up, w_down):
    b_, s_, d_ = x.shape
    n_tok = b_ * s_
    xt = x.reshape(n_tok, d_)
    gp = jax.nn.softmax((xt @ w_group).astype(jnp.float32) + b_group.astype(jnp.float32), axis=-1)
    g_w, g_idx = lax.top_k(gp, 1)
    elog = (xt @ w_expert).astype(jnp.float32) + b_expert.astype(jnp.float32)
    elog = elog.reshape(n_tok, N_EXPERT_GROUPS, EXPERTS_PER_GROUP)
    elog_sel = elog[jnp.arange(n_tok), g_idx[:, 0]]
    e_w, e_idx = lax.top_k(jax.nn.softmax(elog_sel, axis=-1), TOP_K)
    e_w = e_w / jnp.sum(e_w, axis=-1, keepdims=True)
    weights = (g_w * e_w).reshape(-1)
    experts = (g_idx * EXPERTS_PER_GROUP + e_idx).reshape(-1)
    tokens = jnp.repeat(jnp.arange(n_tok), TOP_K)
    n_assign = n_tok * TOP_K
    order = jnp.argsort(experts)
    e_s, w_s, tok_s = experts[order], weights[order], tokens[order]
    counts = jnp.bincount(experts, length=N_EXPERTS)
    start = jnp.cumsum(counts) - counts
    padded = ((counts + MOE_BLOCK - 1) // MOE_BLOCK) * MOE_BLOCK
    pend = jnp.cumsum(padded)
    pstart = pend - padded
    pos = pstart[e_s] + jnp.arange(n_assign) - start[e_s]
    n_pad = n_assign + N_EXPERTS * MOE_BLOCK
    n_blk = n_pad // MOE_BLOCK
    tok_pad = jnp.zeros((n_pad,), jnp.int32).at[pos].set(tok_s.astype(jnp.int32))
    w_pad = jnp.zeros((n_pad,), jnp.float32).at[pos].set(w_s)
    blk_e = jnp.minimum(jnp.searchsorted(pend, jnp.arange(n_blk) * MOE_BLOCK, side='right'),
                        N_EXPERTS - 1)

    def run_block(args):
        tok_b, w_b, e = args
        xb = xt[tok_b]
        h = jax.nn.silu(xb @ w_gate[e]) * (xb @ w_up[e])
        out = h @ w_down[e]
        return out * w_b[:, None].astype(out.dtype)

    out = lax.map(run_block, (tok_pad.reshape(n_blk, MOE_BLOCK), w_pad.reshape(n_blk, MOE_BLOCK), blk_e))
    y = jnp.zeros((n_tok, d_), x.dtype).at[tok_pad].add(out.reshape(n_pad, d_).astype(x.dtype))
    return y.reshape(b_, s_, d_)


def setup_inputs(seed: int = 0) -> dict:
    key = jax.random.key(seed)
    ks = iter(jax.random.split(key, 40))
    f32 = jnp.float32

    def nrm(shape, scale):
        return scale * jax.random.normal(next(ks), shape, f32)

    def gain(shape):
        return 1.0 + nrm(shape, 0.01)

    dt = jnp.exp(jax.random.uniform(next(ks), (DEPTH, 2, N_HEADS), f32, math.log(1e-3), math.log(1e-1)))
    a_init = jax.random.uniform(next(ks), (DEPTH, 2, N_HEADS), f32, 1.0, 16.0)
    decay_lin = jnp.linspace(-math.log(HY_DECAY_TARGET) / HY_SLOW_DECAY,
                             -math.log(HY_DECAY_TARGET) / HY_FAST_DECAY,
                             HY_ORDER * 2 * GROUP_W, dtype=f32)
    return {
        'x': nrm((BATCH, SEQ, D_MODEL), 1.0),
        'norm_mix': gain((DEPTH, D_MODEL)),
        'w_in': nrm((DEPTH, D_MODEL, D_IN), D_MODEL ** -0.5),
        'gdn_conv': nrm((DEPTH, SHORT_CONV, 3 * GROUP_W), SHORT_CONV ** -0.5),
        'gdn_a_log': jnp.log(a_init),
        'gdn_dt_bias': dt + jnp.log(-jnp.expm1(-dt)),
        'gdn_norm': gain((DEPTH, HEAD_DIM)),
        'hy_conv': nrm((DEPTH, SHORT_CONV, 3 * GROUP_W), SHORT_CONV ** -0.5),
        'hy_w1': nrm((DEPTH, HY_EMB, HY_FFN), HY_EMB ** -0.5),
        'hy_b1': nrm((DEPTH, HY_FFN), 0.1),
        'hy_freq1': gain((DEPTH, HY_FFN)),
        'hy_w2': nrm((DEPTH, HY_FFN, HY_FFN), HY_FFN ** -0.5),
        'hy_b2': nrm((DEPTH, HY_FFN), 0.1),
        'hy_freq2': gain((DEPTH, HY_FFN)),
        'hy_w3': nrm((DEPTH, HY_FFN, HY_ORDER * 2 * GROUP_W), HY_FFN ** -0.5),
        'hy_deltas': decay_lin[None] * (1.0 + nrm((DEPTH, HY_ORDER * 2 * GROUP_W), 0.01)),
        'hy_bias': nrm((DEPTH, HY_ORDER, GROUP_W), 1.0),
        'hy_norm': gain((DEPTH, GROUP_W)),
        'swa_sink': nrm((DEPTH, N_HEADS), 0.5),
        'swa_norm': gain((DEPTH, GROUP_W)),
        'ga_q_norm': gain((DEPTH, HEAD_DIM)),
        'ga_k_norm': gain((DEPTH, HEAD_DIM)),
        'ga_norm': gain((DEPTH, GROUP_W)),
        'w_out': nrm((DEPTH, D_MODEL, D_MODEL), D_MODEL ** -0.5),
        'norm_ffn': gain((DEPTH, D_MODEL)),
        'moe_w_group': nrm((DEPTH, D_MODEL, N_EXPERT_GROUPS), D_MODEL ** -0.5),
        'moe_b_group': nrm((DEPTH, N_EXPERT_GROUPS), 0.01),
        'moe_w_expert': nrm((DEPTH, D_MODEL, N_EXPERTS), D_MODEL ** -0.5),
        'moe_b_expert': nrm((DEPTH, N_EXPERTS), 0.01),
        'moe_w_gate': nrm((DEPTH, N_EXPERTS, D_MODEL, D_EXPERT), D_MODEL ** -0.5),
        'moe_w_up': nrm((DEPTH, N_EXPERTS, D_MODEL, D_EXPERT), D_MODEL ** -0.5),
        'moe_w_down': nrm((DEPTH, N_EXPERTS, D_EXPERT, D_MODEL), D_EXPERT ** -0.5),
        'norm_final': gain((D_MODEL,)),
    }


def reference(x, norm_mix, w_in, gdn_conv, gdn_a_log, gdn_dt_bias, gdn_norm, hy_conv, hy_w1,
              hy_b1, hy_freq1, hy_w2, hy_b2, hy_freq2, hy_w3, hy_deltas, hy_bias, hy_norm,
              swa_sink, swa_norm, ga_q_norm, ga_k_norm, ga_norm, w_out, norm_ffn, moe_w_group,
              moe_b_group, moe_w_expert, moe_b_expert, moe_w_gate, moe_w_up, moe_w_down,
              norm_final):
    b_, s_, _ = x.shape
    rows = s_ // GRID_W
    row_idx = jnp.repeat(jnp.arange(rows), GRID_W)
    col_idx = jnp.tile(jnp.arange(GRID_W), rows)
    pos_feat = hyena_pos_features(s_)
    slopes = alibi_slopes(N_HEADS)
    splits = _split_points()
    for l in range(DEPTH):
        xn = rmsnorm(x, norm_mix[l])
        proj = xn @ w_in[l]
        a_qkv, a_z, a_beta, a_alpha, b_in, c_q, c_kv, d_q, d_kv = jnp.split(proj, splits, axis=-1)
        y_a = gdn_mixer(a_qkv, a_z, a_beta, a_alpha, gdn_conv[l], gdn_a_log[l], gdn_dt_bias[l], gdn_norm[l])
        kf = hyena_filters_f(pos_feat, hy_w1[l], hy_b1[l], hy_freq1[l], hy_w2[l], hy_b2[l],
                             hy_freq2[l], hy_w3[l], hy_deltas[l])
        y_b = hyena_mixer(b_in, hy_conv[l], kf, hy_bias[l], hy_norm[l])
        y_c = window_mixer(c_q, c_kv, swa_sink[l], slopes, swa_norm[l])
        y_d = global_mixer(d_q, d_kv, ga_q_norm[l], ga_k_norm[l], row_idx, col_idx, ga_norm[l])
        x = x + jnp.concatenate([y_a, y_b, y_c, y_d], axis=-1) @ w_out[l]
        x = x + hier_moe(rmsnorm(x, norm_ffn[l]), moe_w_group[l], moe_b_group[l], moe_w_expert[l],
                         moe_b_expert[l], moe_w_gate[l], moe_w_up[l], moe_w_down[l])
    return rmsnorm(x, norm_final)
```

```python
import functools
import math

import jax
import jax.numpy as jnp
import numpy as np
from jax import lax
from jax.experimental import pallas as pl
from jax.experimental.pallas import tpu as pltpu

D_MODEL = 2048
DEPTH = 2
N_MIXERS = 4
GROUP_W = D_MODEL // N_MIXERS
HEAD_DIM = 64
N_HEADS = GROUP_W // HEAD_DIM
N_KV_HEADS = 2
GQA_GROUP = N_HEADS // N_KV_HEADS
KV_W = N_KV_HEADS * HEAD_DIM
SHORT_CONV = 3
GDN_CHUNK = 64
HY_ORDER = 2
HY_EMB = 33
HY_BANDS = (HY_EMB - 1) // 2
WINDOW = 128
BLOCK = 128
GRID_W = 64
ROPE_THETA = 10000.0
N_EXPERT_GROUPS = 4
EXPERTS_PER_GROUP = 8
N_EXPERTS = N_EXPERT_GROUPS * EXPERTS_PER_GROUP
TOP_K = 2
D_EXPERT = 512
MOE_BLOCK = 128
EPS = 1e-6
IN_SPLIT_SIZES = (3 * GROUP_W, GROUP_W, 2 * N_HEADS, 2 * N_HEADS, 3 * GROUP_W,
                  GROUP_W, 2 * KV_W, GROUP_W, 2 * KV_W)

ATTN_VMEM_LIMIT_BYTES = 48 * 1024 * 1024


def _split_points():
    return [int(v) for v in np.cumsum(IN_SPLIT_SIZES)[:-1]]


def rmsnorm(x, g):
    xf = x.astype(jnp.float32)
    y = xf * lax.rsqrt(jnp.mean(xf * xf, axis=-1, keepdims=True) + EPS)
    return (y * g.astype(jnp.float32)).astype(x.dtype)


def l2norm(x):
    return x * lax.rsqrt(jnp.sum(x * x, axis=-1, keepdims=True) + EPS)


def short_conv(u, w):
    k_w = w.shape[0]
    p = k_w // 2
    s_len = u.shape[1]
    up = jnp.pad(u, ((0, 0), (p, p), (0, 0)))
    out = up[:, 0:s_len] * w[0]
    for j in range(1, k_w):
        out = out + up[:, j:j + s_len] * w[j]
    return out


def gdn_chunked(q, k, v, beta, g):
    b_, s_, h_, dk = q.shape
    dv = v.shape[-1]
    c = GDN_CHUNK
    n = s_ // c

    def chunks(t):
        return t.reshape(b_, n, c, h_, -1).transpose(0, 3, 1, 2, 4)

    q = chunks(q) * (dk ** -0.5)
    k = chunks(k)
    v = chunks(v)
    beta = beta.reshape(b_, n, c, h_).transpose(0, 3, 1, 2)
    gc = jnp.cumsum(g.reshape(b_, n, c, h_).transpose(0, 3, 1, 2), axis=-1)
    idx = jnp.arange(c)
    incl = idx[:, None] >= idx[None, :]
    strict = idx[:, None] > idx[None, :]
    decay = jnp.exp(jnp.where(incl, gc[..., :, None] - gc[..., None, :], -jnp.inf))
    kb = k * beta[..., None]
    a_strict = jnp.where(strict, jnp.einsum('bhnid,bhnjd->bhnij', kb, k) * decay, 0.0)
    rhs = jnp.concatenate([v * beta[..., None], kb * jnp.exp(gc)[..., None]], axis=-1)
    sol = lax.linalg.triangular_solve(a_strict, rhs, left_side=True, lower=True,
                                      unit_diagonal=True)
    value, k_cumdecay = sol[..., :dv], sol[..., dv:]
    attn = jnp.einsum('bhnid,bhnjd->bhnij', q, k) * decay
    q_dec = q * jnp.exp(gc)[..., None]
    k_dec = k * jnp.exp(gc[..., -1:] - gc)[..., None]
    chunk_dec = jnp.exp(gc[..., -1])

    def step(state, xs):
        value_n, kcd_n, attn_n, qd_n, kd_n, cd_n = xs
        v_new = value_n - jnp.einsum('bhck,bhkv->bhcv', kcd_n, state)
        o = (jnp.einsum('bhck,bhkv->bhcv', qd_n, state)
             + jnp.einsum('bhij,bhjv->bhiv', attn_n, v_new))
        state = state * cd_n[..., None, None] + jnp.einsum('bhck,bhcv->bhkv', kd_n, v_new)
        return state, o

    xs = tuple(jnp.moveaxis(t, 2, 0) for t in (value, k_cumdecay, attn, q_dec, k_dec, chunk_dec))
    state0 = jnp.zeros((b_, h_, dk, dv), jnp.float32)
    _, o = lax.scan(step, state0, xs)
    return o.transpose(1, 0, 3, 2, 4).reshape(b_, s_, h_, dv)


def gdn_mixer(qkv, z, b_in, a_in, conv_w, a_log, dt_bias, norm_g):
    b_, s_, _ = qkv.shape
    dtype = qkv.dtype
    qkv = jax.nn.silu(short_conv(qkv, conv_w)).astype(jnp.float32)
    q, k, v = jnp.split(qkv, 3, axis=-1)
    q = l2norm(q.reshape(b_, s_, N_HEADS, HEAD_DIM))
    k = l2norm(k.reshape(b_, s_, N_HEADS, HEAD_DIM))
    v = v.reshape(b_, s_, N_HEADS, HEAD_DIM)
    b_in = b_in.astype(jnp.float32).reshape(b_, s_, 2, N_HEADS)
    a_in = a_in.astype(jnp.float32).reshape(b_, s_, 2, N_HEADS)
    beta = jax.nn.sigmoid(b_in)
    g = -jnp.exp(a_log.astype(jnp.float32)) * jax.nn.softplus(a_in + dt_bias.astype(jnp.float32))
    flip = lambda t: jnp.flip(t, axis=1)
    o_fwd = gdn_chunked(q, k, v, beta[:, :, 0], g[:, :, 0])
    o_bwd = flip(gdn_chunked(flip(q), flip(k), flip(v), flip(beta[:, :, 1]), flip(g[:, :, 1])))
    o = o_fwd + o_bwd
    zg = jax.nn.silu(z.astype(jnp.float32)).reshape(b_, s_, N_HEADS, HEAD_DIM)
    o = rmsnorm(o, norm_g) * zg
    return o.reshape(b_, s_, GROUP_W).astype(dtype)


def hyena_pos_features(length):
    t = jnp.linspace(0.0, 1.0, length, dtype=jnp.float32)[:, None]
    w = 2.0 * math.pi * jnp.arange(length, dtype=jnp.float32) / length
    f = jnp.linspace(1e-4, HY_BANDS - 1, HY_BANDS, dtype=jnp.float32)
    fw = w[:, None] * f[None, :]
    return jnp.concatenate([t, jnp.cos(fw), -jnp.sin(fw)], axis=-1)


def hyena_filters_f(z, w1, b1, f1, w2, b2, f2, w3, deltas):
    f32 = jnp.float32
    length = z.shape[0]
    t = z[:, :1]
    h = jnp.sin(f1.astype(f32) * (z @ w1.astype(f32) + b1.astype(f32)))
    h = jnp.sin(f2.astype(f32) * (h @ w2.astype(f32) + b2.astype(f32)))
    h = (h @ w3.astype(f32)) * jnp.exp(-t * jnp.abs(deltas.astype(f32)))
    h = h.reshape(length, HY_ORDER, 2, GROUP_W)
    kern = jnp.concatenate([h[:, :, 0], jnp.zeros((1, HY_ORDER, GROUP_W), f32),
                            h[:0:-1, :, 1]], axis=0)
    kern = kern / (jnp.sum(jnp.abs(kern), axis=0, keepdims=True) + EPS)
    return jnp.fft.rfft(kern, axis=0)


def fft_conv(u, kf, bias):
    length = u.shape[1]
    uf = jnp.fft.rfft(u, n=2 * length, axis=1)
    y = jnp.fft.irfft(uf * kf[None], n=2 * length, axis=1)[:, :length]
    return y + u * bias


def hyena_mixer(u, conv_w, kf, bias, norm_g):
    dtype = u.dtype
    u = short_conv(u, conv_w).astype(jnp.float32)
    x1, x2, v = jnp.split(u, 3, axis=-1)
    bias = bias.astype(jnp.float32)
    y = x1 * fft_conv(v, kf[:, 0], bias[0])
    y = x2 * fft_conv(y, kf[:, 1], bias[1])
    return rmsnorm(y.astype(dtype), norm_g)


def alibi_slopes(n):
    return 2.0 ** (-8.0 * jnp.arange(1, n + 1, dtype=jnp.float32) / n)


def window_mixer(c_q, c_kv, sink, slopes, norm_g):
    b_, s_, _ = c_q.shape
    nb = s_ // BLOCK
    q = c_q.reshape(b_, nb, BLOCK, N_KV_HEADS, GQA_GROUP, HEAD_DIM)
    k, v = jnp.split(c_kv, 2, axis=-1)

    def band(t):
        tp = jnp.pad(t.reshape(b_, s_, N_KV_HEADS, HEAD_DIM), ((0, 0), (BLOCK, BLOCK), (0, 0), (0, 0)))
        tp = tp.reshape(b_, nb + 2, BLOCK, N_KV_HEADS, HEAD_DIM)
        return jnp.concatenate([tp[:, :-2], tp[:, 1:-1], tp[:, 2:]], axis=2)

    kb, vb = band(k), band(v)
    s = jnp.einsum('bnqkgd,bnskd->bnkgqs', q, kb).astype(jnp.float32) * (HEAD_DIM ** -0.5)
    rel = BLOCK + jnp.arange(BLOCK)[:, None] - jnp.arange(3 * BLOCK)[None, :]
    key_pos = jnp.arange(nb)[:, None] * BLOCK - BLOCK + jnp.arange(3 * BLOCK)[None, :]
    valid = (jnp.abs(rel) <= WINDOW)[None] & ((key_pos >= 0) & (key_pos < s_))[:, None, :]
    dist = jnp.abs(rel).astype(jnp.float32)
    logits = s - slopes.reshape(N_KV_HEADS, GQA_GROUP)[None, None, :, :, None, None] * dist
    logits = jnp.where(valid[None, :, None, None], logits, -jnp.inf)
    sink_col = jnp.broadcast_to(
        sink.astype(jnp.float32).reshape(N_KV_HEADS, GQA_GROUP)[None, None, :, :, None, None],
        logits.shape[:-1] + (1,))
    p = jax.nn.softmax(jnp.concatenate([logits, sink_col], axis=-1), axis=-1)[..., :-1]
    o = jnp.einsum('bnkgqs,bnskd->bnqkgd', p.astype(c_q.dtype), vb)
    return rmsnorm(o.reshape(b_, s_, GROUP_W), norm_g)


def rope_1d(x, pos):
    d = x.shape[-1]
    inv = ROPE_THETA ** (-jnp.arange(0, d, 2, dtype=jnp.float32) / d)
    ang = pos.astype(jnp.float32)[:, None] * inv[None, :]
    cos = jnp.cos(ang)[None, :, None, :]
    sin = jnp.sin(ang)[None, :, None, :]
    xf = x.astype(jnp.float32)
    x1, x2 = xf[..., :d // 2], xf[..., d // 2:]
    return jnp.concatenate([x1 * cos - x2 * sin, x2 * cos + x1 * sin], axis=-1).astype(x.dtype)


def axial_rope(x, row_idx, col_idx):
    half = HEAD_DIM // 2
    return jnp.concatenate([rope_1d(x[..., :half], row_idx), rope_1d(x[..., half:], col_idx)], axis=-1)


def _flash_kernel(q_ref, kt_ref, v_ref, o_ref, *, tq, tk, n_kc):
    qa = q_ref[0]
    q = jnp.concatenate([qa[:, g * HEAD_DIM:(g + 1) * HEAD_DIM] for g in range(GQA_GROUP)], axis=0)
    m_rows = GQA_GROUP * tq

    def body(c, carry):
        m, l, acc = carry
        off = pl.multiple_of(c * tk, tk)
        kt = kt_ref[0, 0, :, pl.ds(off, tk)]
        v = v_ref[0, 0, pl.ds(off, tk), :]
        s = jnp.dot(q, kt, preferred_element_type=jnp.float32)
        m_new = jnp.maximum(m, jnp.max(s, axis=-1, keepdims=True))
        alpha = jnp.exp(m - m_new)
        p = jnp.exp(s - m_new)
        l = l * alpha + jnp.sum(p, axis=-1, keepdims=True)
        acc = acc * alpha + jnp.dot(p.astype(jnp.bfloat16), v, preferred_element_type=jnp.float32)
        return m_new, l, acc

    m0 = jnp.full((m_rows, 1), -jnp.inf, jnp.float32)
    l0 = jnp.zeros((m_rows, 1), jnp.float32)
    acc0 = jnp.zeros((m_rows, HEAD_DIM), jnp.float32)
    m, l, acc = lax.fori_loop(0, n_kc, body, (m0, l0, acc0))
    o = acc / l
    o_ref[0] = jnp.concatenate([o[g * tq:(g + 1) * tq] for g in range(GQA_GROUP)], axis=-1)


def _flash_attention(q, kt, v, *, tq=128, tk=512):
    b_, s_, _ = q.shape
    gw = GQA_GROUP * HEAD_DIM
    kern = functools.partial(_flash_kernel, tq=tq, tk=tk, n_kc=s_ // tk)
    return pl.pallas_call(
        kern,
        out_shape=jax.ShapeDtypeStruct((b_, s_, N_HEADS * HEAD_DIM), jnp.float32),
        grid=(b_, N_KV_HEADS, s_ // tq),
        in_specs=[
            pl.BlockSpec((1, tq, gw), lambda b, h, i: (b, i, h)),
            pl.BlockSpec((1, 1, HEAD_DIM, s_), lambda b, h, i: (b, h, 0, 0)),
            pl.BlockSpec((1, 1, s_, HEAD_DIM), lambda b, h, i: (b, h, 0, 0)),
        ],
        out_specs=pl.BlockSpec((1, tq, gw), lambda b, h, i: (b, i, h)),
        compiler_params=pltpu.CompilerParams(
            dimension_semantics=("arbitrary", "arbitrary", "arbitrary"),
            vmem_limit_bytes=ATTN_VMEM_LIMIT_BYTES),
        name="global_flash_attention",
    )(q, kt, v)


def global_mixer(d_q, d_kv, q_norm_g, k_norm_g, row_idx, col_idx, norm_g):
    b_, s_, _ = d_q.shape
    q = d_q.reshape(b_, s_, N_HEADS, HEAD_DIM)
    k, v = jnp.split(d_kv, 2, axis=-1)
    k = k.reshape(b_, s_, N_KV_HEADS, HEAD_DIM)
    v = v.reshape(b_, s_, N_KV_HEADS, HEAD_DIM)
    q = axial_rope(rmsnorm(q, q_norm_g), row_idx, col_idx) * (HEAD_DIM ** -0.5)
    k = axial_rope(rmsnorm(k, k_norm_g), row_idx, col_idx)
    qb = q.reshape(b_, s_, GROUP_W).astype(jnp.bfloat16)
    kt = k.transpose(0, 2, 3, 1).astype(jnp.bfloat16)
    vt = v.transpose(0, 2, 1, 3).astype(jnp.bfloat16)
    o = _flash_attention(qb, kt, vt)
    return rmsnorm(o, norm_g)


def hier_moe(x, w_group, b_group, w_expert, b_expert, w_gate, w_up, w_down):
    b_, s_, d_ = x.shape
    n_tok = b_ * s_
    xt = x.reshape(n_tok, d_)
    gp = jax.nn.softmax((xt @ w_group).astype(jnp.float32) + b_group.astype(jnp.float32), axis=-1)
    g_w, g_idx = lax.top_k(gp, 1)
    elog = (xt @ w_expert).astype(jnp.float32) + b_expert.astype(jnp.float32)
    elog = elog.reshape(n_tok, N_EXPERT_GROUPS, EXPERTS_PER_GROUP)
    elog_sel = elog[jnp.arange(n_tok), g_idx[:, 0]]
    e_w, e_idx = lax.top_k(jax.nn.softmax(elog_sel, axis=-1), TOP_K)
    e_w = e_w / jnp.sum(e_w, axis=-1, keepdims=True)
    weights = (g_w * e_w).reshape(-1)
    experts = (g_idx * EXPERTS_PER_GROUP + e_idx).reshape(-1)
    tokens = jnp.repeat(jnp.arange(n_tok), TOP_K)
    n_assign = n_tok * TOP_K
    order = jnp.argsort(experts)
    e_s, w_s, tok_s = experts[order], weights[order], tokens[order]
    counts = jnp.bincount(experts, length=N_EXPERTS)
    start = jnp.cumsum(counts) - counts
    padded = ((counts + MOE_BLOCK - 1) // MOE_BLOCK) * MOE_BLOCK
    pend = jnp.cumsum(padded)
    pstart = pend - padded
    pos = pstart[e_s] + jnp.arange(n_assign) - start[e_s]
    n_pad = n_assign + N_EXPERTS * MOE_BLOCK
    n_blk = n_pad // MOE_BLOCK
    tok_pad = jnp.zeros((n_pad,), jnp.int32).at[pos].set(tok_s.astype(jnp.int32))
    w_pad = jnp.zeros((n_pad,), jnp.float32).at[pos].set(w_s)
    blk_e = jnp.minimum(jnp.searchsorted(pend, jnp.arange(n_blk) * MOE_BLOCK, side='right'),
                        N_EXPERTS - 1)

    def run_block(args):
        tok_b, w_b, e = args
        xb = xt[tok_b]
        h = jax.nn.silu(xb @ w_gate[e]) * (xb @ w_up[e])
        out = h @ w_down[e]
        return out * w_b[:, None].astype(out.dtype)

    out = lax.map(run_block, (tok_pad.reshape(n_blk, MOE_BLOCK), w_pad.reshape(n_blk, MOE_BLOCK), blk_e))
    y = jnp.zeros((n_tok, d_), x.dtype).at[tok_pad].add(out.reshape(n_pad, d_).astype(x.dtype))
    return y.reshape(b_, s_, d_)


def kernel(x, norm_mix, w_in, gdn_conv, gdn_a_log, gdn_dt_bias, gdn_norm, hy_conv, hy_w1, hy_b1, hy_freq1, hy_w2, hy_b2, hy_freq2, hy_w3, hy_deltas, hy_bias, hy_norm, swa_sink, swa_norm, ga_q_norm, ga_k_norm, ga_norm, w_out, norm_ffn, moe_w_group, moe_b_group, moe_w_expert, moe_b_expert, moe_w_gate, moe_w_up, moe_w_down, norm_final):
    b_, s_, _ = x.shape
    rows = s_ // GRID_W
    row_idx = jnp.repeat(jnp.arange(rows), GRID_W)
    col_idx = jnp.tile(jnp.arange(GRID_W), rows)
    pos_feat = hyena_pos_features(s_)
    slopes = alibi_slopes(N_HEADS)
    splits = _split_points()
    for l in range(DEPTH):
        xn = rmsnorm(x, norm_mix[l])
        proj = xn @ w_in[l]
        a_qkv, a_z, a_beta, a_alpha, b_in, c_q, c_kv, d_q, d_kv = jnp.split(proj, splits, axis=-1)
        y_a = gdn_mixer(a_qkv, a_z, a_beta, a_alpha, gdn_conv[l], gdn_a_log[l], gdn_dt_bias[l], gdn_norm[l])
        kf = hyena_filters_f(pos_feat, hy_w1[l], hy_b1[l], hy_freq1[l], hy_w2[l], hy_b2[l],
                             hy_freq2[l], hy_w3[l], hy_deltas[l])
        y_b = hyena_mixer(b_in, hy_conv[l], kf, hy_bias[l], hy_norm[l])
        y_c = window_mixer(c_q, c_kv, swa_sink[l], slopes, swa_norm[l])
        y_d = global_mixer(d_q, d_kv, ga_q_norm[l], ga_k_norm[l], row_idx, col_idx, ga_norm[l])
        x = x + jnp.concatenate([y_a, y_b, y_c, y_d], axis=-1) @ w_out[l]
        x = x + hier_moe(rmsnorm(x, norm_ffn[l]), moe_w_group[l], moe_b_group[l], moe_w_expert[l],
                         moe_b_expert[l], moe_w_gate[l], moe_w_up[l], moe_w_down[l])
    return rmsnorm(x, norm_final)
```

```python
import functools
import math

import jax
import jax.numpy as jnp
import numpy as np
from jax import lax
from jax.experimental import pallas as pl
from jax.experimental.pallas import tpu as pltpu

D_MODEL = 2048
DEPTH = 2
N_MIXERS = 4
GROUP_W = D_MODEL // N_MIXERS
HEAD_DIM = 64
N_HEADS = GROUP_W // HEAD_DIM
N_KV_HEADS = 2
GQA_GROUP = N_HEADS // N_KV_HEADS
KV_W = N_KV_HEADS * HEAD_DIM
SHORT_CONV = 3
GDN_CHUNK = 64
HY_ORDER = 2
HY_EMB = 33
HY_BANDS = (HY_EMB - 1) // 2
WINDOW = 128
BLOCK = 128
GRID_W = 64
ROPE_THETA = 10000.0
N_EXPERT_GROUPS = 4
EXPERTS_PER_GROUP = 8
N_EXPERTS = N_EXPERT_GROUPS * EXPERTS_PER_GROUP
TOP_K = 2
D_EXPERT = 512
MOE_BLOCK = 128
EPS = 1e-6
IN_SPLIT_SIZES = (3 * GROUP_W, GROUP_W, 2 * N_HEADS, 2 * N_HEADS, 3 * GROUP_W,
                  GROUP_W, 2 * KV_W, GROUP_W, 2 * KV_W)

ATTN_VMEM_LIMIT_BYTES = 48 * 1024 * 1024
ATTN_TQ = 256
ATTN_TK = 2048
MOE_VMEM_LIMIT_BYTES = 48 * 1024 * 1024
MOE_TM = 512
MOE_TB = 256
LOG2E = 1.4426950408889634
ATTN_SAFE_SHIFT = 50.0


def _split_points():
    return [int(v) for v in np.cumsum(IN_SPLIT_SIZES)[:-1]]


def rmsnorm(x, g):
    xf = x.astype(jnp.float32)
    y = xf * lax.rsqrt(jnp.mean(xf * xf, axis=-1, keepdims=True) + EPS)
    return (y * g.astype(jnp.float32)).astype(x.dtype)


def l2norm(x):
    return x * lax.rsqrt(jnp.sum(x * x, axis=-1, keepdims=True) + EPS)


def short_conv(u, w):
    k_w = w.shape[0]
    p = k_w // 2
    s_len = u.shape[1]
    up = jnp.pad(u, ((0, 0), (p, p), (0, 0)))
    out = up[:, 0:s_len] * w[0]
    for j in range(1, k_w):
        out = out + up[:, j:j + s_len] * w[j]
    return out


def gdn_chunked(q, k, v, beta, g):
    b_, s_, h_, dk = q.shape
    dv = v.shape[-1]
    c = GDN_CHUNK
    n = s_ // c

    def chunks(t):
        return t.reshape(b_, n, c, h_, -1).transpose(0, 3, 1, 2, 4)

    q = chunks(q) * (dk ** -0.5)
    k = chunks(k)
    v = chunks(v)
    beta = beta.reshape(b_, n, c, h_).transpose(0, 3, 1, 2)
    gc = jnp.cumsum(g.reshape(b_, n, c, h_).transpose(0, 3, 1, 2), axis=-1)
    idx = jnp.arange(c)
    incl = idx[:, None] >= idx[None, :]
    strict = idx[:, None] > idx[None, :]
    decay = jnp.exp(jnp.where(incl, gc[..., :, None] - gc[..., None, :], -jnp.inf))
    kb = k * beta[..., None]
    a_strict = jnp.where(strict, jnp.einsum('bhnid,bhnjd->bhnij', kb, k) * decay, 0.0)
    rhs = jnp.concatenate([v * beta[..., None], kb * jnp.exp(gc)[..., None]], axis=-1)
    sol = lax.linalg.triangular_solve(a_strict, rhs, left_side=True, lower=True,
                                      unit_diagonal=True)
    value, k_cumdecay = sol[..., :dv], sol[..., dv:]
    attn = jnp.einsum('bhnid,bhnjd->bhnij', q, k) * decay
    q_dec = q * jnp.exp(gc)[..., None]
    k_dec = k * jnp.exp(gc[..., -1:] - gc)[..., None]
    chunk_dec = jnp.exp(gc[..., -1])

    def step(state, xs):
        value_n, kcd_n, attn_n, qd_n, kd_n, cd_n = xs
        v_new = value_n - jnp.einsum('bhck,bhkv->bhcv', kcd_n, state)
        o = (jnp.einsum('bhck,bhkv->bhcv', qd_n, state)
             + jnp.einsum('bhij,bhjv->bhiv', attn_n, v_new))
        state = state * cd_n[..., None, None] + jnp.einsum('bhck,bhcv->bhkv', kd_n, v_new)
        return state, o

    xs = tuple(jnp.moveaxis(t, 2, 0) for t in (value, k_cumdecay, attn, q_dec, k_dec, chunk_dec))
    state0 = jnp.zeros((b_, h_, dk, dv), jnp.float32)
    _, o = lax.scan(step, state0, xs)
    return o.transpose(1, 0, 3, 2, 4).reshape(b_, s_, h_, dv)


def gdn_mixer(qkv, z, b_in, a_in, conv_w, a_log, dt_bias, norm_g):
    b_, s_, _ = qkv.shape
    dtype = qkv.dtype
    qkv = jax.nn.silu(short_conv(qkv, conv_w)).astype(jnp.float32)
    q, k, v = jnp.split(qkv, 3, axis=-1)
    q = l2norm(q.reshape(b_, s_, N_HEADS, HEAD_DIM))
    k = l2norm(k.reshape(b_, s_, N_HEADS, HEAD_DIM))
    v = v.reshape(b_, s_, N_HEADS, HEAD_DIM)
    b_in = b_in.astype(jnp.float32).reshape(b_, s_, 2, N_HEADS)
    a_in = a_in.astype(jnp.float32).reshape(b_, s_, 2, N_HEADS)
    beta = jax.nn.sigmoid(b_in)
    g = -jnp.exp(a_log.astype(jnp.float32)) * jax.nn.softplus(a_in + dt_bias.astype(jnp.float32))
    flip = lambda t: jnp.flip(t, axis=1)
    o_fwd = gdn_chunked(q, k, v, beta[:, :, 0], g[:, :, 0])
    o_bwd = flip(gdn_chunked(flip(q), flip(k), flip(v), flip(beta[:, :, 1]), flip(g[:, :, 1])))
    o = o_fwd + o_bwd
    zg = jax.nn.silu(z.astype(jnp.float32)).reshape(b_, s_, N_HEADS, HEAD_DIM)
    o = rmsnorm(o, norm_g) * zg
    return o.reshape(b_, s_, GROUP_W).astype(dtype)


def hyena_pos_features(length):
    t = jnp.linspace(0.0, 1.0, length, dtype=jnp.float32)[:, None]
    w = 2.0 * math.pi * jnp.arange(length, dtype=jnp.float32) / length
    f = jnp.linspace(1e-4, HY_BANDS - 1, HY_BANDS, dtype=jnp.float32)
    fw = w[:, None] * f[None, :]
    return jnp.concatenate([t, jnp.cos(fw), -jnp.sin(fw)], axis=-1)


def hyena_filters_f(z, w1, b1, f1, w2, b2, f2, w3, deltas):
    f32 = jnp.float32
    length = z.shape[0]
    t = z[:, :1]
    h = jnp.sin(f1.astype(f32) * (z @ w1.astype(f32) + b1.astype(f32)))
    h = jnp.sin(f2.astype(f32) * (h @ w2.astype(f32) + b2.astype(f32)))
    h = (h @ w3.astype(f32)) * jnp.exp(-t * jnp.abs(deltas.astype(f32)))
    h = h.reshape(length, HY_ORDER, 2, GROUP_W)
    kern = jnp.concatenate([h[:, :, 0], jnp.zeros((1, HY_ORDER, GROUP_W), f32),
                            h[:0:-1, :, 1]], axis=0)
    kern = kern / (jnp.sum(jnp.abs(kern), axis=0, keepdims=True) + EPS)
    return jnp.fft.rfft(kern, axis=0)


def fft_conv(u, kf, bias):
    length = u.shape[1]
    uf = jnp.fft.rfft(u, n=2 * length, axis=1)
    y = jnp.fft.irfft(uf * kf[None], n=2 * length, axis=1)[:, :length]
    return y + u * bias


def hyena_mixer(u, conv_w, kf, bias, norm_g):
    dtype = u.dtype
    u = short_conv(u, conv_w).astype(jnp.float32)
    x1, x2, v = jnp.split(u, 3, axis=-1)
    bias = bias.astype(jnp.float32)
    y = x1 * fft_conv(v, kf[:, 0], bias[0])
    y = x2 * fft_conv(y, kf[:, 1], bias[1])
    return rmsnorm(y.astype(dtype), norm_g)


def alibi_slopes(n):
    return 2.0 ** (-8.0 * jnp.arange(1, n + 1, dtype=jnp.float32) / n)


def window_mixer(c_q, c_kv, sink, slopes, norm_g):
    b_, s_, _ = c_q.shape
    nb = s_ // BLOCK
    q = c_q.reshape(b_, nb, BLOCK, N_KV_HEADS, GQA_GROUP, HEAD_DIM)
    k, v = jnp.split(c_kv, 2, axis=-1)

    def band(t):
        tp = jnp.pad(t.reshape(b_, s_, N_KV_HEADS, HEAD_DIM), ((0, 0), (BLOCK, BLOCK), (0, 0), (0, 0)))
        tp = tp.reshape(b_, nb + 2, BLOCK, N_KV_HEADS, HEAD_DIM)
        return jnp.concatenate([tp[:, :-2], tp[:, 1:-1], tp[:, 2:]], axis=2)

    kb, vb = band(k), band(v)
    s = jnp.einsum('bnqkgd,bnskd->bnkgqs', q, kb).astype(jnp.float32) * (HEAD_DIM ** -0.5)
    rel = BLOCK + jnp.arange(BLOCK)[:, None] - jnp.arange(3 * BLOCK)[None, :]
    key_pos = jnp.arange(nb)[:, None] * BLOCK - BLOCK + jnp.arange(3 * BLOCK)[None, :]
    valid = (jnp.abs(rel) <= WINDOW)[None] & ((key_pos >= 0) & (key_pos < s_))[:, None, :]
    dist = jnp.abs(rel).astype(jnp.float32)
    logits = s - slopes.reshape(N_KV_HEADS, GQA_GROUP)[None, None, :, :, None, None] * dist
    logits = jnp.where(valid[None, :, None, None], logits, -jnp.inf)
    sink_col = jnp.broadcast_to(
        sink.astype(jnp.float32).reshape(N_KV_HEADS, GQA_GROUP)[None, None, :, :, None, None],
        logits.shape[:-1] + (1,))
    p = jax.nn.softmax(jnp.concatenate([logits, sink_col], axis=-1), axis=-1)[..., :-1]
    o = jnp.einsum('bnkgqs,bnskd->bnqkgd', p.astype(c_q.dtype), vb)
    return rmsnorm(o.reshape(b_, s_, GROUP_W), norm_g)


def rope_1d(x, pos):
    d = x.shape[-1]
    inv = ROPE_THETA ** (-jnp.arange(0, d, 2, dtype=jnp.float32) / d)
    ang = pos.astype(jnp.float32)[:, None] * inv[None, :]
    cos = jnp.cos(ang)[None, :, None, :]
    sin = jnp.sin(ang)[None, :, None, :]
    xf = x.astype(jnp.float32)
    x1, x2 = xf[..., :d // 2], xf[..., d // 2:]
    return jnp.concatenate([x1 * cos - x2 * sin, x2 * cos + x1 * sin], axis=-1).astype(x.dtype)


def axial_rope(x, row_idx, col_idx):
    half = HEAD_DIM // 2
    return jnp.concatenate([rope_1d(x[..., :half], row_idx), rope_1d(x[..., half:], col_idx)], axis=-1)


def _flash_kernel(q_ref, kt_ref, v_ref, o_ref, *, tq, tk, n_kc):
    m_rows = GQA_GROUP * tq
    q = q_ref[0, 0].reshape(m_rows, 2 * HEAD_DIM)

    def body(c, acc):
        off = pl.multiple_of(c * tk, tk)
        s = jnp.dot(q, kt_ref[0, 0, :, pl.ds(off, tk)], preferred_element_type=jnp.float32)
        p = jnp.exp2(s).astype(jnp.bfloat16)
        return acc + jnp.dot(p, v_ref[0, 0, pl.ds(off, tk), :], preferred_element_type=jnp.float32)

    acc = lax.fori_loop(0, n_kc, body, jnp.zeros((m_rows, 2 * HEAD_DIM), jnp.float32))
    o = acc[:, :HEAD_DIM] / acc[:, HEAD_DIM:HEAD_DIM + 1]
    o_ref[0, 0] = o.reshape(GQA_GROUP, tq, HEAD_DIM)


def _rowmax_kernel(q_ref, kt_ref, m_ref, *, tq, tk, n_kc):
    m_rows = GQA_GROUP * tq
    q = q_ref[0, 0].reshape(m_rows, 2 * HEAD_DIM)

    def body(c, mx):
        off = pl.multiple_of(c * tk, tk)
        s = jnp.dot(q, kt_ref[0, 0, :, pl.ds(off, tk)], preferred_element_type=jnp.float32)
        for j in range(tk // 128):
            mx = jnp.maximum(mx, s[:, j * 128:(j + 1) * 128])
        return mx

    mx = lax.fori_loop(0, n_kc, body, jnp.full((m_rows, 128), -jnp.inf, jnp.float32))
    m_ref[0, 0] = jnp.max(mx, axis=-1, keepdims=True).reshape(GQA_GROUP, tq, 1)


def _attn_call(body, q, kt, v, out_w, name, *, tq=ATTN_TQ, tk=ATTN_TK):
    b_, _, _, s_, _ = q.shape
    wide = 2 * HEAD_DIM
    q_spec = pl.BlockSpec((1, 1, GQA_GROUP, tq, wide), lambda b, h, i: (b, h, 0, i, 0))
    kt_spec = pl.BlockSpec((1, 1, wide, s_), lambda b, h, i: (b, h, 0, 0))
    v_spec = pl.BlockSpec((1, 1, s_, wide), lambda b, h, i: (b, h, 0, 0))
    o_spec = pl.BlockSpec((1, 1, GQA_GROUP, tq, out_w), lambda b, h, i: (b, h, 0, i, 0))
    operands = (q, kt) if v is None else (q, kt, v)
    return pl.pallas_call(
        functools.partial(body, tq=tq, tk=tk, n_kc=s_ // tk),
        out_shape=jax.ShapeDtypeStruct((b_, N_KV_HEADS, GQA_GROUP, s_, out_w), jnp.float32),
        grid=(b_, N_KV_HEADS, s_ // tq),
        in_specs=[q_spec, kt_spec] if v is None else [q_spec, kt_spec, v_spec],
        out_specs=o_spec,
        compiler_params=pltpu.CompilerParams(
            dimension_semantics=("arbitrary", "arbitrary", "arbitrary"),
            vmem_limit_bytes=ATTN_VMEM_LIMIT_BYTES),
        name=name,
    )(*operands)


def global_mixer(d_q, d_kv, q_norm_g, k_norm_g, row_idx, col_idx, norm_g):
    b_, s_, _ = d_q.shape
    q = d_q.reshape(b_, s_, N_HEADS, HEAD_DIM)
    k, v = jnp.split(d_kv, 2, axis=-1)
    k = k.reshape(b_, s_, N_KV_HEADS, HEAD_DIM)
    v = v.reshape(b_, s_, N_KV_HEADS, HEAD_DIM)
    q = axial_rope(rmsnorm(q, q_norm_g), row_idx, col_idx) * (LOG2E * HEAD_DIM ** -0.5)
    k = axial_rope(rmsnorm(k, k_norm_g), row_idx, col_idx)
    bf16 = jnp.bfloat16
    qb = q.astype(bf16).reshape(b_, s_, N_KV_HEADS, GQA_GROUP, HEAD_DIM).transpose(0, 2, 3, 1, 4)
    kb = k.astype(bf16).transpose(0, 2, 1, 3)
    vb = v.astype(bf16).transpose(0, 2, 1, 3)
    qn = jnp.sqrt(jnp.sum(jnp.square(qb.astype(jnp.float32)), axis=-1, keepdims=True))
    kn = jnp.sqrt(jnp.max(jnp.sum(jnp.square(kb.astype(jnp.float32)), axis=-1), axis=-1))
    bound = qn * kn[:, :, None, None, None]
    one = jnp.ones((b_, N_KV_HEADS, s_, 1), bf16)
    pad = jnp.zeros((b_, N_KV_HEADS, s_, HEAD_DIM - 1), bf16)
    kta = jnp.concatenate([kb, one, pad], axis=-1).transpose(0, 1, 3, 2)
    va = jnp.concatenate([vb, one, pad], axis=-1)
    qpad = jnp.zeros(qb.shape[:-1] + (HEAD_DIM - 1,), bf16)

    def with_shift(m):
        return jnp.concatenate([qb, (-m).astype(bf16), qpad], axis=-1)

    m = lax.cond(jnp.max(bound) < ATTN_SAFE_SHIFT, lambda: bound,
                 lambda: _attn_call(_rowmax_kernel, with_shift(jnp.zeros_like(bound)), kta, None, 1,
                                    "global_attention_rowmax"))
    o = _attn_call(_flash_kernel, with_shift(m), kta, va, HEAD_DIM, "global_flash_attention")
    o = o.transpose(0, 3, 1, 2, 4).reshape(b_, s_, GROUP_W)
    return rmsnorm(o, norm_g)


def _row_copy(src, src_row, dst, dst_row, sem):
    return pltpu.make_async_copy(src.at[pl.ds(src_row, 1)], dst.at[pl.ds(dst_row, 1)], sem)


def _moe_dispatch_kernel(pos_ref, x_ref, g_ref, zeros_hbm, xs_hbm, xn_scr, sem, *, tb):
    del zeros_hbm
    i = pl.program_id(0)
    x = x_ref[...]
    xn_scr[...] = x * lax.rsqrt(jnp.mean(x * x, axis=-1, keepdims=True) + EPS) * g_ref[...]

    def issue(r, carry):
        for k in range(TOP_K):
            _row_copy(xn_scr, r, xs_hbm, pos_ref[(i * tb + r) * TOP_K + k], sem).start()
        return carry

    lax.fori_loop(0, tb, issue, 0, unroll=8)
    for _ in range(TOP_K):
        pltpu.make_async_copy(xn_scr, xn_scr, sem).wait()


def _moe_expert_kernel(blk_e_ref, blk_rows_ref, xs_ref, wg_ref, wu_ref, wd_ref, o_ref, *, tm):
    del blk_e_ref
    rows = blk_rows_ref[pl.program_id(0)]

    @pl.when(rows > 0)
    def _():
        x = xs_ref[...].astype(jnp.bfloat16)
        g = jnp.dot(x, wg_ref[0], preferred_element_type=jnp.float32)
        u = jnp.dot(x, wu_ref[0], preferred_element_type=jnp.float32)
        h = (g * jax.nn.sigmoid(g) * u).astype(jnp.bfloat16)
        o_ref[...] = jnp.dot(h, wd_ref[0], preferred_element_type=jnp.float32)

    @pl.when(rows == 0)
    def _():
        o_ref[...] = jnp.zeros((tm, o_ref.shape[1]), jnp.float32)


def _moe_combine_kernel(pos_ref, x_ref, w_ref, os_hbm, y_ref, buf, sem, *, tb):
    i = pl.program_id(0)

    def issue(r, carry):
        for k in range(TOP_K):
            _row_copy(os_hbm, pos_ref[(i * tb + r) * TOP_K + k], buf.at[k], r, sem).start()
        return carry

    lax.fori_loop(0, tb, issue, 0, unroll=8)
    for k in range(TOP_K):
        pltpu.make_async_copy(buf.at[k], buf.at[k], sem).wait()
    w = w_ref[...]
    y = x_ref[...]
    for k in range(TOP_K):
        y = y + w[:, k:k + 1] * buf[k]
    y_ref[...] = y


def _moe_route(xn, w_group, b_group, w_expert, b_expert, tm):
    n_tok = xn.shape[0]
    gp = jax.nn.softmax((xn @ w_group).astype(jnp.float32) + b_group.astype(jnp.float32), axis=-1)
    g_w, g_idx = lax.top_k(gp, 1)
    elog = (xn @ w_expert).astype(jnp.float32) + b_expert.astype(jnp.float32)
    elog = elog.reshape(n_tok, N_EXPERT_GROUPS, EXPERTS_PER_GROUP)
    elog_sel = jnp.take_along_axis(elog, g_idx[:, :, None], axis=1)[:, 0]
    e_w, e_idx = lax.top_k(jax.nn.softmax(elog_sel, axis=-1), TOP_K)
    e_w = e_w / jnp.sum(e_w, axis=-1, keepdims=True)
    weights = g_w * e_w
    experts = (g_idx * EXPERTS_PER_GROUP + e_idx).reshape(-1)
    onehot = (experts[:, None] == jnp.arange(N_EXPERTS)[None, :]).astype(jnp.int32)
    csum = jnp.cumsum(onehot, axis=0)
    counts = csum[-1]
    rank = jnp.sum(onehot * csum, axis=1) - 1
    padded = ((counts + tm - 1) // tm) * tm
    pend = jnp.cumsum(padded)
    pstart = pend - padded
    pos = (pstart[experts] + rank).astype(jnp.int32)
    n_blk = (n_tok * TOP_K) // tm + N_EXPERTS
    blk_start = jnp.arange(n_blk) * tm
    blk_e = jnp.minimum(jnp.searchsorted(pend, blk_start, side='right'), N_EXPERTS - 1).astype(jnp.int32)
    blk_rows = jnp.clip(counts[blk_e] - (blk_start - pstart[blk_e]), 0, tm).astype(jnp.int32)
    return weights, pos, blk_e, blk_rows, n_blk


def hier_moe_residual(x, norm_g, w_group, b_group, w_expert, b_expert, w_gate, w_up, w_down,
                      *, tm=MOE_TM, tb=MOE_TB):
    b_, s_, d_ = x.shape
    n_tok = b_ * s_
    xt = x.reshape(n_tok, d_)
    weights, pos, blk_e, blk_rows, n_blk = _moe_route(rmsnorm(xt, norm_g), w_group, b_group,
                                                      w_expert, b_expert, tm)
    n_pad = n_blk * tm
    vmem = pltpu.CompilerParams(dimension_semantics=("arbitrary",),
                                vmem_limit_bytes=MOE_VMEM_LIMIT_BYTES)
    xs = pl.pallas_call(
        functools.partial(_moe_dispatch_kernel, tb=tb),
        grid_spec=pltpu.PrefetchScalarGridSpec(
            num_scalar_prefetch=1, grid=(n_tok // tb,),
            in_specs=[pl.BlockSpec((tb, d_), lambda i, pos: (i, 0)),
                      pl.BlockSpec((1, d_), lambda i, pos: (0, 0)),
                      pl.BlockSpec(memory_space=pl.ANY)],
            out_specs=pl.BlockSpec(memory_space=pl.ANY),
            scratch_shapes=[pltpu.VMEM((tb, d_), jnp.float32), pltpu.SemaphoreType.DMA(())]),
        out_shape=jax.ShapeDtypeStruct((n_pad, d_), jnp.float32),
        input_output_aliases={3: 0},
        compiler_params=vmem, name="moe_dispatch",
    )(pos, xt, norm_g.astype(jnp.float32).reshape(1, d_), jnp.zeros((n_pad, d_), jnp.float32))
    bf16 = jnp.bfloat16
    outs = pl.pallas_call(
        functools.partial(_moe_expert_kernel, tm=tm),
        grid_spec=pltpu.PrefetchScalarGridSpec(
            num_scalar_prefetch=2, grid=(n_blk,),
            in_specs=[pl.BlockSpec((tm, d_), lambda i, be, br: (i, 0)),
                      pl.BlockSpec((1, d_, D_EXPERT), lambda i, be, br: (be[i], 0, 0)),
                      pl.BlockSpec((1, d_, D_EXPERT), lambda i, be, br: (be[i], 0, 0)),
                      pl.BlockSpec((1, D_EXPERT, d_), lambda i, be, br: (be[i], 0, 0))],
            out_specs=pl.BlockSpec((tm, d_), lambda i, be, br: (i, 0))),
        out_shape=jax.ShapeDtypeStruct((n_pad, d_), jnp.float32),
        compiler_params=vmem, name="moe_experts",
    )(blk_e, blk_rows, xs, w_gate.astype(bf16), w_up.astype(bf16), w_down.astype(bf16))
    y = pl.pallas_call(
        functools.partial(_moe_combine_kernel, tb=tb),
        grid_spec=pltpu.PrefetchScalarGridSpec(
            num_scalar_prefetch=1, grid=(n_tok // tb,),
            in_specs=[pl.BlockSpec((tb, d_), lambda i, pos: (i, 0)),
                      pl.BlockSpec((tb, TOP_K), lambda i, pos: (i, 0)),
                      pl.BlockSpec(memory_space=pl.ANY)],
            out_specs=pl.BlockSpec((tb, d_), lambda i, pos: (i, 0)),
            scratch_shapes=[pltpu.VMEM((TOP_K, tb, d_), jnp.float32), pltpu.SemaphoreType.DMA(())]),
        out_shape=jax.ShapeDtypeStruct((n_tok, d_), jnp.float32),
        compiler_params=vmem, name="moe_combine",
    )(pos, xt, weights, outs)
    return y.reshape(b_, s_, d_)


def kernel(x, norm_mix, w_in, gdn_conv, gdn_a_log, gdn_dt_bias, gdn_norm, hy_conv, hy_w1, hy_b1, hy_freq1, hy_w2, hy_b2, hy_freq2, hy_w3, hy_deltas, hy_bias, hy_norm, swa_sink, swa_norm, ga_q_norm, ga_k_norm, ga_norm, w_out, norm_ffn, moe_w_group, moe_b_group, moe_w_expert, moe_b_expert, moe_w_gate, moe_w_up, moe_w_down, norm_final):
    b_, s_, _ = x.shape
    rows = s_ // GRID_W
    row_idx = jnp.repeat(jnp.arange(rows), GRID_W)
    col_idx = jnp.tile(jnp.arange(GRID_W), rows)
    pos_feat = hyena_pos_features(s_)
    slopes = alibi_slopes(N_HEADS)
    splits = _split_points()
    for l in range(DEPTH):
        xn = rmsnorm(x, norm_mix[l])
        proj = xn @ w_in[l]
        a_qkv, a_z, a_beta, a_alpha, b_in, c_q, c_kv, d_q, d_kv = jnp.split(proj, splits, axis=-1)
        y_a = gdn_mixer(a_qkv, a_z, a_beta, a_alpha, gdn_conv[l], gdn_a_log[l], gdn_dt_bias[l], gdn_norm[l])
        kf = hyena_filters_f(pos_feat, hy_w1[l], hy_b1[l], hy_freq1[l], hy_w2[l], hy_b2[l],
                             hy_freq2[l], hy_w3[l], hy_deltas[l])
        y_b = hyena_mixer(b_in, hy_conv[l], kf, hy_bias[l], hy_norm[l])
        y_c = window_mixer(c_q, c_kv, swa_sink[l], slopes, swa_norm[l])
        y_d = global_mixer(d_q, d_kv, ga_q_norm[l], ga_k_norm[l], row_idx, col_idx, ga_norm[l])
        x = x + jnp.concatenate([y_a, y_b, y_c, y_d], axis=-1) @ w_out[l]
        x = hier_moe_residual(x, norm_ffn[l], moe_w_group[l], moe_b_group[l], moe_w_expert[l],
                              moe_b_expert[l], moe_w_gate[l], moe_w_up[l], moe_w_down[l])
    return rmsnorm(x, norm_final)
```

```python
import functools
import math

import jax
import jax.numpy as jnp
import numpy as np
from jax import lax
from jax.experimental import pallas as pl
from jax.experimental.pallas import tpu as pltpu

D_MODEL = 2048
DEPTH = 2
N_MIXERS = 4
GROUP_W = D_MODEL // N_MIXERS
HEAD_DIM = 64
N_HEADS = GROUP_W // HEAD_DIM
N_KV_HEADS = 2
GQA_GROUP = N_HEADS // N_KV_HEADS
KV_W = N_KV_HEADS * HEAD_DIM
SHORT_CONV = 3
GDN_CHUNK = 64
HY_ORDER = 2
HY_EMB = 33
HY_BANDS = (HY_EMB - 1) // 2
WINDOW = 128
BLOCK = 128
GRID_W = 64
ROPE_THETA = 10000.0
N_EXPERT_GROUPS = 4
EXPERTS_PER_GROUP = 8
N_EXPERTS = N_EXPERT_GROUPS * EXPERTS_PER_GROUP
TOP_K = 2
D_EXPERT = 512
MOE_BLOCK = 128
EPS = 1e-6
IN_SPLIT_SIZES = (3 * GROUP_W, GROUP_W, 2 * N_HEADS, 2 * N_HEADS, 3 * GROUP_W,
                  GROUP_W, 2 * KV_W, GROUP_W, 2 * KV_W)

ATTN_VMEM_LIMIT_BYTES = 48 * 1024 * 1024
ATTN_TQ = 256
ATTN_TK = 2048
FFT_VMEM_LIMIT_BYTES = 48 * 1024 * 1024
FFT_N2 = 256
FFT_COL_TILE = 4096
MOE_VMEM_LIMIT_BYTES = 48 * 1024 * 1024
MOE_TM = 512
MOE_TB = 256
LOG2E = 1.4426950408889634
ATTN_SAFE_SHIFT = 50.0


def _split_points():
    return [int(v) for v in np.cumsum(IN_SPLIT_SIZES)[:-1]]


def rmsnorm(x, g):
    xf = x.astype(jnp.float32)
    y = xf * lax.rsqrt(jnp.mean(xf * xf, axis=-1, keepdims=True) + EPS)
    return (y * g.astype(jnp.float32)).astype(x.dtype)


def l2norm(x):
    return x * lax.rsqrt(jnp.sum(x * x, axis=-1, keepdims=True) + EPS)


def short_conv(u, w):
    k_w = w.shape[0]
    p = k_w // 2
    s_len = u.shape[1]
    up = jnp.pad(u, ((0, 0), (p, p), (0, 0)))
    out = up[:, 0:s_len] * w[0]
    for j in range(1, k_w):
        out = out + up[:, j:j + s_len] * w[j]
    return out


def gdn_chunked(q, k, v, beta, g):
    b_, s_, h_, dk = q.shape
    dv = v.shape[-1]
    c = GDN_CHUNK
    n = s_ // c

    def chunks(t):
        return t.reshape(b_, n, c, h_, -1).transpose(0, 3, 1, 2, 4)

    q = chunks(q) * (dk ** -0.5)
    k = chunks(k)
    v = chunks(v)
    beta = beta.reshape(b_, n, c, h_).transpose(0, 3, 1, 2)
    gc = jnp.cumsum(g.reshape(b_, n, c, h_).transpose(0, 3, 1, 2), axis=-1)
    idx = jnp.arange(c)
    incl = idx[:, None] >= idx[None, :]
    strict = idx[:, None] > idx[None, :]
    decay = jnp.exp(jnp.where(incl, gc[..., :, None] - gc[..., None, :], -jnp.inf))
    kb = k * beta[..., None]
    a_strict = jnp.where(strict, jnp.einsum('bhnid,bhnjd->bhnij', kb, k) * decay, 0.0)
    rhs = jnp.concatenate([v * beta[..., None], kb * jnp.exp(gc)[..., None]], axis=-1)
    sol = lax.linalg.triangular_solve(a_strict, rhs, left_side=True, lower=True,
                                      unit_diagonal=True)
    value, k_cumdecay = sol[..., :dv], sol[..., dv:]
    attn = jnp.einsum('bhnid,bhnjd->bhnij', q, k) * decay
    q_dec = q * jnp.exp(gc)[..., None]
    k_dec = k * jnp.exp(gc[..., -1:] - gc)[..., None]
    chunk_dec = jnp.exp(gc[..., -1])

    def step(state, xs):
        value_n, kcd_n, attn_n, qd_n, kd_n, cd_n = xs
        v_new = value_n - jnp.einsum('bhck,bhkv->bhcv', kcd_n, state)
        o = (jnp.einsum('bhck,bhkv->bhcv', qd_n, state)
             + jnp.einsum('bhij,bhjv->bhiv', attn_n, v_new))
        state = state * cd_n[..., None, None] + jnp.einsum('bhck,bhcv->bhkv', kd_n, v_new)
        return state, o

    xs = tuple(jnp.moveaxis(t, 2, 0) for t in (value, k_cumdecay, attn, q_dec, k_dec, chunk_dec))
    state0 = jnp.zeros((b_, h_, dk, dv), jnp.float32)
    _, o = lax.scan(step, state0, xs)
    return o.transpose(1, 0, 3, 2, 4).reshape(b_, s_, h_, dv)


def gdn_mixer(qkv, z, b_in, a_in, conv_w, a_log, dt_bias, norm_g):
    b_, s_, _ = qkv.shape
    dtype = qkv.dtype
    qkv = jax.nn.silu(short_conv(qkv, conv_w)).astype(jnp.float32)
    q, k, v = jnp.split(qkv, 3, axis=-1)
    q = l2norm(q.reshape(b_, s_, N_HEADS, HEAD_DIM))
    k = l2norm(k.reshape(b_, s_, N_HEADS, HEAD_DIM))
    v = v.reshape(b_, s_, N_HEADS, HEAD_DIM)
    b_in = b_in.astype(jnp.float32).reshape(b_, s_, 2, N_HEADS)
    a_in = a_in.astype(jnp.float32).reshape(b_, s_, 2, N_HEADS)
    beta = jax.nn.sigmoid(b_in)
    g = -jnp.exp(a_log.astype(jnp.float32)) * jax.nn.softplus(a_in + dt_bias.astype(jnp.float32))
    flip = lambda t: jnp.flip(t, axis=1)
    o_fwd = gdn_chunked(q, k, v, beta[:, :, 0], g[:, :, 0])
    o_bwd = flip(gdn_chunked(flip(q), flip(k), flip(v), flip(beta[:, :, 1]), flip(g[:, :, 1])))
    o = o_fwd + o_bwd
    zg = jax.nn.silu(z.astype(jnp.float32)).reshape(b_, s_, N_HEADS, HEAD_DIM)
    o = rmsnorm(o, norm_g) * zg
    return o.reshape(b_, s_, GROUP_W).astype(dtype)


def hyena_pos_features(length):
    t = jnp.linspace(0.0, 1.0, length, dtype=jnp.float32)[:, None]
    w = 2.0 * math.pi * jnp.arange(length, dtype=jnp.float32) / length
    f = jnp.linspace(1e-4, HY_BANDS - 1, HY_BANDS, dtype=jnp.float32)
    fw = w[:, None] * f[None, :]
    return jnp.concatenate([t, jnp.cos(fw), -jnp.sin(fw)], axis=-1)


def hyena_filters_f(z, w1, b1, f1, w2, b2, f2, w3, deltas):
    f32 = jnp.float32
    length = z.shape[0]
    t = z[:, :1]
    h = jnp.sin(f1.astype(f32) * (z @ w1.astype(f32) + b1.astype(f32)))
    h = jnp.sin(f2.astype(f32) * (h @ w2.astype(f32) + b2.astype(f32)))
    h = (h @ w3.astype(f32)) * jnp.exp(-t * jnp.abs(deltas.astype(f32)))
    h = h.reshape(length, HY_ORDER, 2, GROUP_W)
    kern = jnp.concatenate([h[:, :, 0], jnp.zeros((1, HY_ORDER, GROUP_W), f32),
                            h[:0:-1, :, 1]], axis=0)
    return kern / (jnp.sum(jnp.abs(kern), axis=0, keepdims=True) + EPS)


def _dft_tables(n_fft):
    n1 = n_fft // FFT_N2
    def dft(n):
        kk = (np.arange(n)[:, None] * np.arange(n)[None, :]) % n
        ang = -2.0 * np.pi * kk / n
        return np.cos(ang), np.sin(ang)
    f1r, f1i = dft(n1)
    f2r, f2i = dft(FFT_N2)
    kk = (np.arange(n1)[:, None] * np.arange(FFT_N2)[None, :]) % n_fft
    tw = -2.0 * np.pi * kk / n_fft
    f32 = np.float32
    return dict(
        f1=np.concatenate([f1r, f1i], axis=0).astype(f32),
        f1_inv=(np.concatenate([f1r[:n1 // 2], f1i[:n1 // 2]], axis=0) / n_fft).astype(f32),
        f2=np.concatenate([f2r, f2i], axis=0).astype(f32),
        twr=np.cos(tw).astype(f32)[:, :, None], twi=np.sin(tw).astype(f32)[:, :, None])


def _dot_f32(a, b):
    return jnp.dot(a, b, precision=lax.Precision.HIGHEST, preferred_element_type=jnp.float32)


def _fft_stage1_kernel(f_ref, zr_ref, zi_ref, yr_ref, yi_ref, *, n1):
    f = f_ref[...]
    p = _dot_f32(f, zr_ref[0])
    if zi_ref is None:
        yr_ref[...] = p[:n1]
        yi_ref[...] = p[n1:]
    else:
        q = _dot_f32(f, zi_ref[0])
        yr_ref[...] = p[:n1] - q[n1:]
        yi_ref[...] = q[:n1] + p[n1:]


def _fft_stage1_real_kernel(f_ref, zr_ref, yr_ref, yi_ref, *, n1):
    _fft_stage1_kernel(f_ref, zr_ref, None, yr_ref, yi_ref, n1=n1)


def _fft_mid_kernel(f_ref, twr_ref, twi_ref, yr_ref, yi_ref, kr_ref, ki_ref, qr_ref, qi_ref):
    n2 = FFT_N2
    twr, twi = twr_ref[0], twi_ref[0]
    yr, yi = yr_ref[0], yi_ref[0]
    f = f_ref[...]
    p = _dot_f32(f, yr * twr - yi * twi)
    q = _dot_f32(f, yr * twi + yi * twr)
    xr = p[:n2] - q[n2:]
    xi = q[:n2] + p[n2:]
    if kr_ref is None:
        qr_ref[0] = xr
        qi_ref[0] = xi
        return
    kr, ki = kr_ref[0], ki_ref[0]
    p = _dot_f32(f, xr * kr - xi * ki)
    q = _dot_f32(f, xr * ki + xi * kr)
    wr = p[:n2] + q[n2:]
    wi = q[:n2] - p[n2:]
    qr_ref[0] = wr * twr + wi * twi
    qi_ref[0] = wi * twr - wr * twi


def _fft_mid_spectrum_kernel(f_ref, twr_ref, twi_ref, yr_ref, yi_ref, qr_ref, qi_ref):
    _fft_mid_kernel(f_ref, twr_ref, twi_ref, yr_ref, yi_ref, None, None, qr_ref, qi_ref)


def _fft_last_kernel(f_ref, qr_ref, qi_ref, u_ref, gate_ref, bias_ref, o_ref, *, nh):
    f = f_ref[...]
    p = _dot_f32(f, qr_ref[...])
    q = _dot_f32(f, qi_ref[...])
    bias = bias_ref[...]
    o_ref[0] = gate_ref[0] * (p[:nh] + q[nh:] + u_ref[0] * bias)
    o_ref[1] = gate_ref[1] * (q[:nh] - p[nh:] + u_ref[1] * bias)


def _fft_params(n_axes):
    return pltpu.CompilerParams(dimension_semantics=("arbitrary",) * n_axes,
                                vmem_limit_bytes=FFT_VMEM_LIMIT_BYTES)


def _fft_forward(tab, z, n_ch):
    n1 = tab["f1"].shape[1]
    parts, rows, cols = z.shape
    tn = min(FFT_COL_TILE, cols)
    f1 = jnp.asarray(tab["f1"][:, :rows])
    y_shape = jax.ShapeDtypeStruct((n1, cols), jnp.float32)
    col_spec = pl.BlockSpec((n1, tn), lambda j: (0, j))
    z_specs = [pl.BlockSpec((1, rows, tn), lambda j, p=p: (p, 0, j)) for p in range(parts)]
    body = _fft_stage1_kernel if parts == 2 else _fft_stage1_real_kernel
    yr, yi = pl.pallas_call(
        functools.partial(body, n1=n1), grid=(cols // tn,),
        in_specs=[pl.BlockSpec(f1.shape, lambda j: (0, 0))] + z_specs,
        out_specs=[col_spec, col_spec], out_shape=[y_shape, y_shape],
        compiler_params=_fft_params(1), name="fft_stage1",
    )(f1, *([z] * parts))
    return yr.reshape(n1, FFT_N2, n_ch), yi.reshape(n1, FFT_N2, n_ch)


def _fft_mid(tab, yr, yi, kr=None, ki=None):
    n1, n2, n_ch = yr.shape
    slab = pl.BlockSpec((1, n2, n_ch), lambda i: (i, 0, 0))
    tw_spec = pl.BlockSpec((1, n2, 1), lambda i: (i, 0, 0))
    f2 = jnp.asarray(tab["f2"])
    ops = [f2, jnp.asarray(tab["twr"]), jnp.asarray(tab["twi"]), yr, yi]
    specs = [pl.BlockSpec(f2.shape, lambda i: (0, 0)), tw_spec, tw_spec, slab, slab]
    body = _fft_mid_spectrum_kernel
    if kr is not None:
        ops += [kr, ki]
        specs += [slab, slab]
        body = _fft_mid_kernel
    shape = jax.ShapeDtypeStruct((n1, n2, n_ch), jnp.float32)
    return pl.pallas_call(
        body, grid=(n1,), in_specs=specs, out_specs=[slab, slab], out_shape=[shape, shape],
        compiler_params=_fft_params(1), name="fft_mid",
    )(*ops)


def _fft_conv_gate(tab, u, gate, bias, kr, ki):
    b_, length, n_ch = u.shape
    assert b_ == 2
    n1 = tab["f1"].shape[1]
    nh = n1 // 2
    cols = FFT_N2 * n_ch
    uv = u.reshape(b_, nh, cols)
    yr, yi = _fft_forward(tab, uv, n_ch)
    qr, qi = _fft_mid(tab, yr, yi, kr, ki)
    tn = min(FFT_COL_TILE, cols)
    f1_inv = jnp.asarray(tab["f1_inv"])
    q_spec = pl.BlockSpec((n1, tn), lambda j: (0, j))
    u_spec = pl.BlockSpec((b_, nh, tn), lambda j: (0, 0, j))
    out = pl.pallas_call(
        functools.partial(_fft_last_kernel, nh=nh), grid=(cols // tn,),
        in_specs=[pl.BlockSpec(f1_inv.shape, lambda j: (0, 0)), q_spec, q_spec, u_spec, u_spec,
                  pl.BlockSpec((1, tn), lambda j: (0, 0))],
        out_specs=u_spec, out_shape=jax.ShapeDtypeStruct((b_, nh, cols), jnp.float32),
        compiler_params=_fft_params(1), name="fft_last",
    )(f1_inv, qr.reshape(n1, cols), qi.reshape(n1, cols), uv, gate.reshape(b_, nh, cols),
      jnp.tile(bias.astype(jnp.float32), tn // n_ch).reshape(1, tn))
    return out.reshape(b_, length, n_ch)


def hyena_mixer(u, conv_w, kern, bias, norm_g):
    dtype = u.dtype
    length = u.shape[1]
    u = short_conv(u, conv_w).astype(jnp.float32)
    x1, x2, v = jnp.split(u, 3, axis=-1)
    tab = _dft_tables(2 * length)
    n1 = tab["f1"].shape[1]
    n_filt = HY_ORDER * GROUP_W
    kr, ki = _fft_mid(tab, *_fft_forward(tab, kern.reshape(1, n1, FFT_N2 * n_filt), n_filt))
    y = _fft_conv_gate(tab, v, x1, bias[0], kr[:, :, :GROUP_W], ki[:, :, :GROUP_W])
    y = _fft_conv_gate(tab, y, x2, bias[1], kr[:, :, GROUP_W:], ki[:, :, GROUP_W:])
    return rmsnorm(y.astype(dtype), norm_g)


def alibi_slopes(n):
    return 2.0 ** (-8.0 * jnp.arange(1, n + 1, dtype=jnp.float32) / n)


def window_mixer(c_q, c_kv, sink, slopes, norm_g):
    b_, s_, _ = c_q.shape
    nb = s_ // BLOCK
    q = c_q.reshape(b_, nb, BLOCK, N_KV_HEADS, GQA_GROUP, HEAD_DIM)
    k, v = jnp.split(c_kv, 2, axis=-1)

    def band(t):
        tp = jnp.pad(t.reshape(b_, s_, N_KV_HEADS, HEAD_DIM), ((0, 0), (BLOCK, BLOCK), (0, 0), (0, 0)))
        tp = tp.reshape(b_, nb + 2, BLOCK, N_KV_HEADS, HEAD_DIM)
        return jnp.concatenate([tp[:, :-2], tp[:, 1:-1], tp[:, 2:]], axis=2)

    kb, vb = band(k), band(v)
    s = jnp.einsum('bnqkgd,bnskd->bnkgqs', q, kb).astype(jnp.float32) * (HEAD_DIM ** -0.5)
    rel = BLOCK + jnp.arange(BLOCK)[:, None] - jnp.arange(3 * BLOCK)[None, :]
    key_pos = jnp.arange(nb)[:, None] * BLOCK - BLOCK + jnp.arange(3 * BLOCK)[None, :]
    valid = (jnp.abs(rel) <= WINDOW)[None] & ((key_pos >= 0) & (key_pos < s_))[:, None, :]
    dist = jnp.abs(rel).astype(jnp.float32)
    logits = s - slopes.reshape(N_KV_HEADS, GQA_GROUP)[None, None, :, :, None, None] * dist
    logits = jnp.where(valid[None, :, None, None], logits, -jnp.inf)
    sink_col = jnp.broadcast_to(
        sink.astype(jnp.float32).reshape(N_KV_HEADS, GQA_GROUP)[None, None, :, :, None, None],
        logits.shape[:-1] + (1,))
    p = jax.nn.softmax(jnp.concatenate([logits, sink_col], axis=-1), axis=-1)[..., :-1]
    o = jnp.einsum('bnkgqs,bnskd->bnqkgd', p.astype(c_q.dtype), vb)
    return rmsnorm(o.reshape(b_, s_, GROUP_W), norm_g)


def rope_1d(x, pos):
    d = x.shape[-1]
    inv = ROPE_THETA ** (-jnp.arange(0, d, 2, dtype=jnp.float32) / d)
    ang = pos.astype(jnp.float32)[:, None] * inv[None, :]
    cos = jnp.cos(ang)[None, :, None, :]
    sin = jnp.sin(ang)[None, :, None, :]
    xf = x.astype(jnp.float32)
    x1, x2 = xf[..., :d // 2], xf[..., d // 2:]
    return jnp.concatenate([x1 * cos - x2 * sin, x2 * cos + x1 * sin], axis=-1).astype(x.dtype)


def axial_rope(x, row_idx, col_idx):
    half = HEAD_DIM // 2
    return jnp.concatenate([rope_1d(x[..., :half], row_idx), rope_1d(x[..., half:], col_idx)], axis=-1)


def _flash_kernel(q_ref, kt_ref, v_ref, o_ref, *, tq, tk, n_kc):
    m_rows = GQA_GROUP * tq
    q = q_ref[0, 0].reshape(m_rows, 2 * HEAD_DIM)

    def body(c, acc):
        off = pl.multiple_of(c * tk, tk)
        s = jnp.dot(q, kt_ref[0, 0, :, pl.ds(off, tk)], preferred_element_type=jnp.float32)
        p = jnp.exp2(s).astype(jnp.bfloat16)
        return acc + jnp.dot(p, v_ref[0, 0, pl.ds(off, tk), :], preferred_element_type=jnp.float32)

    acc = lax.fori_loop(0, n_kc, body, jnp.zeros((m_rows, 2 * HEAD_DIM), jnp.float32))
    o = acc[:, :HEAD_DIM] / acc[:, HEAD_DIM:HEAD_DIM + 1]
    o_ref[0, 0] = o.reshape(GQA_GROUP, tq, HEAD_DIM)


def _rowmax_kernel(q_ref, kt_ref, m_ref, *, tq, tk, n_kc):
    m_rows = GQA_GROUP * tq
    q = q_ref[0, 0].reshape(m_rows, 2 * HEAD_DIM)

    def body(c, mx):
        off = pl.multiple_of(c * tk, tk)
        s = jnp.dot(q, kt_ref[0, 0, :, pl.ds(off, tk)], preferred_element_type=jnp.float32)
        for j in range(tk // 128):
            mx = jnp.maximum(mx, s[:, j * 128:(j + 1) * 128])
        return mx

    mx = lax.fori_loop(0, n_kc, body, jnp.full((m_rows, 128), -jnp.inf, jnp.float32))
    m_ref[0, 0] = jnp.max(mx, axis=-1, keepdims=True).reshape(GQA_GROUP, tq, 1)


def _attn_call(body, q, kt, v, out_w, name, *, tq=ATTN_TQ, tk=ATTN_TK):
    b_, _, _, s_, _ = q.shape
    wide = 2 * HEAD_DIM
    q_spec = pl.BlockSpec((1, 1, GQA_GROUP, tq, wide), lambda b, h, i: (b, h, 0, i, 0))
    kt_spec = pl.BlockSpec((1, 1, wide, s_), lambda b, h, i: (b, h, 0, 0))
    v_spec = pl.BlockSpec((1, 1, s_, wide), lambda b, h, i: (b, h, 0, 0))
    o_spec = pl.BlockSpec((1, 1, GQA_GROUP, tq, out_w), lambda b, h, i: (b, h, 0, i, 0))
    operands = (q, kt) if v is None else (q, kt, v)
    return pl.pallas_call(
        functools.partial(body, tq=tq, tk=tk, n_kc=s_ // tk),
        out_shape=jax.ShapeDtypeStruct((b_, N_KV_HEADS, GQA_GROUP, s_, out_w), jnp.float32),
        grid=(b_, N_KV_HEADS, s_ // tq),
        in_specs=[q_spec, kt_spec] if v is None else [q_spec, kt_spec, v_spec],
        out_specs=o_spec,
        compiler_params=pltpu.CompilerParams(
            dimension_semantics=("arbitrary", "arbitrary", "arbitrary"),
            vmem_limit_bytes=ATTN_VMEM_LIMIT_BYTES),
        name=name,
    )(*operands)


def global_mixer(d_q, d_kv, q_norm_g, k_norm_g, row_idx, col_idx, norm_g):
    b_, s_, _ = d_q.shape
    q = d_q.reshape(b_, s_, N_HEADS, HEAD_DIM)
    k, v = jnp.split(d_kv, 2, axis=-1)
    k = k.reshape(b_, s_, N_KV_HEADS, HEAD_DIM)
    v = v.reshape(b_, s_, N_KV_HEADS, HEAD_DIM)
    q = axial_rope(rmsnorm(q, q_norm_g), row_idx, col_idx) * (LOG2E * HEAD_DIM ** -0.5)
    k = axial_rope(rmsnorm(k, k_norm_g), row_idx, col_idx)
    bf16 = jnp.bfloat16
    qb = q.astype(bf16).reshape(b_, s_, N_KV_HEADS, GQA_GROUP, HEAD_DIM).transpose(0, 2, 3, 1, 4)
    kb = k.astype(bf16).transpose(0, 2, 1, 3)
    vb = v.astype(bf16).transpose(0, 2, 1, 3)
    qn = jnp.sqrt(jnp.sum(jnp.square(qb.astype(jnp.float32)), axis=-1, keepdims=True))
    kn = jnp.sqrt(jnp.max(jnp.sum(jnp.square(kb.astype(jnp.float32)), axis=-1), axis=-1))
    bound = qn * kn[:, :, None, None, None]
    one = jnp.ones((b_, N_KV_HEADS, s_, 1), bf16)
    pad = jnp.zeros((b_, N_KV_HEADS, s_, HEAD_DIM - 1), bf16)
    kta = jnp.concatenate([kb, one, pad], axis=-1).transpose(0, 1, 3, 2)
    va = jnp.concatenate([vb, one, pad], axis=-1)
    qpad = jnp.zeros(qb.shape[:-1] + (HEAD_DIM - 1,), bf16)

    def with_shift(m):
        return jnp.concatenate([qb, (-m).astype(bf16), qpad], axis=-1)

    m = lax.cond(jnp.max(bound) < ATTN_SAFE_SHIFT, lambda: bound,
                 lambda: _attn_call(_rowmax_kernel, with_shift(jnp.zeros_like(bound)), kta, None, 1,
                                    "global_attention_rowmax"))
    o = _attn_call(_flash_kernel, with_shift(m), kta, va, HEAD_DIM, "global_flash_attention")
    o = o.transpose(0, 3, 1, 2, 4).reshape(b_, s_, GROUP_W)
    return rmsnorm(o, norm_g)


def _row_copy(src, src_row, dst, dst_row, sem):
    return pltpu.make_async_copy(src.at[pl.ds(src_row, 1)], dst.at[pl.ds(dst_row, 1)], sem)


def _moe_dispatch_kernel(pos_ref, x_ref, g_ref, zeros_hbm, xs_hbm, xn_scr, sem, *, tb):
    del zeros_hbm
    i = pl.program_id(0)
    x = x_ref[...]
    xn_scr[...] = x * lax.rsqrt(jnp.mean(x * x, axis=-1, keepdims=True) + EPS) * g_ref[...]

    def issue(r, carry):
        for k in range(TOP_K):
            _row_copy(xn_scr, r, xs_hbm, pos_ref[(i * tb + r) * TOP_K + k], sem).start()
        return carry

    lax.fori_loop(0, tb, issue, 0, unroll=8)
    for _ in range(TOP_K):
        pltpu.make_async_copy(xn_scr, xn_scr, sem).wait()


def _moe_expert_kernel(blk_e_ref, blk_rows_ref, xs_ref, wg_ref, wu_ref, wd_ref, o_ref, *, tm):
    del blk_e_ref
    rows = blk_rows_ref[pl.program_id(0)]

    @pl.when(rows > 0)
    def _():
        x = xs_ref[...].astype(jnp.bfloat16)
        g = jnp.dot(x, wg_ref[0], preferred_element_type=jnp.float32)
        u = jnp.dot(x, wu_ref[0], preferred_element_type=jnp.float32)
        h = (g * jax.nn.sigmoid(g) * u).astype(jnp.bfloat16)
        o_ref[...] = jnp.dot(h, wd_ref[0], preferred_element_type=jnp.float32)

    @pl.when(rows == 0)
    def _():
        o_ref[...] = jnp.zeros((tm, o_ref.shape[1]), jnp.float32)


def _moe_combine_kernel(pos_ref, x_ref, w_ref, os_hbm, y_ref, buf, sem, *, tb):
    i = pl.program_id(0)

    def issue(r, carry):
        for k in range(TOP_K):
            _row_copy(os_hbm, pos_ref[(i * tb + r) * TOP_K + k], buf.at[k], r, sem).start()
        return carry

    lax.fori_loop(0, tb, issue, 0, unroll=8)
    for k in range(TOP_K):
        pltpu.make_async_copy(buf.at[k], buf.at[k], sem).wait()
    w = w_ref[...]
    y = x_ref[...]
    for k in range(TOP_K):
        y = y + w[:, k:k + 1] * buf[k]
    y_ref[...] = y


def _moe_route(xn, w_group, b_group, w_expert, b_expert, tm):
    n_tok = xn.shape[0]
    gp = jax.nn.softmax((xn @ w_group).astype(jnp.float32) + b_group.astype(jnp.float32), axis=-1)
    g_w, g_idx = lax.top_k(gp, 1)
    elog = (xn @ w_expert).astype(jnp.float32) + b_expert.astype(jnp.float32)
    elog = elog.reshape(n_tok, N_EXPERT_GROUPS, EXPERTS_PER_GROUP)
    elog_sel = jnp.take_along_axis(elog, g_idx[:, :, None], axis=1)[:, 0]
    e_w, e_idx = lax.top_k(jax.nn.softmax(elog_sel, axis=-1), TOP_K)
    e_w = e_w / jnp.sum(e_w, axis=-1, keepdims=True)
    weights = g_w * e_w
    experts = (g_idx * EXPERTS_PER_GROUP + e_idx).reshape(-1)
    onehot = (experts[:, None] == jnp.arange(N_EXPERTS)[None, :]).astype(jnp.int32)
    csum = jnp.cumsum(onehot, axis=0)
    counts = csum[-1]
    rank = jnp.sum(onehot * csum, axis=1) - 1
    padded = ((counts + tm - 1) // tm) * tm
    pend = jnp.cumsum(padded)
    pstart = pend - padded
    pos = (pstart[experts] + rank).astype(jnp.int32)
    n_blk = (n_tok * TOP_K) // tm + N_EXPERTS
    blk_start = jnp.arange(n_blk) * tm
    blk_e = jnp.minimum(jnp.searchsorted(pend, blk_start, side='right'), N_EXPERTS - 1).astype(jnp.int32)
    blk_rows = jnp.clip(counts[blk_e] - (blk_start - pstart[blk_e]), 0, tm).astype(jnp.int32)
    return weights, pos, blk_e, blk_rows, n_blk


def hier_moe_residual(x, norm_g, w_group, b_group, w_expert, b_expert, w_gate, w_up, w_down,
                      *, tm=MOE_TM, tb=MOE_TB):
    b_, s_, d_ = x.shape
    n_tok = b_ * s_
    xt = x.reshape(n_tok, d_)
    weights, pos, blk_e, blk_rows, n_blk = _moe_route(rmsnorm(xt, norm_g), w_group, b_group,
                                                      w_expert, b_expert, tm)
    n_pad = n_blk * tm
    vmem = pltpu.CompilerParams(dimension_semantics=("arbitrary",),
                                vmem_limit_bytes=MOE_VMEM_LIMIT_BYTES)
    xs = pl.pallas_call(
        functools.partial(_moe_dispatch_kernel, tb=tb),
        grid_spec=pltpu.PrefetchScalarGridSpec(
            num_scalar_prefetch=1, grid=(n_tok // tb,),
            in_specs=[pl.BlockSpec((tb, d_), lambda i, pos: (i, 0)),
                      pl.BlockSpec((1, d_), lambda i, pos: (0, 0)),
                      pl.BlockSpec(memory_space=pl.ANY)],
            out_specs=pl.BlockSpec(memory_space=pl.ANY),
            scratch_shapes=[pltpu.VMEM((tb, d_), jnp.float32), pltpu.SemaphoreType.DMA(())]),
        out_shape=jax.ShapeDtypeStruct((n_pad, d_), jnp.float32),
        input_output_aliases={3: 0},
        compiler_params=vmem, name="moe_dispatch",
    )(pos, xt, norm_g.astype(jnp.float32).reshape(1, d_), jnp.zeros((n_pad, d_), jnp.float32))
    bf16 = jnp.bfloat16
    outs = pl.pallas_call(
        functools.partial(_moe_expert_kernel, tm=tm),
        grid_spec=pltpu.PrefetchScalarGridSpec(
            num_scalar_prefetch=2, grid=(n_blk,),
            in_specs=[pl.BlockSpec((tm, d_), lambda i, be, br: (i, 0)),
                      pl.BlockSpec((1, d_, D_EXPERT), lambda i, be, br: (be[i], 0, 0)),
                      pl.BlockSpec((1, d_, D_EXPERT), lambda i, be, br: (be[i], 0, 0)),
                      pl.BlockSpec((1, D_EXPERT, d_), lambda i, be, br: (be[i], 0, 0))],
            out_specs=pl.BlockSpec((tm, d_), lambda i, be, br: (i, 0))),
        out_shape=jax.ShapeDtypeStruct((n_pad, d_), jnp.float32),
        compiler_params=vmem, name="moe_experts",
    )(blk_e, blk_rows, xs, w_gate.astype(bf16), w_up.astype(bf16), w_down.astype(bf16))
    y = pl.pallas_call(
        functools.partial(_moe_combine_kernel, tb=tb),
        grid_spec=pltpu.PrefetchScalarGridSpec(
            num_scalar_prefetch=1, grid=(n_tok // tb,),
            in_specs=[pl.BlockSpec((tb, d_), lambda i, pos: (i, 0)),
                      pl.BlockSpec((tb, TOP_K), lambda i, pos: (i, 0)),
                      pl.BlockSpec(memory_space=pl.ANY)],
            out_specs=pl.BlockSpec((tb, d_), lambda i, pos: (i, 0)),
            scratch_shapes=[pltpu.VMEM((TOP_K, tb, d_), jnp.float32), pltpu.SemaphoreType.DMA(())]),
        out_shape=jax.ShapeDtypeStruct((n_tok, d_), jnp.float32),
        compiler_params=vmem, name="moe_combine",
    )(pos, xt, weights, outs)
    return y.reshape(b_, s_, d_)


def kernel(x, norm_mix, w_in, gdn_conv, gdn_a_log, gdn_dt_bias, gdn_norm, hy_conv, hy_w1, hy_b1, hy_freq1, hy_w2, hy_b2, hy_freq2, hy_w3, hy_deltas, hy_bias, hy_norm, swa_sink, swa_norm, ga_q_norm, ga_k_norm, ga_norm, w_out, norm_ffn, moe_w_group, moe_b_group, moe_w_expert, moe_b_expert, moe_w_gate, moe_w_up, moe_w_down, norm_final):
    b_, s_, _ = x.shape
    rows = s_ // GRID_W
    row_idx = jnp.repeat(jnp.arange(rows), GRID_W)
    col_idx = jnp.tile(jnp.arange(GRID_W), rows)
    pos_feat = hyena_pos_features(s_)
    slopes = alibi_slopes(N_HEADS)
    splits = _split_points()
    for l in range(DEPTH):
        xn = rmsnorm(x, norm_mix[l])
        proj = xn @ w_in[l]
        a_qkv, a_z, a_beta, a_alpha, b_in, c_q, c_kv, d_q, d_kv = jnp.split(proj, splits, axis=-1)
        y_a = gdn_mixer(a_qkv, a_z, a_beta, a_alpha, gdn_conv[l], gdn_a_log[l], gdn_dt_bias[l], gdn_norm[l])
        kf = hyena_filters_f(pos_feat, hy_w1[l], hy_b1[l], hy_freq1[l], hy_w2[l], hy_b2[l],
                             hy_freq2[l], hy_w3[l], hy_deltas[l])
        y_b = hyena_mixer(b_in, hy_conv[l], kf, hy_bias[l], hy_norm[l])
        y_c = window_mixer(c_q, c_kv, swa_sink[l], slopes, swa_norm[l])
        y_d = global_mixer(d_q, d_kv, ga_q_norm[l], ga_k_norm[l], row_idx, col_idx, ga_norm[l])
        x = x + jnp.concatenate([y_a, y_b, y_c, y_d], axis=-1) @ w_out[l]
        x = hier_moe_residual(x, norm_ffn[l], moe_w_group[l], moe_b_group[l], moe_w_expert[l],
                              moe_b_expert[l], moe_w_gate[l], moe_w_up[l], moe_w_down[l])
    return rmsnorm(x, norm_final)
```

```python
import functools
import math

import jax
import jax.numpy as jnp
import numpy as np
from jax import lax
from jax.experimental import pallas as pl
from jax.experimental.pallas import tpu as pltpu

D_MODEL = 2048
DEPTH = 2
N_MIXERS = 4
GROUP_W = D_MODEL // N_MIXERS
HEAD_DIM = 64
N_HEADS = GROUP_W // HEAD_DIM
N_KV_HEADS = 2
GQA_GROUP = N_HEADS // N_KV_HEADS
KV_W = N_KV_HEADS * HEAD_DIM
SHORT_CONV = 3
GDN_CHUNK = 64
HY_ORDER = 2
HY_EMB = 33
HY_BANDS = (HY_EMB - 1) // 2
WINDOW = 128
BLOCK = 128
GRID_W = 64
ROPE_THETA = 10000.0
N_EXPERT_GROUPS = 4
EXPERTS_PER_GROUP = 8
N_EXPERTS = N_EXPERT_GROUPS * EXPERTS_PER_GROUP
TOP_K = 2
D_EXPERT = 512
MOE_BLOCK = 128
EPS = 1e-6
IN_SPLIT_SIZES = (3 * GROUP_W, GROUP_W, 2 * N_HEADS, 2 * N_HEADS, 3 * GROUP_W,
                  GROUP_W, 2 * KV_W, GROUP_W, 2 * KV_W)

ATTN_VMEM_LIMIT_BYTES = 48 * 1024 * 1024
ATTN_TQ = 256
ATTN_TK = 2048
WINDOW_MASK = -1e30
GDN_VMEM_LIMIT_BYTES = 48 * 1024 * 1024
GDN_SB = 256
GDN_INV_BASE = 8
FFT_VMEM_LIMIT_BYTES = 48 * 1024 * 1024
FFT_N2 = 256
FFT_COL_TILE = 4096
MOE_VMEM_LIMIT_BYTES = 48 * 1024 * 1024
MOE_TM = 512
MOE_TB = 256
LOG2E = 1.4426950408889634
ATTN_SAFE_SHIFT = 50.0


def _split_points():
    return [int(v) for v in np.cumsum(IN_SPLIT_SIZES)[:-1]]


def rmsnorm(x, g):
    xf = x.astype(jnp.float32)
    y = xf * lax.rsqrt(jnp.mean(xf * xf, axis=-1, keepdims=True) + EPS)
    return (y * g.astype(jnp.float32)).astype(x.dtype)


def l2norm(x):
    return x * lax.rsqrt(jnp.sum(x * x, axis=-1, keepdims=True) + EPS)


def short_conv(u, w):
    k_w = w.shape[0]
    p = k_w // 2
    s_len = u.shape[1]
    up = jnp.pad(u, ((0, 0), (p, p), (0, 0)))
    out = up[:, 0:s_len] * w[0]
    for j in range(1, k_w):
        out = out + up[:, j:j + s_len] * w[j]
    return out


def _gdn_masks(rev):
    r = np.arange(GDN_SB)
    i, j = r[:, None], r[None, :]
    same = lambda s: (i // s) == (j // s)
    before = (i < j) if rev else (i > j)
    chunk = same(GDN_CHUNK)
    masks = [chunk & (before | (i == j)),
             chunk,
             chunk & before,
             i == j,
             same(GDN_INV_BASE)]
    s = GDN_INV_BASE
    while s < GDN_CHUNK:
        masks.append(same(2 * s) & ~same(s))
        s *= 2
    return np.stack(masks).astype(np.float32)


def _mm_bf16(a, b):
    return jnp.dot(a.astype(jnp.bfloat16), b.astype(jnp.bfloat16), preferred_element_type=jnp.float32)


def _unit_triangular_inverses(a_list, eye, m_base, m_offs):
    n_list = [-a * m_base for a in a_list]
    t_list = [eye + n for n in n_list]
    power = 2
    while power < GDN_INV_BASE:
        n_list = [_mm_bf16(n, n) for n in n_list]
        t_list = [t + _mm_bf16(t, n) for t, n in zip(t_list, n_list)]
        power *= 2
    for m_off in m_offs:
        u_list = [_mm_bf16(a * m_off, t) for a, t in zip(a_list, t_list)]
        t_list = [t - _mm_bf16(t, u) for t, u in zip(t_list, u_list)]
    return t_list


def _gdn_kernel(mask_ref, q_ref, k_ref, v_ref, beta_ref, g_ref, gt_ref, o_ref,
                state, vnew, val_s, kcd_s, qd_s, kd_s, attn_s, *, rev):
    bf16 = jnp.bfloat16
    hd = HEAD_DIM
    nt = (((1,), (1,)), ((), ()))
    tn = (((0,), (0,)), ((), ()))

    @pl.when(pl.program_id(1) == 0)
    def _():
        state[...] = jnp.zeros(state.shape, state.dtype)
        vnew[...] = jnp.zeros(vnew.shape, vnew.dtype)

    incl, ones, strict, eye, m_base = (mask_ref[t] for t in range(5))
    m_offs = [mask_ref[t] for t in range(5, mask_ref.shape[0])]
    g = g_ref[0, 0]
    beta = beta_ref[0, 0]
    gc = _dot_f32(incl, g)
    gl = _dot_f32(ones, g)
    gct = lax.dot_general(gt_ref[0, 0], incl, nt, precision=lax.Precision.HIGHEST,
                          preferred_element_type=jnp.float32)
    eg = jnp.exp(gc)
    ekd = jnp.exp(gl - gc)
    cd = jnp.exp(gl)
    a_list, rhs_list = [], []
    for h in range(N_HEADS):
        sl = slice(h * hd, (h + 1) * hd)
        q, k, v = q_ref[0][:, sl], k_ref[0][:, sl], v_ref[0][:, sl]
        bh = beta[:, h:h + 1]
        kb = k * bh
        kbf = k.astype(bf16)
        dec = jnp.exp(jnp.minimum(gc[:, h:h + 1] - gct[h:h + 1, :], 0.0)) * incl
        a_list.append(lax.dot_general(kb.astype(bf16), kbf, nt, preferred_element_type=jnp.float32)
                      * dec * strict)
        attn = lax.dot_general(q.astype(bf16), kbf, nt, preferred_element_type=jnp.float32) * dec
        attn_s[h] = attn.astype(bf16)
        rhs_list.append(jnp.concatenate([v * bh, kb * eg[:, h:h + 1]], axis=1))
        qd_s[h] = (q * eg[:, h:h + 1]).astype(bf16)
        kd_s[h] = (k * ekd[:, h:h + 1]).astype(bf16)
    t_list = _unit_triangular_inverses(a_list, eye, m_base, m_offs)
    for h in range(N_HEADS):
        sol = _mm_bf16(t_list[h], rhs_list[h])
        val_s[h] = sol[:, :hd]
        kcd_s[h] = sol[:, hd:].astype(bf16)
    n_chunks = GDN_SB // GDN_CHUNK
    for c in (reversed(range(n_chunks)) if rev else range(n_chunks)):
        rows = slice(c * GDN_CHUNK, (c + 1) * GDN_CHUNK)
        for h in range(N_HEADS):
            s_old = state[h]
            s_bf = s_old.astype(bf16)
            v_new = val_s[h, rows, :] - jnp.dot(kcd_s[h, rows, :], s_bf, preferred_element_type=jnp.float32)
            v_bf = v_new.astype(bf16)
            vnew[h, rows, :] = v_bf
            o = (jnp.dot(qd_s[h, rows, :], s_bf, preferred_element_type=jnp.float32)
                 + jnp.dot(attn_s[h, rows, :], vnew[h], preferred_element_type=jnp.float32))
            state[h] = (s_old * cd[c * GDN_CHUNK:c * GDN_CHUNK + 1, h:h + 1]
                        + lax.dot_general(kd_s[h, rows, :], v_bf, tn, preferred_element_type=jnp.float32))
            o_ref[0, rows, h * hd:(h + 1) * hd] = o


def _gdn_scan(q, k, v, beta, g, rev):
    b_, s_, w_ = q.shape
    n_sb = s_ // GDN_SB
    masks = jnp.asarray(_gdn_masks(rev))
    step = (lambda i: n_sb - 1 - i) if rev else (lambda i: i)
    tok = pl.BlockSpec((1, GDN_SB, w_), lambda b, i: (b, step(i), 0))
    gate = pl.BlockSpec((1, 1, GDN_SB, N_HEADS), lambda b, i: (b, 0, step(i), 0))
    gate_t = pl.BlockSpec((1, 1, N_HEADS, GDN_SB), lambda b, i: (b, 0, 0, step(i)))
    per_head = lambda width, dt: pltpu.VMEM((N_HEADS, GDN_SB, width), dt)
    return pl.pallas_call(
        functools.partial(_gdn_kernel, rev=rev),
        grid=(b_, n_sb),
        in_specs=[pl.BlockSpec(masks.shape, lambda b, i: (0, 0, 0)), tok, tok, tok, gate, gate, gate_t],
        out_specs=tok,
        out_shape=jax.ShapeDtypeStruct((b_, s_, w_), jnp.float32),
        scratch_shapes=[pltpu.VMEM((N_HEADS, HEAD_DIM, HEAD_DIM), jnp.float32),
                        per_head(HEAD_DIM, jnp.bfloat16), per_head(HEAD_DIM, jnp.float32),
                        per_head(HEAD_DIM, jnp.bfloat16), per_head(HEAD_DIM, jnp.bfloat16),
                        per_head(HEAD_DIM, jnp.bfloat16), per_head(GDN_SB, jnp.bfloat16)],
        compiler_params=pltpu.CompilerParams(dimension_semantics=("arbitrary", "arbitrary"),
                                             vmem_limit_bytes=GDN_VMEM_LIMIT_BYTES),
        name="gdn_scan_rev" if rev else "gdn_scan_fwd",
    )(masks, q, k, v, beta, g, g.transpose(0, 1, 3, 2))


def gdn_mixer(qkv, z, b_in, a_in, conv_w, a_log, dt_bias, norm_g):
    b_, s_, _ = qkv.shape
    dtype = qkv.dtype
    qkv = jax.nn.silu(short_conv(qkv, conv_w)).astype(jnp.float32)
    q, k, v = jnp.split(qkv, 3, axis=-1)
    q = l2norm(q.reshape(b_, s_, N_HEADS, HEAD_DIM)).reshape(b_, s_, GROUP_W) * (HEAD_DIM ** -0.5)
    k = l2norm(k.reshape(b_, s_, N_HEADS, HEAD_DIM)).reshape(b_, s_, GROUP_W)
    b_in = b_in.astype(jnp.float32).reshape(b_, s_, 2, N_HEADS)
    a_in = a_in.astype(jnp.float32).reshape(b_, s_, 2, N_HEADS)
    beta = jax.nn.sigmoid(b_in).transpose(0, 2, 1, 3)
    g = -jnp.exp(a_log.astype(jnp.float32)) * jax.nn.softplus(a_in + dt_bias.astype(jnp.float32))
    g = g.transpose(0, 2, 1, 3)
    o = (_gdn_scan(q, k, v, beta[:, 0:1], g[:, 0:1], False)
         + _gdn_scan(q, k, v, beta[:, 1:2], g[:, 1:2], True))
    o = o.reshape(b_, s_, N_HEADS, HEAD_DIM)
    zg = jax.nn.silu(z.astype(jnp.float32)).reshape(b_, s_, N_HEADS, HEAD_DIM)
    o = rmsnorm(o, norm_g) * zg
    return o.reshape(b_, s_, GROUP_W).astype(dtype)


def hyena_pos_features(length):
    t = jnp.linspace(0.0, 1.0, length, dtype=jnp.float32)[:, None]
    w = 2.0 * math.pi * jnp.arange(length, dtype=jnp.float32) / length
    f = jnp.linspace(1e-4, HY_BANDS - 1, HY_BANDS, dtype=jnp.float32)
    fw = w[:, None] * f[None, :]
    return jnp.concatenate([t, jnp.cos(fw), -jnp.sin(fw)], axis=-1)


def hyena_filters_f(z, w1, b1, f1, w2, b2, f2, w3, deltas):
    f32 = jnp.float32
    length = z.shape[0]
    t = z[:, :1]
    h = jnp.sin(f1.astype(f32) * (z @ w1.astype(f32) + b1.astype(f32)))
    h = jnp.sin(f2.astype(f32) * (h @ w2.astype(f32) + b2.astype(f32)))
    h = (h @ w3.astype(f32)) * jnp.exp(-t * jnp.abs(deltas.astype(f32)))
    h = h.reshape(length, HY_ORDER, 2, GROUP_W)
    kern = jnp.concatenate([h[:, :, 0], jnp.zeros((1, HY_ORDER, GROUP_W), f32),
                            h[:0:-1, :, 1]], axis=0)
    return kern / (jnp.sum(jnp.abs(kern), axis=0, keepdims=True) + EPS)


def _dft_tables(n_fft):
    n1 = n_fft // FFT_N2
    def dft(n):
        kk = (np.arange(n)[:, None] * np.arange(n)[None, :]) % n
        ang = -2.0 * np.pi * kk / n
        return np.cos(ang), np.sin(ang)
    f1r, f1i = dft(n1)
    f2r, f2i = dft(FFT_N2)
    kk = (np.arange(n1)[:, None] * np.arange(FFT_N2)[None, :]) % n_fft
    tw = -2.0 * np.pi * kk / n_fft
    f32 = np.float32
    return dict(
        f1=np.concatenate([f1r, f1i], axis=0).astype(f32),
        f1_inv=(np.concatenate([f1r[:n1 // 2], f1i[:n1 // 2]], axis=0) / n_fft).astype(f32),
        f2=np.concatenate([f2r, f2i], axis=0).astype(f32),
        twr=np.cos(tw).astype(f32)[:, :, None], twi=np.sin(tw).astype(f32)[:, :, None])


def _dot_f32(a, b):
    return jnp.dot(a, b, precision=lax.Precision.HIGHEST, preferred_element_type=jnp.float32)


def _fft_stage1_kernel(f_ref, zr_ref, zi_ref, yr_ref, yi_ref, *, n1):
    f = f_ref[...]
    p = _dot_f32(f, zr_ref[0])
    if zi_ref is None:
        yr_ref[...] = p[:n1]
        yi_ref[...] = p[n1:]
    else:
        q = _dot_f32(f, zi_ref[0])
        yr_ref[...] = p[:n1] - q[n1:]
        yi_ref[...] = q[:n1] + p[n1:]


def _fft_stage1_real_kernel(f_ref, zr_ref, yr_ref, yi_ref, *, n1):
    _fft_stage1_kernel(f_ref, zr_ref, None, yr_ref, yi_ref, n1=n1)


def _fft_mid_kernel(f_ref, twr_ref, twi_ref, yr_ref, yi_ref, kr_ref, ki_ref, qr_ref, qi_ref):
    n2 = FFT_N2
    twr, twi = twr_ref[0], twi_ref[0]
    yr, yi = yr_ref[0], yi_ref[0]
    f = f_ref[...]
    p = _dot_f32(f, yr * twr - yi * twi)
    q = _dot_f32(f, yr * twi + yi * twr)
    xr = p[:n2] - q[n2:]
    xi = q[:n2] + p[n2:]
    if kr_ref is None:
        qr_ref[0] = xr
        qi_ref[0] = xi
        return
    kr, ki = kr_ref[0], ki_ref[0]
    p = _dot_f32(f, xr * kr - xi * ki)
    q = _dot_f32(f, xr * ki + xi * kr)
    wr = p[:n2] + q[n2:]
    wi = q[:n2] - p[n2:]
    qr_ref[0] = wr * twr + wi * twi
    qi_ref[0] = wi * twr - wr * twi


def _fft_mid_spectrum_kernel(f_ref, twr_ref, twi_ref, yr_ref, yi_ref, qr_ref, qi_ref):
    _fft_mid_kernel(f_ref, twr_ref, twi_ref, yr_ref, yi_ref, None, None, qr_ref, qi_ref)


def _fft_last_kernel(f_ref, qr_ref, qi_ref, u_ref, gate_ref, bias_ref, o_ref, *, nh):
    f = f_ref[...]
    p = _dot_f32(f, qr_ref[...])
    q = _dot_f32(f, qi_ref[...])
    bias = bias_ref[...]
    o_ref[0] = gate_ref[0] * (p[:nh] + q[nh:] + u_ref[0] * bias)
    o_ref[1] = gate_ref[1] * (q[:nh] - p[nh:] + u_ref[1] * bias)


def _fft_params(n_axes):
    return pltpu.CompilerParams(dimension_semantics=("arbitrary",) * n_axes,
                                vmem_limit_bytes=FFT_VMEM_LIMIT_BYTES)


def _fft_forward(tab, z, n_ch):
    n1 = tab["f1"].shape[1]
    parts, rows, cols = z.shape
    tn = min(FFT_COL_TILE, cols)
    f1 = jnp.asarray(tab["f1"][:, :rows])
    y_shape = jax.ShapeDtypeStruct((n1, cols), jnp.float32)
    col_spec = pl.BlockSpec((n1, tn), lambda j: (0, j))
    z_specs = [pl.BlockSpec((1, rows, tn), lambda j, p=p: (p, 0, j)) for p in range(parts)]
    body = _fft_stage1_kernel if parts == 2 else _fft_stage1_real_kernel
    yr, yi = pl.pallas_call(
        functools.partial(body, n1=n1), grid=(cols // tn,),
        in_specs=[pl.BlockSpec(f1.shape, lambda j: (0, 0))] + z_specs,
        out_specs=[col_spec, col_spec], out_shape=[y_shape, y_shape],
        compiler_params=_fft_params(1), name="fft_stage1",
    )(f1, *([z] * parts))
    return yr.reshape(n1, FFT_N2, n_ch), yi.reshape(n1, FFT_N2, n_ch)


def _fft_mid(tab, yr, yi, kr=None, ki=None):
    n1, n2, n_ch = yr.shape
    slab = pl.BlockSpec((1, n2, n_ch), lambda i: (i, 0, 0))
    tw_spec = pl.BlockSpec((1, n2, 1), lambda i: (i, 0, 0))
    f2 = jnp.asarray(tab["f2"])
    ops = [f2, jnp.asarray(tab["twr"]), jnp.asarray(tab["twi"]), yr, yi]
    specs = [pl.BlockSpec(f2.shape, lambda i: (0, 0)), tw_spec, tw_spec, slab, slab]
    body = _fft_mid_spectrum_kernel
    if kr is not None:
        ops += [kr, ki]
        specs += [slab, slab]
        body = _fft_mid_kernel
    shape = jax.ShapeDtypeStruct((n1, n2, n_ch), jnp.float32)
    return pl.pallas_call(
        body, grid=(n1,), in_specs=specs, out_specs=[slab, slab], out_shape=[shape, shape],
        compiler_params=_fft_params(1), name="fft_mid",
    )(*ops)


def _fft_conv_gate(tab, u, gate, bias, kr, ki):
    b_, length, n_ch = u.shape
    assert b_ == 2
    n1 = tab["f1"].shape[1]
    nh = n1 // 2
    cols = FFT_N2 * n_ch
    uv = u.reshape(b_, nh, cols)
    yr, yi = _fft_forward(tab, uv, n_ch)
    qr, qi = _fft_mid(tab, yr, yi, kr, ki)
    tn = min(FFT_COL_TILE, cols)
    f1_inv = jnp.asarray(tab["f1_inv"])
    q_spec = pl.BlockSpec((n1, tn), lambda j: (0, j))
    u_spec = pl.BlockSpec((b_, nh, tn), lambda j: (0, 0, j))
    out = pl.pallas_call(
        functools.partial(_fft_last_kernel, nh=nh), grid=(cols // tn,),
        in_specs=[pl.BlockSpec(f1_inv.shape, lambda j: (0, 0)), q_spec, q_spec, u_spec, u_spec,
                  pl.BlockSpec((1, tn), lambda j: (0, 0))],
        out_specs=u_spec, out_shape=jax.ShapeDtypeStruct((b_, nh, cols), jnp.float32),
        compiler_params=_fft_params(1), name="fft_last",
    )(f1_inv, qr.reshape(n1, cols), qi.reshape(n1, cols), uv, gate.reshape(b_, nh, cols),
      jnp.tile(bias.astype(jnp.float32), tn // n_ch).reshape(1, tn))
    return out.reshape(b_, length, n_ch)


def hyena_mixer(u, conv_w, kern, bias, norm_g):
    dtype = u.dtype
    length = u.shape[1]
    u = short_conv(u, conv_w).astype(jnp.float32)
    x1, x2, v = jnp.split(u, 3, axis=-1)
    tab = _dft_tables(2 * length)
    n1 = tab["f1"].shape[1]
    n_filt = HY_ORDER * GROUP_W
    kr, ki = _fft_mid(tab, *_fft_forward(tab, kern.reshape(1, n1, FFT_N2 * n_filt), n_filt))
    y = _fft_conv_gate(tab, v, x1, bias[0], kr[:, :, :GROUP_W], ki[:, :, :GROUP_W])
    y = _fft_conv_gate(tab, y, x2, bias[1], kr[:, :, GROUP_W:], ki[:, :, GROUP_W:])
    return rmsnorm(y.astype(dtype), norm_g)


def alibi_slopes(n):
    return 2.0 ** (-8.0 * jnp.arange(1, n + 1, dtype=jnp.float32) / n)


def _window_kernel(q_ref, kp_ref, kc_ref, kn_ref, bias_ref, sink_ref, g_ref, o_ref, *, nb):
    bf16 = jnp.bfloat16
    n = pl.program_id(1)
    nt = (((1,), (1,)), ((), ()))
    col = lax.broadcasted_iota(jnp.int32, (1, 3 * BLOCK), 1)
    edge = jnp.where(((n == 0) & (col < BLOCK)) | ((n == nb - 1) & (col >= 2 * BLOCK)), WINDOW_MASK, 0.0)
    q = q_ref[0] * (HEAD_DIM ** -0.5)
    kv = jnp.concatenate([kp_ref[0], kc_ref[0], kn_ref[0]], axis=0)
    outs = []
    for j in range(N_KV_HEADS):
        q4 = jnp.concatenate([q[:, (j * GQA_GROUP + g) * HEAD_DIM:(j * GQA_GROUP + g + 1) * HEAD_DIM]
                              for g in range(GQA_GROUP)], axis=0)
        k = kv[:, j * HEAD_DIM:(j + 1) * HEAD_DIM]
        v = kv[:, KV_W + j * HEAD_DIM:KV_W + (j + 1) * HEAD_DIM]
        s = lax.dot_general(q4.astype(bf16), k.astype(bf16), nt, preferred_element_type=jnp.float32)
        s = s + bias_ref[j] + edge
        sink = sink_ref[j]
        m = jnp.maximum(jnp.max(s, axis=-1, keepdims=True), sink)
        p = jnp.exp(s - m)
        denom = jnp.sum(p, axis=-1, keepdims=True) + jnp.exp(sink - m)
        o = jnp.dot(p.astype(bf16), v.astype(bf16), preferred_element_type=jnp.float32) / denom
        outs += [o[g * BLOCK:(g + 1) * BLOCK] for g in range(GQA_GROUP)]
    o = jnp.concatenate(outs, axis=-1)
    o_ref[0] = o * lax.rsqrt(jnp.mean(o * o, axis=-1, keepdims=True) + EPS) * g_ref[...]


def window_mixer(c_q, c_kv, sink, slopes, norm_g):
    b_, s_, _ = c_q.shape
    nb = s_ // BLOCK
    rel = BLOCK + jnp.arange(BLOCK)[:, None] - jnp.arange(3 * BLOCK)[None, :]
    dist = jnp.abs(rel).astype(jnp.float32)
    bias = jnp.where(jnp.abs(rel) <= WINDOW, -slopes[:, None, None] * dist, WINDOW_MASK)
    bias = bias.reshape(N_KV_HEADS, GQA_GROUP * BLOCK, 3 * BLOCK)
    sink_rows = jnp.repeat(sink.astype(jnp.float32), BLOCK).reshape(N_KV_HEADS, GQA_GROUP * BLOCK, 1)
    kv_spec = lambda shift: pl.BlockSpec(
        (1, BLOCK, 2 * KV_W), lambda b, n: (b, jnp.clip(n + shift, 0, nb - 1), 0))
    const = lambda shape: pl.BlockSpec(shape, lambda b, n: (0,) * len(shape))
    return pl.pallas_call(
        functools.partial(_window_kernel, nb=nb),
        grid=(b_, nb),
        in_specs=[pl.BlockSpec((1, BLOCK, GROUP_W), lambda b, n: (b, n, 0)),
                  kv_spec(-1), kv_spec(0), kv_spec(1),
                  const(bias.shape), const(sink_rows.shape), const((1, GROUP_W))],
        out_specs=pl.BlockSpec((1, BLOCK, GROUP_W), lambda b, n: (b, n, 0)),
        out_shape=jax.ShapeDtypeStruct((b_, s_, GROUP_W), jnp.float32),
        compiler_params=pltpu.CompilerParams(dimension_semantics=("arbitrary", "arbitrary")),
        name="window_attention",
    )(c_q, c_kv, c_kv, c_kv, bias, sink_rows, norm_g.astype(jnp.float32).reshape(1, GROUP_W))


def rope_1d(x, pos):
    d = x.shape[-1]
    inv = ROPE_THETA ** (-jnp.arange(0, d, 2, dtype=jnp.float32) / d)
    ang = pos.astype(jnp.float32)[:, None] * inv[None, :]
    cos = jnp.cos(ang)[None, :, None, :]
    sin = jnp.sin(ang)[None, :, None, :]
    xf = x.astype(jnp.float32)
    x1, x2 = xf[..., :d // 2], xf[..., d // 2:]
    return jnp.concatenate([x1 * cos - x2 * sin, x2 * cos + x1 * sin], axis=-1).astype(x.dtype)


def axial_rope(x, row_idx, col_idx):
    half = HEAD_DIM // 2
    return jnp.concatenate([rope_1d(x[..., :half], row_idx), rope_1d(x[..., half:], col_idx)], axis=-1)


def _flash_kernel(q_ref, kt_ref, v_ref, o_ref, *, tq, tk, n_kc):
    m_rows = GQA_GROUP * tq
    q = q_ref[0, 0].reshape(m_rows, 2 * HEAD_DIM)

    def body(c, acc):
        off = pl.multiple_of(c * tk, tk)
        s = jnp.dot(q, kt_ref[0, 0, :, pl.ds(off, tk)], preferred_element_type=jnp.float32)
        p = jnp.exp2(s).astype(jnp.bfloat16)
        return acc + jnp.dot(p, v_ref[0, 0, pl.ds(off, tk), :], preferred_element_type=jnp.float32)

    acc = lax.fori_loop(0, n_kc, body, jnp.zeros((m_rows, 2 * HEAD_DIM), jnp.float32))
    o = acc[:, :HEAD_DIM] / acc[:, HEAD_DIM:HEAD_DIM + 1]
    o_ref[0, 0] = o.reshape(GQA_GROUP, tq, HEAD_DIM)


def _rowmax_kernel(q_ref, kt_ref, m_ref, *, tq, tk, n_kc):
    m_rows = GQA_GROUP * tq
    q = q_ref[0, 0].reshape(m_rows, 2 * HEAD_DIM)

    def body(c, mx):
        off = pl.multiple_of(c * tk, tk)
        s = jnp.dot(q, kt_ref[0, 0, :, pl.ds(off, tk)], preferred_element_type=jnp.float32)
        for j in range(tk // 128):
            mx = jnp.maximum(mx, s[:, j * 128:(j + 1) * 128])
        return mx

    mx = lax.fori_loop(0, n_kc, body, jnp.full((m_rows, 128), -jnp.inf, jnp.float32))
    m_ref[0, 0] = jnp.max(mx, axis=-1, keepdims=True).reshape(GQA_GROUP, tq, 1)


def _attn_call(body, q, kt, v, out_w, name, *, tq=ATTN_TQ, tk=ATTN_TK):
    b_, _, _, s_, _ = q.shape
    wide = 2 * HEAD_DIM
    q_spec = pl.BlockSpec((1, 1, GQA_GROUP, tq, wide), lambda b, h, i: (b, h, 0, i, 0))
    kt_spec = pl.BlockSpec((1, 1, wide, s_), lambda b, h, i: (b, h, 0, 0))
    v_spec = pl.BlockSpec((1, 1, s_, wide), lambda b, h, i: (b, h, 0, 0))
    o_spec = pl.BlockSpec((1, 1, GQA_GROUP, tq, out_w), lambda b, h, i: (b, h, 0, i, 0))
    operands = (q, kt) if v is None else (q, kt, v)
    return pl.pallas_call(
        functools.partial(body, tq=tq, tk=tk, n_kc=s_ // tk),
        out_shape=jax.ShapeDtypeStruct((b_, N_KV_HEADS, GQA_GROUP, s_, out_w), jnp.float32),
        grid=(b_, N_KV_HEADS, s_ // tq),
        in_specs=[q_spec, kt_spec] if v is None else [q_spec, kt_spec, v_spec],
        out_specs=o_spec,
        compiler_params=pltpu.CompilerParams(
            dimension_semantics=("arbitrary", "arbitrary", "arbitrary"),
            vmem_limit_bytes=ATTN_VMEM_LIMIT_BYTES),
        name=name,
    )(*operands)


def global_mixer(d_q, d_kv, q_norm_g, k_norm_g, row_idx, col_idx, norm_g):
    b_, s_, _ = d_q.shape
    q = d_q.reshape(b_, s_, N_HEADS, HEAD_DIM)
    k, v = jnp.split(d_kv, 2, axis=-1)
    k = k.reshape(b_, s_, N_KV_HEADS, HEAD_DIM)
    v = v.reshape(b_, s_, N_KV_HEADS, HEAD_DIM)
    q = axial_rope(rmsnorm(q, q_norm_g), row_idx, col_idx) * (LOG2E * HEAD_DIM ** -0.5)
    k = axial_rope(rmsnorm(k, k_norm_g), row_idx, col_idx)
    bf16 = jnp.bfloat16
    qb = q.astype(bf16).reshape(b_, s_, N_KV_HEADS, GQA_GROUP, HEAD_DIM).transpose(0, 2, 3, 1, 4)
    kb = k.astype(bf16).transpose(0, 2, 1, 3)
    vb = v.astype(bf16).transpose(0, 2, 1, 3)
    qn = jnp.sqrt(jnp.sum(jnp.square(qb.astype(jnp.float32)), axis=-1, keepdims=True))
    kn = jnp.sqrt(jnp.max(jnp.sum(jnp.square(kb.astype(jnp.float32)), axis=-1), axis=-1))
    bound = qn * kn[:, :, None, None, None]
    one = jnp.ones((b_, N_KV_HEADS, s_, 1), bf16)
    pad = jnp.zeros((b_, N_KV_HEADS, s_, HEAD_DIM - 1), bf16)
    kta = jnp.concatenate([kb, one, pad], axis=-1).transpose(0, 1, 3, 2)
    va = jnp.concatenate([vb, one, pad], axis=-1)
    qpad = jnp.zeros(qb.shape[:-1] + (HEAD_DIM - 1,), bf16)

    def with_shift(m):
        return jnp.concatenate([qb, (-m).astype(bf16), qpad], axis=-1)

    m = lax.cond(jnp.max(bound) < ATTN_SAFE_SHIFT, lambda: bound,
                 lambda: _attn_call(_rowmax_kernel, with_shift(jnp.zeros_like(bound)), kta, None, 1,
                                    "global_attention_rowmax"))
    o = _attn_call(_flash_kernel, with_shift(m), kta, va, HEAD_DIM, "global_flash_attention")
    o = o.transpose(0, 3, 1, 2, 4).reshape(b_, s_, GROUP_W)
    return rmsnorm(o, norm_g)


def _row_copy(src, src_row, dst, dst_row, sem):
    return pltpu.make_async_copy(src.at[pl.ds(src_row, 1)], dst.at[pl.ds(dst_row, 1)], sem)


def _moe_dispatch_kernel(pos_ref, x_ref, g_ref, zeros_hbm, xs_hbm, xn_scr, sem, *, tb):
    del zeros_hbm
    i = pl.program_id(0)
    x = x_ref[...]
    xn_scr[...] = x * lax.rsqrt(jnp.mean(x * x, axis=-1, keepdims=True) + EPS) * g_ref[...]

    def issue(r, carry):
        for k in range(TOP_K):
            _row_copy(xn_scr, r, xs_hbm, pos_ref[(i * tb + r) * TOP_K + k], sem).start()
        return carry

    lax.fori_loop(0, tb, issue, 0, unroll=8)
    for _ in range(TOP_K):
        pltpu.make_async_copy(xn_scr, xn_scr, sem).wait()


def _moe_expert_kernel(blk_e_ref, blk_rows_ref, xs_ref, wg_ref, wu_ref, wd_ref, o_ref, *, tm):
    del blk_e_ref
    rows = blk_rows_ref[pl.program_id(0)]

    @pl.when(rows > 0)
    def _():
        x = xs_ref[...].astype(jnp.bfloat16)
        g = jnp.dot(x, wg_ref[0], preferred_element_type=jnp.float32)
        u = jnp.dot(x, wu_ref[0], preferred_element_type=jnp.float32)
        h = (g * jax.nn.sigmoid(g) * u).astype(jnp.bfloat16)
        o_ref[...] = jnp.dot(h, wd_ref[0], preferred_element_type=jnp.float32)

    @pl.when(rows == 0)
    def _():
        o_ref[...] = jnp.zeros((tm, o_ref.shape[1]), jnp.float32)


def _moe_combine_kernel(pos_ref, x_ref, w_ref, os_hbm, y_ref, buf, sem, *, tb):
    i = pl.program_id(0)

    def issue(r, carry):
        for k in range(TOP_K):
            _row_copy(os_hbm, pos_ref[(i * tb + r) * TOP_K + k], buf.at[k], r, sem).start()
        return carry

    lax.fori_loop(0, tb, issue, 0, unroll=8)
    for k in range(TOP_K):
        pltpu.make_async_copy(buf.at[k], buf.at[k], sem).wait()
    w = w_ref[...]
    y = x_ref[...]
    for k in range(TOP_K):
        y = y + w[:, k:k + 1] * buf[k]
    y_ref[...] = y


def _moe_route(xn, w_group, b_group, w_expert, b_expert, tm):
    n_tok = xn.shape[0]
    gp = jax.nn.softmax((xn @ w_group).astype(jnp.float32) + b_group.astype(jnp.float32), axis=-1)
    g_w, g_idx = lax.top_k(gp, 1)
    elog = (xn @ w_expert).astype(jnp.float32) + b_expert.astype(jnp.float32)
    elog = elog.reshape(n_tok, N_EXPERT_GROUPS, EXPERTS_PER_GROUP)
    elog_sel = jnp.take_along_axis(elog, g_idx[:, :, None], axis=1)[:, 0]
    e_w, e_idx = lax.top_k(jax.nn.softmax(elog_sel, axis=-1), TOP_K)
    e_w = e_w / jnp.sum(e_w, axis=-1, keepdims=True)
    weights = g_w * e_w
    experts = (g_idx * EXPERTS_PER_GROUP + e_idx).reshape(-1)
    onehot = (experts[:, None] == jnp.arange(N_EXPERTS)[None, :]).astype(jnp.int32)
    csum = jnp.cumsum(onehot, axis=0)
    counts = csum[-1]
    rank = jnp.sum(onehot * csum, axis=1) - 1
    padded = ((counts + tm - 1) // tm) * tm
    pend = jnp.cumsum(padded)
    pstart = pend - padded
    pos = (pstart[experts] + rank).astype(jnp.int32)
    n_blk = (n_tok * TOP_K) // tm + N_EXPERTS
    blk_start = jnp.arange(n_blk) * tm
    blk_e = jnp.minimum(jnp.searchsorted(pend, blk_start, side='right'), N_EXPERTS - 1).astype(jnp.int32)
    blk_rows = jnp.clip(counts[blk_e] - (blk_start - pstart[blk_e]), 0, tm).astype(jnp.int32)
    return weights, pos, blk_e, blk_rows, n_blk


def hier_moe_residual(x, norm_g, w_group, b_group, w_expert, b_expert, w_gate, w_up, w_down,
                      *, tm=MOE_TM, tb=MOE_TB):
    b_, s_, d_ = x.shape
    n_tok = b_ * s_
    xt = x.reshape(n_tok, d_)
    weights, pos, blk_e, blk_rows, n_blk = _moe_route(rmsnorm(xt, norm_g), w_group, b_group,
                                                      w_expert, b_expert, tm)
    n_pad = n_blk * tm
    vmem = pltpu.CompilerParams(dimension_semantics=("arbitrary",),
                                vmem_limit_bytes=MOE_VMEM_LIMIT_BYTES)
    xs = pl.pallas_call(
        functools.partial(_moe_dispatch_kernel, tb=tb),
        grid_spec=pltpu.PrefetchScalarGridSpec(
            num_scalar_prefetch=1, grid=(n_tok // tb,),
            in_specs=[pl.BlockSpec((tb, d_), lambda i, pos: (i, 0)),
                      pl.BlockSpec((1, d_), lambda i, pos: (0, 0)),
                      pl.BlockSpec(memory_space=pl.ANY)],
            out_specs=pl.BlockSpec(memory_space=pl.ANY),
            scratch_shapes=[pltpu.VMEM((tb, d_), jnp.float32), pltpu.SemaphoreType.DMA(())]),
        out_shape=jax.ShapeDtypeStruct((n_pad, d_), jnp.float32),
        input_output_aliases={3: 0},
        compiler_params=vmem, name="moe_dispatch",
    )(pos, xt, norm_g.astype(jnp.float32).reshape(1, d_), jnp.zeros((n_pad, d_), jnp.float32))
    bf16 = jnp.bfloat16
    outs = pl.pallas_call(
        functools.partial(_moe_expert_kernel, tm=tm),
        grid_spec=pltpu.PrefetchScalarGridSpec(
            num_scalar_prefetch=2, grid=(n_blk,),
            in_specs=[pl.BlockSpec((tm, d_), lambda i, be, br: (i, 0)),
                      pl.BlockSpec((1, d_, D_EXPERT), lambda i, be, br: (be[i], 0, 0)),
                      pl.BlockSpec((1, d_, D_EXPERT), lambda i, be, br: (be[i], 0, 0)),
                      pl.BlockSpec((1, D_EXPERT, d_), lambda i, be, br: (be[i], 0, 0))],
            out_specs=pl.BlockSpec((tm, d_), lambda i, be, br: (i, 0))),
        out_shape=jax.ShapeDtypeStruct((n_pad, d_), jnp.float32),
        compiler_params=vmem, name="moe_experts",
    )(blk_e, blk_rows, xs, w_gate.astype(bf16), w_up.astype(bf16), w_down.astype(bf16))
    y = pl.pallas_call(
        functools.partial(_moe_combine_kernel, tb=tb),
        grid_spec=pltpu.PrefetchScalarGridSpec(
            num_scalar_prefetch=1, grid=(n_tok // tb,),
            in_specs=[pl.BlockSpec((tb, d_), lambda i, pos: (i, 0)),
                      pl.BlockSpec((tb, TOP_K), lambda i, pos: (i, 0)),
                      pl.BlockSpec(memory_space=pl.ANY)],
            out_specs=pl.BlockSpec((tb, d_), lambda i, pos: (i, 0)),
            scratch_shapes=[pltpu.VMEM((TOP_K, tb, d_), jnp.float32), pltpu.SemaphoreType.DMA(())]),
        out_shape=jax.ShapeDtypeStruct((n_tok, d_), jnp.float32),
        compiler_params=vmem, name="moe_combine",
    )(pos, xt, weights, outs)
    return y.reshape(b_, s_, d_)


def kernel(x, norm_mix, w_in, gdn_conv, gdn_a_log, gdn_dt_bias, gdn_norm, hy_conv, hy_w1, hy_b1, hy_freq1, hy_w2, hy_b2, hy_freq2, hy_w3, hy_deltas, hy_bias, hy_norm, swa_sink, swa_norm, ga_q_norm, ga_k_norm, ga_norm, w_out, norm_ffn, moe_w_group, moe_b_group, moe_w_expert, moe_b_expert, moe_w_gate, moe_w_up, moe_w_down, norm_final):
    b_, s_, _ = x.shape
    rows = s_ // GRID_W
    row_idx = jnp.repeat(jnp.arange(rows), GRID_W)
    col_idx = jnp.tile(jnp.arange(GRID_W), rows)
    pos_feat = hyena_pos_features(s_)
    slopes = alibi_slopes(N_HEADS)
    splits = _split_points()
    for l in range(DEPTH):
        xn = rmsnorm(x, norm_mix[l])
        proj = xn @ w_in[l]
        a_qkv, a_z, a_beta, a_alpha, b_in, c_q, c_kv, d_q, d_kv = jnp.split(proj, splits, axis=-1)
        y_a = gdn_mixer(a_qkv, a_z, a_beta, a_alpha, gdn_conv[l], gdn_a_log[l], gdn_dt_bias[l], gdn_norm[l])
        kf = hyena_filters_f(pos_feat, hy_w1[l], hy_b1[l], hy_freq1[l], hy_w2[l], hy_b2[l],
                             hy_freq2[l], hy_w3[l], hy_deltas[l])
        y_b = hyena_mixer(b_in, hy_conv[l], kf, hy_bias[l], hy_norm[l])
        y_c = window_mixer(c_q, c_kv, swa_sink[l], slopes, swa_norm[l])
        y_d = global_mixer(d_q, d_kv, ga_q_norm[l], ga_k_norm[l], row_idx, col_idx, ga_norm[l])
        x = x + jnp.concatenate([y_a, y_b, y_c, y_d], axis=-1) @ w_out[l]
        x = hier_moe_residual(x, norm_ffn[l], moe_w_group[l], moe_b_group[l], moe_w_expert[l],
                              moe_b_expert[l], moe_w_gate[l], moe_w_up[l], moe_w_down[l])
    return rmsnorm(x, norm_final)
```

```python
import functools
import math

import jax
import jax.numpy as jnp
import numpy as np
from jax import lax
from jax.experimental import pallas as pl
from jax.experimental.pallas import tpu as pltpu

D_MODEL = 2048
DEPTH = 2
N_MIXERS = 4
GROUP_W = D_MODEL // N_MIXERS
HEAD_DIM = 64
N_HEADS = GROUP_W // HEAD_DIM
N_KV_HEADS = 2
GQA_GROUP = N_HEADS // N_KV_HEADS
KV_W = N_KV_HEADS * HEAD_DIM
SHORT_CONV = 3
GDN_CHUNK = 64
HY_ORDER = 2
HY_EMB = 33
HY_BANDS = (HY_EMB - 1) // 2
WINDOW = 128
BLOCK = 128
GRID_W = 64
ROPE_THETA = 10000.0
N_EXPERT_GROUPS = 4
EXPERTS_PER_GROUP = 8
N_EXPERTS = N_EXPERT_GROUPS * EXPERTS_PER_GROUP
TOP_K = 2
D_EXPERT = 512
MOE_BLOCK = 128
EPS = 1e-6
IN_SPLIT_SIZES = (3 * GROUP_W, GROUP_W, 2 * N_HEADS, 2 * N_HEADS, 3 * GROUP_W,
                  GROUP_W, 2 * KV_W, GROUP_W, 2 * KV_W)

ATTN_VMEM_LIMIT_BYTES = 48 * 1024 * 1024
ATTN_TQ = 256
ATTN_TK = 2048
WINDOW_MASK = -1e30
GDN_VMEM_LIMIT_BYTES = 48 * 1024 * 1024
PROJ_VMEM_LIMIT_BYTES = 48 * 1024 * 1024
PROJ_TM = 1024
PROJ_TILE = 512
OUT_TM = 256
GDN_SB = 256
GDN_INV_BASE = 8
FFT_VMEM_LIMIT_BYTES = 48 * 1024 * 1024
FFT_N2 = 256
FFT_COL_TILE = 4096
MOE_VMEM_LIMIT_BYTES = 48 * 1024 * 1024
MOE_TM = 512
MOE_TB = 256
LOG2E = 1.4426950408889634
ATTN_SAFE_SHIFT = 50.0


def _split_points():
    return [int(v) for v in np.cumsum(IN_SPLIT_SIZES)[:-1]]


def rmsnorm(x, g):
    xf = x.astype(jnp.float32)
    y = xf * lax.rsqrt(jnp.mean(xf * xf, axis=-1, keepdims=True) + EPS)
    return (y * g.astype(jnp.float32)).astype(x.dtype)


def l2norm(x):
    return x * lax.rsqrt(jnp.sum(x * x, axis=-1, keepdims=True) + EPS)


def short_conv(u, w):
    k_w = w.shape[0]
    p = k_w // 2
    s_len = u.shape[1]
    up = jnp.pad(u, ((0, 0), (p, p), (0, 0)))
    out = up[:, 0:s_len] * w[0]
    for j in range(1, k_w):
        out = out + up[:, j:j + s_len] * w[j]
    return out


def _gdn_masks(rev):
    r = np.arange(GDN_SB)
    i, j = r[:, None], r[None, :]
    same = lambda s: (i // s) == (j // s)
    before = (i < j) if rev else (i > j)
    chunk = same(GDN_CHUNK)
    masks = [chunk & (before | (i == j)),
             chunk,
             chunk & before,
             i == j,
             same(GDN_INV_BASE)]
    s = GDN_INV_BASE
    while s < GDN_CHUNK:
        masks.append(same(2 * s) & ~same(s))
        s *= 2
    return np.stack(masks).astype(np.float32)


def _mm_bf16(a, b):
    return jnp.dot(a.astype(jnp.bfloat16), b.astype(jnp.bfloat16), preferred_element_type=jnp.float32)


def _unit_triangular_inverses(a_list, eye, m_base, m_offs):
    n_list = [-a * m_base for a in a_list]
    t_list = [eye + n for n in n_list]
    power = 2
    while power < GDN_INV_BASE:
        n_list = [_mm_bf16(n, n) for n in n_list]
        t_list = [t + _mm_bf16(t, n) for t, n in zip(t_list, n_list)]
        power *= 2
    for m_off in m_offs:
        u_list = [_mm_bf16(a * m_off, t) for a, t in zip(a_list, t_list)]
        t_list = [t - _mm_bf16(t, u) for t, u in zip(t_list, u_list)]
    return t_list


def _gdn_kernel(mask_ref, q_ref, k_ref, v_ref, beta_ref, g_ref, gt_ref, o_ref,
                state, vnew, val_s, kcd_s, qd_s, kd_s, attn_s, *, rev):
    bf16 = jnp.bfloat16
    hd = HEAD_DIM
    nt = (((1,), (1,)), ((), ()))
    tn = (((0,), (0,)), ((), ()))

    @pl.when(pl.program_id(1) == 0)
    def _():
        state[...] = jnp.zeros(state.shape, state.dtype)
        vnew[...] = jnp.zeros(vnew.shape, vnew.dtype)

    incl, ones, strict, eye, m_base = (mask_ref[t] for t in range(5))
    m_offs = [mask_ref[t] for t in range(5, mask_ref.shape[0])]
    g = g_ref[0, 0]
    beta = beta_ref[0, 0]
    gc = _dot_f32(incl, g)
    gl = _dot_f32(ones, g)
    gct = lax.dot_general(gt_ref[0, 0], incl, nt, precision=lax.Precision.HIGHEST,
                          preferred_element_type=jnp.float32)
    eg = jnp.exp(gc)
    ekd = jnp.exp(gl - gc)
    cd = jnp.exp(gl)
    a_list, rhs_list = [], []
    for h in range(N_HEADS):
        sl = slice(h * hd, (h + 1) * hd)
        q, k, v = q_ref[0][:, sl], k_ref[0][:, sl], v_ref[0][:, sl]
        bh = beta[:, h:h + 1]
        kb = k * bh
        kbf = k.astype(bf16)
        dec = jnp.exp(jnp.minimum(gc[:, h:h + 1] - gct[h:h + 1, :], 0.0)) * incl
        a_list.append(lax.dot_general(kb.astype(bf16), kbf, nt, preferred_element_type=jnp.float32)
                      * dec * strict)
        attn = lax.dot_general(q.astype(bf16), kbf, nt, preferred_element_type=jnp.float32) * dec
        attn_s[h] = attn.astype(bf16)
        rhs_list.append(jnp.concatenate([v * bh, kb * eg[:, h:h + 1]], axis=1))
        qd_s[h] = (q * eg[:, h:h + 1]).astype(bf16)
        kd_s[h] = (k * ekd[:, h:h + 1]).astype(bf16)
    t_list = _unit_triangular_inverses(a_list, eye, m_base, m_offs)
    for h in range(N_HEADS):
        sol = _mm_bf16(t_list[h], rhs_list[h])
        val_s[h] = sol[:, :hd]
        kcd_s[h] = sol[:, hd:].astype(bf16)
    n_chunks = GDN_SB // GDN_CHUNK
    for c in (reversed(range(n_chunks)) if rev else range(n_chunks)):
        rows = slice(c * GDN_CHUNK, (c + 1) * GDN_CHUNK)
        for h in range(N_HEADS):
            s_old = state[h]
            s_bf = s_old.astype(bf16)
            v_new = val_s[h, rows, :] - jnp.dot(kcd_s[h, rows, :], s_bf, preferred_element_type=jnp.float32)
            v_bf = v_new.astype(bf16)
            vnew[h, rows, :] = v_bf
            o = (jnp.dot(qd_s[h, rows, :], s_bf, preferred_element_type=jnp.float32)
                 + jnp.dot(attn_s[h, rows, :], vnew[h], preferred_element_type=jnp.float32))
            state[h] = (s_old * cd[c * GDN_CHUNK:c * GDN_CHUNK + 1, h:h + 1]
                        + lax.dot_general(kd_s[h, rows, :], v_bf, tn, preferred_element_type=jnp.float32))
            o_ref[0, rows, h * hd:(h + 1) * hd] = o


def _gdn_scan(q, k, v, beta, g, rev):
    b_, s_, w_ = q.shape
    n_sb = s_ // GDN_SB
    masks = jnp.asarray(_gdn_masks(rev))
    step = (lambda i: n_sb - 1 - i) if rev else (lambda i: i)
    tok = pl.BlockSpec((1, GDN_SB, w_), lambda b, i: (b, step(i), 0))
    gate = pl.BlockSpec((1, 1, GDN_SB, N_HEADS), lambda b, i: (b, 0, step(i), 0))
    gate_t = pl.BlockSpec((1, 1, N_HEADS, GDN_SB), lambda b, i: (b, 0, 0, step(i)))
    per_head = lambda width, dt: pltpu.VMEM((N_HEADS, GDN_SB, width), dt)
    return pl.pallas_call(
        functools.partial(_gdn_kernel, rev=rev),
        grid=(b_, n_sb),
        in_specs=[pl.BlockSpec(masks.shape, lambda b, i: (0, 0, 0)), tok, tok, tok, gate, gate, gate_t],
        out_specs=tok,
        out_shape=jax.ShapeDtypeStruct((b_, s_, w_), jnp.float32),
        scratch_shapes=[pltpu.VMEM((N_HEADS, HEAD_DIM, HEAD_DIM), jnp.float32),
                        per_head(HEAD_DIM, jnp.bfloat16), per_head(HEAD_DIM, jnp.float32),
                        per_head(HEAD_DIM, jnp.bfloat16), per_head(HEAD_DIM, jnp.bfloat16),
                        per_head(HEAD_DIM, jnp.bfloat16), per_head(GDN_SB, jnp.bfloat16)],
        compiler_params=pltpu.CompilerParams(dimension_semantics=("arbitrary", "arbitrary"),
                                             vmem_limit_bytes=GDN_VMEM_LIMIT_BYTES),
        name="gdn_scan_rev" if rev else "gdn_scan_fwd",
    )(masks, q, k, v, beta, g, g.transpose(0, 1, 3, 2))


def gdn_mixer(qkv, z, b_in, a_in, conv_w, a_log, dt_bias, norm_g):
    b_, s_, _ = qkv.shape
    dtype = qkv.dtype
    qkv = jax.nn.silu(short_conv(qkv, conv_w)).astype(jnp.float32)
    q, k, v = jnp.split(qkv, 3, axis=-1)
    q = l2norm(q.reshape(b_, s_, N_HEADS, HEAD_DIM)).reshape(b_, s_, GROUP_W) * (HEAD_DIM ** -0.5)
    k = l2norm(k.reshape(b_, s_, N_HEADS, HEAD_DIM)).reshape(b_, s_, GROUP_W)
    b_in = b_in.astype(jnp.float32).reshape(b_, s_, 2, N_HEADS)
    a_in = a_in.astype(jnp.float32).reshape(b_, s_, 2, N_HEADS)
    beta = jax.nn.sigmoid(b_in).transpose(0, 2, 1, 3)
    g = -jnp.exp(a_log.astype(jnp.float32)) * jax.nn.softplus(a_in + dt_bias.astype(jnp.float32))
    g = g.transpose(0, 2, 1, 3)
    o = (_gdn_scan(q, k, v, beta[:, 0:1], g[:, 0:1], False)
         + _gdn_scan(q, k, v, beta[:, 1:2], g[:, 1:2], True))
    o = o.reshape(b_, s_, N_HEADS, HEAD_DIM)
    zg = jax.nn.silu(z.astype(jnp.float32)).reshape(b_, s_, N_HEADS, HEAD_DIM)
    o = rmsnorm(o, norm_g) * zg
    return o.reshape(b_, s_, GROUP_W).astype(dtype)


def hyena_pos_features(length):
    t = jnp.linspace(0.0, 1.0, length, dtype=jnp.float32)[:, None]
    w = 2.0 * math.pi * jnp.arange(length, dtype=jnp.float32) / length
    f = jnp.linspace(1e-4, HY_BANDS - 1, HY_BANDS, dtype=jnp.float32)
    fw = w[:, None] * f[None, :]
    return jnp.concatenate([t, jnp.cos(fw), -jnp.sin(fw)], axis=-1)


def hyena_filters_f(z, w1, b1, f1, w2, b2, f2, w3, deltas):
    f32 = jnp.float32
    length = z.shape[0]
    t = z[:, :1]
    h = jnp.sin(f1.astype(f32) * (z @ w1.astype(f32) + b1.astype(f32)))
    h = jnp.sin(f2.astype(f32) * (h @ w2.astype(f32) + b2.astype(f32)))
    h = (h @ w3.astype(f32)) * jnp.exp(-t * jnp.abs(deltas.astype(f32)))
    h = h.reshape(length, HY_ORDER, 2, GROUP_W)
    kern = jnp.concatenate([h[:, :, 0], jnp.zeros((1, HY_ORDER, GROUP_W), f32),
                            h[:0:-1, :, 1]], axis=0)
    return kern / (jnp.sum(jnp.abs(kern), axis=0, keepdims=True) + EPS)


def _dft_tables(n_fft):
    n1 = n_fft // FFT_N2
    def dft(n):
        kk = (np.arange(n)[:, None] * np.arange(n)[None, :]) % n
        ang = -2.0 * np.pi * kk / n
        return np.cos(ang), np.sin(ang)
    f1r, f1i = dft(n1)
    f2r, f2i = dft(FFT_N2)
    kk = (np.arange(n1)[:, None] * np.arange(FFT_N2)[None, :]) % n_fft
    tw = -2.0 * np.pi * kk / n_fft
    f32 = np.float32
    return dict(
        f1=np.concatenate([f1r, f1i], axis=0).astype(f32),
        f1_inv=(np.concatenate([f1r[:n1 // 2], f1i[:n1 // 2]], axis=0) / n_fft).astype(f32),
        f2=np.concatenate([f2r, f2i], axis=0).astype(f32),
        twr=np.cos(tw).astype(f32)[:, :, None], twi=np.sin(tw).astype(f32)[:, :, None])


def _dot_f32(a, b):
    return jnp.dot(a, b, precision=lax.Precision.HIGHEST, preferred_element_type=jnp.float32)


def _split_lhs(f):
    f = jnp.asarray(f, jnp.float32)
    hi = f.astype(jnp.bfloat16)
    lo = (f - hi.astype(jnp.float32)).astype(jnp.bfloat16)
    return jnp.concatenate([hi, hi, lo], axis=1)


def _dot_split(f3, x):
    hi = x.astype(jnp.bfloat16)
    lo = (x - hi.astype(jnp.float32)).astype(jnp.bfloat16)
    return jnp.dot(f3, jnp.concatenate([hi, lo, hi], axis=0), preferred_element_type=jnp.float32)


def _fft_stage1_kernel(f_ref, zr_ref, zi_ref, yr_ref, yi_ref, *, n1):
    f = f_ref[...]
    p = _dot_split(f, zr_ref[0])
    if zi_ref is None:
        yr_ref[...] = p[:n1]
        yi_ref[...] = p[n1:]
    else:
        q = _dot_split(f, zi_ref[0])
        yr_ref[...] = p[:n1] - q[n1:]
        yi_ref[...] = q[:n1] + p[n1:]


def _fft_stage1_real_kernel(f_ref, zr_ref, yr_ref, yi_ref, *, n1):
    _fft_stage1_kernel(f_ref, zr_ref, None, yr_ref, yi_ref, n1=n1)


def _fft_mid_kernel(f_ref, twr_ref, twi_ref, yr_ref, yi_ref, kr_ref, ki_ref, qr_ref, qi_ref):
    n2 = FFT_N2
    twr, twi = twr_ref[0], twi_ref[0]
    yr, yi = yr_ref[0], yi_ref[0]
    f = f_ref[...]
    p = _dot_split(f, yr * twr - yi * twi)
    q = _dot_split(f, yr * twi + yi * twr)
    xr = p[:n2] - q[n2:]
    xi = q[:n2] + p[n2:]
    if kr_ref is None:
        qr_ref[0] = xr
        qi_ref[0] = xi
        return
    kr, ki = kr_ref[0], ki_ref[0]
    p = _dot_split(f, xr * kr - xi * ki)
    q = _dot_split(f, xr * ki + xi * kr)
    wr = p[:n2] + q[n2:]
    wi = q[:n2] - p[n2:]
    qr_ref[0] = wr * twr + wi * twi
    qi_ref[0] = wi * twr - wr * twi


def _fft_mid_spectrum_kernel(f_ref, twr_ref, twi_ref, yr_ref, yi_ref, qr_ref, qi_ref):
    _fft_mid_kernel(f_ref, twr_ref, twi_ref, yr_ref, yi_ref, None, None, qr_ref, qi_ref)


def _fft_last_kernel(f_ref, qr_ref, qi_ref, u_ref, gate_ref, bias_ref, o_ref, *, nh):
    f = f_ref[...]
    p = _dot_split(f, qr_ref[...])
    q = _dot_split(f, qi_ref[...])
    bias = bias_ref[...]
    o_ref[0] = gate_ref[0] * (p[:nh] + q[nh:] + u_ref[0] * bias)
    o_ref[1] = gate_ref[1] * (q[:nh] - p[nh:] + u_ref[1] * bias)


def _fft_params(n_axes):
    return pltpu.CompilerParams(dimension_semantics=("arbitrary",) * n_axes,
                                vmem_limit_bytes=FFT_VMEM_LIMIT_BYTES)


def _fft_forward(tab, z, n_ch):
    n1 = tab["f1"].shape[1]
    parts, rows, cols = z.shape
    tn = min(FFT_COL_TILE, cols)
    f1 = _split_lhs(tab["f1"][:, :rows])
    y_shape = jax.ShapeDtypeStruct((n1, cols), jnp.float32)
    col_spec = pl.BlockSpec((n1, tn), lambda j: (0, j))
    z_specs = [pl.BlockSpec((1, rows, tn), lambda j, p=p: (p, 0, j)) for p in range(parts)]
    body = _fft_stage1_kernel if parts == 2 else _fft_stage1_real_kernel
    yr, yi = pl.pallas_call(
        functools.partial(body, n1=n1), grid=(cols // tn,),
        in_specs=[pl.BlockSpec(f1.shape, lambda j: (0, 0))] + z_specs,
        out_specs=[col_spec, col_spec], out_shape=[y_shape, y_shape],
        compiler_params=_fft_params(1), name="fft_stage1",
    )(f1, *([z] * parts))
    return yr.reshape(n1, FFT_N2, n_ch), yi.reshape(n1, FFT_N2, n_ch)


def _fft_mid(tab, yr, yi, kr=None, ki=None):
    n1, n2, n_ch = yr.shape
    slab = pl.BlockSpec((1, n2, n_ch), lambda i: (i, 0, 0))
    tw_spec = pl.BlockSpec((1, n2, 1), lambda i: (i, 0, 0))
    f2 = _split_lhs(tab["f2"])
    ops = [f2, jnp.asarray(tab["twr"]), jnp.asarray(tab["twi"]), yr, yi]
    specs = [pl.BlockSpec(f2.shape, lambda i: (0, 0)), tw_spec, tw_spec, slab, slab]
    body = _fft_mid_spectrum_kernel
    if kr is not None:
        ops += [kr, ki]
        specs += [slab, slab]
        body = _fft_mid_kernel
    shape = jax.ShapeDtypeStruct((n1, n2, n_ch), jnp.float32)
    return pl.pallas_call(
        body, grid=(n1,), in_specs=specs, out_specs=[slab, slab], out_shape=[shape, shape],
        compiler_params=_fft_params(1), name="fft_mid",
    )(*ops)


def _fft_conv_gate(tab, u, gate, bias, kr, ki):
    b_, length, n_ch = u.shape
    assert b_ == 2
    n1 = tab["f1"].shape[1]
    nh = n1 // 2
    cols = FFT_N2 * n_ch
    uv = u.reshape(b_, nh, cols)
    yr, yi = _fft_forward(tab, uv, n_ch)
    qr, qi = _fft_mid(tab, yr, yi, kr, ki)
    tn = min(FFT_COL_TILE, cols)
    f1_inv = _split_lhs(tab["f1_inv"])
    q_spec = pl.BlockSpec((n1, tn), lambda j: (0, j))
    u_spec = pl.BlockSpec((b_, nh, tn), lambda j: (0, 0, j))
    out = pl.pallas_call(
        functools.partial(_fft_last_kernel, nh=nh), grid=(cols // tn,),
        in_specs=[pl.BlockSpec(f1_inv.shape, lambda j: (0, 0)), q_spec, q_spec, u_spec, u_spec,
                  pl.BlockSpec((1, tn), lambda j: (0, 0))],
        out_specs=u_spec, out_shape=jax.ShapeDtypeStruct((b_, nh, cols), jnp.float32),
        compiler_params=_fft_params(1), name="fft_last",
    )(f1_inv, qr.reshape(n1, cols), qi.reshape(n1, cols), uv, gate.reshape(b_, nh, cols),
      jnp.tile(bias.astype(jnp.float32), tn // n_ch).reshape(1, tn))
    return out.reshape(b_, length, n_ch)


def hyena_mixer(u, conv_w, kern, bias, norm_g):
    dtype = u.dtype
    length = u.shape[1]
    u = short_conv(u, conv_w).astype(jnp.float32)
    x1, x2, v = jnp.split(u, 3, axis=-1)
    tab = _dft_tables(2 * length)
    n1 = tab["f1"].shape[1]
    n_filt = HY_ORDER * GROUP_W
    kr, ki = _fft_mid(tab, *_fft_forward(tab, kern.reshape(1, n1, FFT_N2 * n_filt), n_filt))
    y = _fft_conv_gate(tab, v, x1, bias[0], kr[:, :, :GROUP_W], ki[:, :, :GROUP_W])
    y = _fft_conv_gate(tab, y, x2, bias[1], kr[:, :, GROUP_W:], ki[:, :, GROUP_W:])
    return rmsnorm(y.astype(dtype), norm_g)


def alibi_slopes(n):
    return 2.0 ** (-8.0 * jnp.arange(1, n + 1, dtype=jnp.float32) / n)


def _window_kernel(q_ref, kp_ref, kc_ref, kn_ref, bias_ref, sink_ref, g_ref, o_ref, *, nb):
    bf16 = jnp.bfloat16
    n = pl.program_id(1)
    nt = (((1,), (1,)), ((), ()))
    col = lax.broadcasted_iota(jnp.int32, (1, 3 * BLOCK), 1)
    edge = jnp.where(((n == 0) & (col < BLOCK)) | ((n == nb - 1) & (col >= 2 * BLOCK)), WINDOW_MASK, 0.0)
    q = q_ref[0] * (HEAD_DIM ** -0.5)
    kv = jnp.concatenate([kp_ref[0], kc_ref[0], kn_ref[0]], axis=0)
    outs = []
    for j in range(N_KV_HEADS):
        q4 = jnp.concatenate([q[:, (j * GQA_GROUP + g) * HEAD_DIM:(j * GQA_GROUP + g + 1) * HEAD_DIM]
                              for g in range(GQA_GROUP)], axis=0)
        k = kv[:, j * HEAD_DIM:(j + 1) * HEAD_DIM]
        v = kv[:, KV_W + j * HEAD_DIM:KV_W + (j + 1) * HEAD_DIM]
        s = lax.dot_general(q4.astype(bf16), k.astype(bf16), nt, preferred_element_type=jnp.float32)
        s = s + bias_ref[j] + edge
        sink = sink_ref[j]
        m = jnp.maximum(jnp.max(s, axis=-1, keepdims=True), sink)
        p = jnp.exp(s - m)
        denom = jnp.sum(p, axis=-1, keepdims=True) + jnp.exp(sink - m)
        o = jnp.dot(p.astype(bf16), v.astype(bf16), preferred_element_type=jnp.float32) / denom
        outs += [o[g * BLOCK:(g + 1) * BLOCK] for g in range(GQA_GROUP)]
    o = jnp.concatenate(outs, axis=-1)
    o_ref[0] = o * lax.rsqrt(jnp.mean(o * o, axis=-1, keepdims=True) + EPS) * g_ref[...]


def window_mixer(proj, q_col, kv_col, sink, slopes, norm_g):
    b_, s_, _ = proj.shape
    assert q_col % GROUP_W == 0 and kv_col % (2 * KV_W) == 0
    nb = s_ // BLOCK
    rel = BLOCK + jnp.arange(BLOCK)[:, None] - jnp.arange(3 * BLOCK)[None, :]
    dist = jnp.abs(rel).astype(jnp.float32)
    bias = jnp.where(jnp.abs(rel) <= WINDOW, -slopes[:, None, None] * dist, WINDOW_MASK)
    bias = bias.reshape(N_KV_HEADS, GQA_GROUP * BLOCK, 3 * BLOCK)
    sink_rows = jnp.repeat(sink.astype(jnp.float32), BLOCK).reshape(N_KV_HEADS, GQA_GROUP * BLOCK, 1)
    kv_spec = lambda shift: pl.BlockSpec(
        (1, BLOCK, 2 * KV_W), lambda b, n: (b, jnp.clip(n + shift, 0, nb - 1), kv_col // (2 * KV_W)))
    const = lambda shape: pl.BlockSpec(shape, lambda b, n: (0,) * len(shape))
    return pl.pallas_call(
        functools.partial(_window_kernel, nb=nb),
        grid=(b_, nb),
        in_specs=[pl.BlockSpec((1, BLOCK, GROUP_W), lambda b, n: (b, n, q_col // GROUP_W)),
                  kv_spec(-1), kv_spec(0), kv_spec(1),
                  const(bias.shape), const(sink_rows.shape), const((1, GROUP_W))],
        out_specs=pl.BlockSpec((1, BLOCK, GROUP_W), lambda b, n: (b, n, 0)),
        out_shape=jax.ShapeDtypeStruct((b_, s_, GROUP_W), jnp.float32),
        compiler_params=pltpu.CompilerParams(dimension_semantics=("arbitrary", "arbitrary")),
        name="window_attention",
    )(proj, proj, proj, proj, bias, sink_rows, norm_g.astype(jnp.float32).reshape(1, GROUP_W))


def rope_1d(x, pos):
    d = x.shape[-1]
    inv = ROPE_THETA ** (-jnp.arange(0, d, 2, dtype=jnp.float32) / d)
    ang = pos.astype(jnp.float32)[:, None] * inv[None, :]
    cos = jnp.cos(ang)[None, :, None, :]
    sin = jnp.sin(ang)[None, :, None, :]
    xf = x.astype(jnp.float32)
    x1, x2 = xf[..., :d // 2], xf[..., d // 2:]
    return jnp.concatenate([x1 * cos - x2 * sin, x2 * cos + x1 * sin], axis=-1).astype(x.dtype)


def axial_rope(x, row_idx, col_idx):
    half = HEAD_DIM // 2
    return jnp.concatenate([rope_1d(x[..., :half], row_idx), rope_1d(x[..., half:], col_idx)], axis=-1)


def _flash_kernel(q_ref, kt_ref, v_ref, o_ref, *, tq, tk, n_kc):
    m_rows = GQA_GROUP * tq
    q = q_ref[0, 0].reshape(m_rows, 2 * HEAD_DIM)

    def body(c, acc):
        off = pl.multiple_of(c * tk, tk)
        s = jnp.dot(q, kt_ref[0, 0, :, pl.ds(off, tk)], preferred_element_type=jnp.float32)
        p = jnp.exp2(s).astype(jnp.bfloat16)
        return acc + jnp.dot(p, v_ref[0, 0, pl.ds(off, tk), :], preferred_element_type=jnp.float32)

    acc = lax.fori_loop(0, n_kc, body, jnp.zeros((m_rows, 2 * HEAD_DIM), jnp.float32))
    o = acc[:, :HEAD_DIM] / acc[:, HEAD_DIM:HEAD_DIM + 1]
    o_ref[0, 0] = o.reshape(GQA_GROUP, tq, HEAD_DIM)


def _rowmax_kernel(q_ref, kt_ref, m_ref, *, tq, tk, n_kc):
    m_rows = GQA_GROUP * tq
    q = q_ref[0, 0].reshape(m_rows, 2 * HEAD_DIM)

    def body(c, mx):
        off = pl.multiple_of(c * tk, tk)
        s = jnp.dot(q, kt_ref[0, 0, :, pl.ds(off, tk)], preferred_element_type=jnp.float32)
        for j in range(tk // 128):
            mx = jnp.maximum(mx, s[:, j * 128:(j + 1) * 128])
        return mx

    mx = lax.fori_loop(0, n_kc, body, jnp.full((m_rows, 128), -jnp.inf, jnp.float32))
    m_ref[0, 0] = jnp.max(mx, axis=-1, keepdims=True).reshape(GQA_GROUP, tq, 1)


def _attn_call(body, q, kt, v, out_w, name, *, tq=ATTN_TQ, tk=ATTN_TK):
    b_, _, _, s_, _ = q.shape
    wide = 2 * HEAD_DIM
    q_spec = pl.BlockSpec((1, 1, GQA_GROUP, tq, wide), lambda b, h, i: (b, h, 0, i, 0))
    kt_spec = pl.BlockSpec((1, 1, wide, s_), lambda b, h, i: (b, h, 0, 0))
    v_spec = pl.BlockSpec((1, 1, s_, wide), lambda b, h, i: (b, h, 0, 0))
    o_spec = pl.BlockSpec((1, 1, GQA_GROUP, tq, out_w), lambda b, h, i: (b, h, 0, i, 0))
    operands = (q, kt) if v is None else (q, kt, v)
    return pl.pallas_call(
        functools.partial(body, tq=tq, tk=tk, n_kc=s_ // tk),
        out_shape=jax.ShapeDtypeStruct((b_, N_KV_HEADS, GQA_GROUP, s_, out_w), jnp.float32),
        grid=(b_, N_KV_HEADS, s_ // tq),
        in_specs=[q_spec, kt_spec] if v is None else [q_spec, kt_spec, v_spec],
        out_specs=o_spec,
        compiler_params=pltpu.CompilerParams(
            dimension_semantics=("arbitrary", "arbitrary", "arbitrary"),
            vmem_limit_bytes=ATTN_VMEM_LIMIT_BYTES),
        name=name,
    )(*operands)


def global_mixer(d_q, d_kv, q_norm_g, k_norm_g, row_idx, col_idx, norm_g):
    b_, s_, _ = d_q.shape
    q = d_q.reshape(b_, s_, N_HEADS, HEAD_DIM)
    k, v = jnp.split(d_kv, 2, axis=-1)
    k = k.reshape(b_, s_, N_KV_HEADS, HEAD_DIM)
    v = v.reshape(b_, s_, N_KV_HEADS, HEAD_DIM)
    q = axial_rope(rmsnorm(q, q_norm_g), row_idx, col_idx) * (LOG2E * HEAD_DIM ** -0.5)
    k = axial_rope(rmsnorm(k, k_norm_g), row_idx, col_idx)
    bf16 = jnp.bfloat16
    qb = q.astype(bf16).reshape(b_, s_, N_KV_HEADS, GQA_GROUP, HEAD_DIM).transpose(0, 2, 3, 1, 4)
    kb = k.astype(bf16).transpose(0, 2, 1, 3)
    vb = v.astype(bf16).transpose(0, 2, 1, 3)
    qn = jnp.sqrt(jnp.sum(jnp.square(qb.astype(jnp.float32)), axis=-1, keepdims=True))
    kn = jnp.sqrt(jnp.max(jnp.sum(jnp.square(kb.astype(jnp.float32)), axis=-1), axis=-1))
    bound = qn * kn[:, :, None, None, None]
    one = jnp.ones((b_, N_KV_HEADS, s_, 1), bf16)
    pad = jnp.zeros((b_, N_KV_HEADS, s_, HEAD_DIM - 1), bf16)
    kta = jnp.concatenate([kb, one, pad], axis=-1).transpose(0, 1, 3, 2)
    va = jnp.concatenate([vb, one, pad], axis=-1)
    qpad = jnp.zeros(qb.shape[:-1] + (HEAD_DIM - 1,), bf16)

    def with_shift(m):
        return jnp.concatenate([qb, (-m).astype(bf16), qpad], axis=-1)

    m = lax.cond(jnp.max(bound) < ATTN_SAFE_SHIFT, lambda: bound,
                 lambda: _attn_call(_rowmax_kernel, with_shift(jnp.zeros_like(bound)), kta, None, 1,
                                    "global_attention_rowmax"))
    o = _attn_call(_flash_kernel, with_shift(m), kta, va, HEAD_DIM, "global_flash_attention")
    o = o.transpose(0, 3, 1, 2, 4).reshape(b_, s_, GROUP_W)
    return rmsnorm(o, norm_g)


def _row_copy(src, src_row, dst, dst_row, sem):
    return pltpu.make_async_copy(src.at[pl.ds(src_row, 1)], dst.at[pl.ds(dst_row, 1)], sem)


def _moe_dispatch_kernel(pos_ref, x_ref, g_ref, zeros_hbm, xs_hbm, xn_scr, sem, *, tb):
    del zeros_hbm
    i = pl.program_id(0)
    x = x_ref[...]
    xn_scr[...] = x * lax.rsqrt(jnp.mean(x * x, axis=-1, keepdims=True) + EPS) * g_ref[...]

    def issue(r, carry):
        for k in range(TOP_K):
            _row_copy(xn_scr, r, xs_hbm, pos_ref[(i * tb + r) * TOP_K + k], sem).start()
        return carry

    lax.fori_loop(0, tb, issue, 0, unroll=8)
    for _ in range(TOP_K):
        pltpu.make_async_copy(xn_scr, xn_scr, sem).wait()


def _moe_expert_kernel(blk_e_ref, blk_rows_ref, xs_ref, wg_ref, wu_ref, wd_ref, o_ref, *, tm):
    del blk_e_ref
    rows = blk_rows_ref[pl.program_id(0)]

    @pl.when(rows > 0)
    def _():
        x = xs_ref[...].astype(jnp.bfloat16)
        g = jnp.dot(x, wg_ref[0], preferred_element_type=jnp.float32)
        u = jnp.dot(x, wu_ref[0], preferred_element_type=jnp.float32)
        h = (g * jax.nn.sigmoid(g) * u).astype(jnp.bfloat16)
        o_ref[...] = jnp.dot(h, wd_ref[0], preferred_element_type=jnp.float32)

    @pl.when(rows == 0)
    def _():
        o_ref[...] = jnp.zeros((tm, o_ref.shape[1]), jnp.float32)


def _moe_combine_kernel(pos_ref, x_ref, w_ref, os_hbm, y_ref, buf, sem, *, tb):
    i = pl.program_id(0)

    def issue(r, carry):
        for k in range(TOP_K):
            _row_copy(os_hbm, pos_ref[(i * tb + r) * TOP_K + k], buf.at[k], r, sem).start()
        return carry

    lax.fori_loop(0, tb, issue, 0, unroll=8)
    for k in range(TOP_K):
        pltpu.make_async_copy(buf.at[k], buf.at[k], sem).wait()
    w = w_ref[...]
    y = x_ref[...]
    for k in range(TOP_K):
        y = y + w[:, k:k + 1] * buf[k]
    y_ref[...] = y


def _moe_route(xn, w_group, b_group, w_expert, b_expert, tm):
    n_tok = xn.shape[0]
    gp = jax.nn.softmax((xn @ w_group).astype(jnp.float32) + b_group.astype(jnp.float32), axis=-1)
    g_idx = jnp.argmax(gp, axis=-1, keepdims=True)
    g_w = jnp.max(gp, axis=-1, keepdims=True)
    elog = (xn @ w_expert).astype(jnp.float32) + b_expert.astype(jnp.float32)
    elog = elog.reshape(n_tok, N_EXPERT_GROUPS, EXPERTS_PER_GROUP)
    elog_sel = jnp.take_along_axis(elog, g_idx[:, :, None], axis=1)[:, 0]
    e_w, e_idx = lax.top_k(jax.nn.softmax(elog_sel, axis=-1), TOP_K)
    e_w = e_w / jnp.sum(e_w, axis=-1, keepdims=True)
    weights = g_w * e_w
    experts = (g_idx * EXPERTS_PER_GROUP + e_idx).reshape(-1)
    onehot = (experts[:, None] == jnp.arange(N_EXPERTS)[None, :]).astype(jnp.int32)
    csum = jnp.cumsum(onehot, axis=0)
    counts = csum[-1]
    rank = jnp.sum(onehot * csum, axis=1) - 1
    padded = ((counts + tm - 1) // tm) * tm
    pend = jnp.cumsum(padded)
    pstart = pend - padded
    pos = (pstart[experts] + rank).astype(jnp.int32)
    n_blk = (n_tok * TOP_K) // tm + N_EXPERTS
    blk_start = jnp.arange(n_blk) * tm
    blk_e = jnp.minimum(jnp.sum(pend[None, :] <= blk_start[:, None], axis=1), N_EXPERTS - 1).astype(jnp.int32)
    blk_rows = jnp.clip(counts[blk_e] - (blk_start - pstart[blk_e]), 0, tm).astype(jnp.int32)
    return weights, pos, blk_e, blk_rows, n_blk


def hier_moe_residual(x, norm_g, w_group, b_group, w_expert, b_expert, w_gate, w_up, w_down,
                      *, tm=MOE_TM, tb=MOE_TB):
    b_, s_, d_ = x.shape
    n_tok = b_ * s_
    xt = x.reshape(n_tok, d_)
    weights, pos, blk_e, blk_rows, n_blk = _moe_route(rmsnorm(xt, norm_g), w_group, b_group,
                                                      w_expert, b_expert, tm)
    n_pad = n_blk * tm
    vmem = pltpu.CompilerParams(dimension_semantics=("arbitrary",),
                                vmem_limit_bytes=MOE_VMEM_LIMIT_BYTES)
    xs = pl.pallas_call(
        functools.partial(_moe_dispatch_kernel, tb=tb),
        grid_spec=pltpu.PrefetchScalarGridSpec(
            num_scalar_prefetch=1, grid=(n_tok // tb,),
            in_specs=[pl.BlockSpec((tb, d_), lambda i, pos: (i, 0)),
                      pl.BlockSpec((1, d_), lambda i, pos: (0, 0)),
                      pl.BlockSpec(memory_space=pl.ANY)],
            out_specs=pl.BlockSpec(memory_space=pl.ANY),
            scratch_shapes=[pltpu.VMEM((tb, d_), jnp.float32), pltpu.SemaphoreType.DMA(())]),
        out_shape=jax.ShapeDtypeStruct((n_pad, d_), jnp.float32),
        input_output_aliases={3: 0},
        compiler_params=vmem, name="moe_dispatch",
    )(pos, xt, norm_g.astype(jnp.float32).reshape(1, d_), jnp.zeros((n_pad, d_), jnp.float32))
    bf16 = jnp.bfloat16
    outs = pl.pallas_call(
        functools.partial(_moe_expert_kernel, tm=tm),
        grid_spec=pltpu.PrefetchScalarGridSpec(
            num_scalar_prefetch=2, grid=(n_blk,),
            in_specs=[pl.BlockSpec((tm, d_), lambda i, be, br: (i, 0)),
                      pl.BlockSpec((1, d_, D_EXPERT), lambda i, be, br: (be[i], 0, 0)),
                      pl.BlockSpec((1, d_, D_EXPERT), lambda i, be, br: (be[i], 0, 0)),
                      pl.BlockSpec((1, D_EXPERT, d_), lambda i, be, br: (be[i], 0, 0))],
            out_specs=pl.BlockSpec((tm, d_), lambda i, be, br: (i, 0))),
        out_shape=jax.ShapeDtypeStruct((n_pad, d_), jnp.float32),
        compiler_params=vmem, name="moe_experts",
    )(blk_e, blk_rows, xs, w_gate.astype(bf16), w_up.astype(bf16), w_down.astype(bf16))
    y = pl.pallas_call(
        functools.partial(_moe_combine_kernel, tb=tb),
        grid_spec=pltpu.PrefetchScalarGridSpec(
            num_scalar_prefetch=1, grid=(n_tok // tb,),
            in_specs=[pl.BlockSpec((tb, d_), lambda i, pos: (i, 0)),
                      pl.BlockSpec((tb, TOP_K), lambda i, pos: (i, 0)),
                      pl.BlockSpec(memory_space=pl.ANY)],
            out_specs=pl.BlockSpec((tb, d_), lambda i, pos: (i, 0)),
            scratch_shapes=[pltpu.VMEM((TOP_K, tb, d_), jnp.float32), pltpu.SemaphoreType.DMA(())]),
        out_shape=jax.ShapeDtypeStruct((n_tok, d_), jnp.float32),
        compiler_params=vmem, name="moe_combine",
    )(pos, xt, weights, outs)
    return y.reshape(b_, s_, d_)


PROJ_DEST = {"a_qkv": (0, 0), "a_z": (1, 1536), "b_in": (4, 2048), "c_q": (5, 3584), "d_q": (7, 4096),
             "c_kv": (6, 4608), "d_kv": (8, 4864), "a_beta": (2, 5120), "a_alpha": (3, 5136)}
PROJ_WIDTH = 5632


def _in_proj_weight(w):
    starts = [0] + _split_points()
    cols = jnp.zeros((w.shape[0], PROJ_WIDTH), jnp.bfloat16)
    for seg, dest in PROJ_DEST.values():
        cols = lax.dynamic_update_slice(
            cols, w[:, starts[seg]:starts[seg] + IN_SPLIT_SIZES[seg]].astype(jnp.bfloat16), (0, dest))
    return cols


def _in_proj_kernel(x_ref, g_ref, w_ref, o_ref, xn_scr):
    @pl.when(pl.program_id(1) == 0)
    def _():
        x = x_ref[...]
        xn = x * lax.rsqrt(jnp.mean(x * x, axis=-1, keepdims=True) + EPS) * g_ref[...]
        xn_scr[...] = xn.astype(jnp.bfloat16)

    o_ref[...] = jnp.dot(xn_scr[...], w_ref[...], preferred_element_type=jnp.float32)


def in_proj(x, norm_g, w):
    n_tok, d_ = x.shape
    return pl.pallas_call(
        _in_proj_kernel,
        grid=(n_tok // PROJ_TM, PROJ_WIDTH // PROJ_TILE),
        in_specs=[pl.BlockSpec((PROJ_TM, d_), lambda i, j: (i, 0)),
                  pl.BlockSpec((1, d_), lambda i, j: (0, 0)),
                  pl.BlockSpec((d_, PROJ_TILE), lambda i, j: (0, j))],
        out_specs=pl.BlockSpec((PROJ_TM, PROJ_TILE), lambda i, j: (i, j)),
        out_shape=jax.ShapeDtypeStruct((n_tok, PROJ_WIDTH), jnp.float32),
        scratch_shapes=[pltpu.VMEM((PROJ_TM, d_), jnp.bfloat16)],
        compiler_params=pltpu.CompilerParams(dimension_semantics=("arbitrary", "arbitrary"),
                                             vmem_limit_bytes=PROJ_VMEM_LIMIT_BYTES),
        name="in_proj",
    )(x, norm_g.astype(jnp.float32).reshape(1, d_), _in_proj_weight(w))


def _out_proj_kernel(x_ref, *refs):
    *y_refs, w_ref, o_ref = refs
    acc = x_ref[...]
    for k, y_ref in enumerate(y_refs):
        acc = acc + jnp.dot(y_ref[...].astype(jnp.bfloat16), w_ref[k * GROUP_W:(k + 1) * GROUP_W, :],
                            preferred_element_type=jnp.float32)
    o_ref[...] = acc


def out_proj_residual(x, ys, w):
    n_tok, d_ = x.shape
    row = lambda width: pl.BlockSpec((OUT_TM, width), lambda i: (i, 0))
    return pl.pallas_call(
        _out_proj_kernel,
        grid=(n_tok // OUT_TM,),
        in_specs=[row(d_)] + [row(GROUP_W)] * len(ys) + [pl.BlockSpec(w.shape, lambda i: (0, 0))],
        out_specs=row(d_),
        out_shape=jax.ShapeDtypeStruct((n_tok, d_), jnp.float32),
        compiler_params=pltpu.CompilerParams(dimension_semantics=("arbitrary",),
                                             vmem_limit_bytes=PROJ_VMEM_LIMIT_BYTES),
        name="out_proj",
    )(x, *ys, w.astype(jnp.bfloat16))


def kernel(x, norm_mix, w_in, gdn_conv, gdn_a_log, gdn_dt_bias, gdn_norm, hy_conv, hy_w1, hy_b1, hy_freq1, hy_w2, hy_b2, hy_freq2, hy_w3, hy_deltas, hy_bias, hy_norm, swa_sink, swa_norm, ga_q_norm, ga_k_norm, ga_norm, w_out, norm_ffn, moe_w_group, moe_b_group, moe_w_expert, moe_b_expert, moe_w_gate, moe_w_up, moe_w_down, norm_final):
    b_, s_, _ = x.shape
    rows = s_ // GRID_W
    row_idx = jnp.repeat(jnp.arange(rows), GRID_W)
    col_idx = jnp.tile(jnp.arange(GRID_W), rows)
    pos_feat = hyena_pos_features(s_)
    slopes = alibi_slopes(N_HEADS)
    n_tok = b_ * s_
    for l in range(DEPTH):
        proj = in_proj(x.reshape(n_tok, D_MODEL), norm_mix[l], w_in[l]).reshape(b_, s_, PROJ_WIDTH)
        seg = lambda name: proj[:, :, PROJ_DEST[name][1]:PROJ_DEST[name][1] + IN_SPLIT_SIZES[PROJ_DEST[name][0]]]
        y_a = gdn_mixer(seg("a_qkv"), seg("a_z"), seg("a_beta"), seg("a_alpha"), gdn_conv[l], gdn_a_log[l],
                        gdn_dt_bias[l], gdn_norm[l])
        kf = hyena_filters_f(pos_feat, hy_w1[l], hy_b1[l], hy_freq1[l], hy_w2[l], hy_b2[l],
                             hy_freq2[l], hy_w3[l], hy_deltas[l])
        y_b = hyena_mixer(seg("b_in"), hy_conv[l], kf, hy_bias[l], hy_norm[l])
        y_c = window_mixer(proj, PROJ_DEST["c_q"][1], PROJ_DEST["c_kv"][1], swa_sink[l], slopes, swa_norm[l])
        y_d = global_mixer(seg("d_q"), seg("d_kv"), ga_q_norm[l], ga_k_norm[l], row_idx, col_idx, ga_norm[l])
        x = out_proj_residual(x.reshape(n_tok, D_MODEL),
                              [y.reshape(n_tok, GROUP_W) for y in (y_a, y_b, y_c, y_d)],
                              w_out[l]).reshape(b_, s_, D_MODEL)
        x = hier_moe_residual(x, norm_ffn[l], moe_w_group[l], moe_b_group[l], moe_w_expert[l],
                              moe_b_expert[l], moe_w_gate[l], moe_w_up[l], moe_w_down[l])
    return rmsnorm(x, norm_final)
```

```python
import functools
import math

import jax
import jax.numpy as jnp
import numpy as np
from jax import lax
from jax.experimental import pallas as pl
from jax.experimental.pallas import tpu as pltpu

D_MODEL = 2048
DEPTH = 2
N_MIXERS = 4
GROUP_W = D_MODEL // N_MIXERS
HEAD_DIM = 64
N_HEADS = GROUP_W // HEAD_DIM
N_KV_HEADS = 2
GQA_GROUP = N_HEADS // N_KV_HEADS
KV_W = N_KV_HEADS * HEAD_DIM
SHORT_CONV = 3
GDN_CHUNK = 64
HY_ORDER = 2
HY_EMB = 33
HY_BANDS = (HY_EMB - 1) // 2
WINDOW = 128
BLOCK = 128
GRID_W = 64
ROPE_THETA = 10000.0
N_EXPERT_GROUPS = 4
EXPERTS_PER_GROUP = 8
N_EXPERTS = N_EXPERT_GROUPS * EXPERTS_PER_GROUP
TOP_K = 2
D_EXPERT = 512
MOE_BLOCK = 128
EPS = 1e-6
IN_SPLIT_SIZES = (3 * GROUP_W, GROUP_W, 2 * N_HEADS, 2 * N_HEADS, 3 * GROUP_W,
                  GROUP_W, 2 * KV_W, GROUP_W, 2 * KV_W)

ATTN_VMEM_LIMIT_BYTES = 48 * 1024 * 1024
ATTN_TQ = 256
ATTN_TK = 2048
WINDOW_MASK = -1e30
GDN_VMEM_LIMIT_BYTES = 48 * 1024 * 1024
PROJ_VMEM_LIMIT_BYTES = 48 * 1024 * 1024
CONV_TM = 512
CONV_HALO = 8
PROJ_TM = 1024
PROJ_TILE = 512
OUT_TM = 256
GDN_SB = 256
GDN_INV_BASE = 8
FFT_VMEM_LIMIT_BYTES = 48 * 1024 * 1024
FFT_N2 = 256
FFT_COL_TILE = 4096
MOE_VMEM_LIMIT_BYTES = 48 * 1024 * 1024
MOE_TM = 512
MOE_TB = 256
ROUTER_W = 128
LOG2E = 1.4426950408889634
ATTN_SAFE_SHIFT = 50.0
ATTN_ROUND_UP = 1.01


def _split_points():
    return [int(v) for v in np.cumsum(IN_SPLIT_SIZES)[:-1]]


def _group_ones(width):
    r = np.arange(width) // HEAD_DIM
    return jnp.asarray(r[:, None] == r[None, :], jnp.bfloat16)


def _group_sum(x, bd):
    hi = x.astype(jnp.bfloat16)
    lo = (x - hi.astype(jnp.float32)).astype(jnp.bfloat16)
    return (jnp.dot(hi, bd, preferred_element_type=jnp.float32)
            + jnp.dot(lo, bd, preferred_element_type=jnp.float32))


def _conv_prep_kernel(*refs, modes, n_steps):
    n = len(modes)
    w_ref, bd_ref = refs[3 * n], refs[3 * n + 1]
    o_refs = refs[3 * n + 2:]
    i = pl.program_id(1)
    tm = refs[0].shape[1]
    row = lax.broadcasted_iota(jnp.int32, (tm, 1), 0)
    for t, mode in enumerate(modes):
        x_ref, prev_ref, next_ref = refs[3 * t:3 * t + 3]
        x = x_ref[0]
        prev = jnp.where(i > 0, prev_ref[0][CONV_HALO - 1:CONV_HALO], 0.0)
        nxt = jnp.where(i < n_steps - 1, next_ref[0][0:1], 0.0)
        x_prev = jnp.where(row == 0, prev, pltpu.roll(x, 1, 0))
        x_next = jnp.where(row == tm - 1, nxt, pltpu.roll(x, tm - 1, 0))
        w = w_ref[:, t * GROUP_W:(t + 1) * GROUP_W]
        y = x_prev * w[0:1] + x * w[1:2] + x_next * w[2:3]
        if mode != "plain":
            y = y * jax.nn.sigmoid(y)
        if mode.startswith("silu_l2"):
            y = y * lax.rsqrt(_group_sum(y * y, bd_ref[...]) + EPS)
        if mode == "silu_l2_scaled":
            y = y * (HEAD_DIM ** -0.5)
        o_refs[t][0] = y


def _conv_prep(proj, col, conv_w, modes):
    b_, s_, _ = proj.shape
    assert col % GROUP_W == 0
    tm = CONV_TM
    n_steps = s_ // tm
    hb = tm // CONV_HALO
    specs = []
    for t in range(len(modes)):
        c = col // GROUP_W + t
        specs += [pl.BlockSpec((1, tm, GROUP_W), lambda b, i, c=c: (b, i, c)),
                  pl.BlockSpec((1, CONV_HALO, GROUP_W), lambda b, i, c=c: (b, jnp.maximum(i * hb - 1, 0), c)),
                  pl.BlockSpec((1, CONV_HALO, GROUP_W),
                               lambda b, i, c=c: (b, jnp.minimum((i + 1) * hb, s_ // CONV_HALO - 1), c))]
    bd = _group_ones(GROUP_W)
    w = conv_w.astype(jnp.float32)
    specs += [pl.BlockSpec(w.shape, lambda b, i: (0, 0)), pl.BlockSpec(bd.shape, lambda b, i: (0, 0))]
    out = jax.ShapeDtypeStruct((b_, s_, GROUP_W), jnp.float32)
    return pl.pallas_call(
        functools.partial(_conv_prep_kernel, modes=modes, n_steps=n_steps),
        grid=(b_, n_steps), in_specs=specs,
        out_specs=[pl.BlockSpec((1, tm, GROUP_W), lambda b, i: (b, i, 0))] * len(modes),
        out_shape=[out] * len(modes),
        compiler_params=pltpu.CompilerParams(dimension_semantics=("arbitrary", "arbitrary")),
        name="conv_prep",
    )(*([proj] * (3 * len(modes))), w, bd)


def _gdn_post_kernel(of_ref, ob_ref, z_ref, g_ref, bd_ref, y_ref):
    o = of_ref[0] + ob_ref[0]
    z = z_ref[0]
    ms = _group_sum(o * o, bd_ref[...]) * (1.0 / HEAD_DIM)
    y_ref[0] = o * lax.rsqrt(ms + EPS) * g_ref[...] * (z * jax.nn.sigmoid(z))


def _gdn_post(o_fwd, o_bwd, proj, z_col, norm_g):
    b_, s_, w_ = o_fwd.shape
    tm = CONV_TM
    tok = pl.BlockSpec((1, tm, w_), lambda b, i: (b, i, 0))
    bd = _group_ones(w_)
    g = jnp.tile(norm_g.astype(jnp.float32), N_HEADS).reshape(1, w_)
    return pl.pallas_call(
        _gdn_post_kernel, grid=(b_, s_ // tm),
        in_specs=[tok, tok, pl.BlockSpec((1, tm, w_), lambda b, i: (b, i, z_col // w_)),
                  pl.BlockSpec(g.shape, lambda b, i: (0, 0)), pl.BlockSpec(bd.shape, lambda b, i: (0, 0))],
        out_specs=tok, out_shape=jax.ShapeDtypeStruct((b_, s_, w_), jnp.float32),
        compiler_params=pltpu.CompilerParams(dimension_semantics=("arbitrary", "arbitrary")),
        name="gdn_post",
    )(o_fwd, o_bwd, proj, g, bd)


def _gdn_masks(rev):
    r = np.arange(GDN_SB)
    i, j = r[:, None], r[None, :]
    same = lambda s: (i // s) == (j // s)
    before = (i < j) if rev else (i > j)
    chunk = same(GDN_CHUNK)
    masks = [chunk & (before | (i == j)),
             chunk,
             chunk & before,
             i == j,
             same(GDN_INV_BASE)]
    s = GDN_INV_BASE
    while s < GDN_CHUNK:
        masks.append(same(2 * s) & ~same(s))
        s *= 2
    return np.stack(masks).astype(np.float32)


def _mm_bf16(a, b):
    return jnp.dot(a.astype(jnp.bfloat16), b.astype(jnp.bfloat16), preferred_element_type=jnp.float32)


def _unit_triangular_inverses(a_list, eye, m_base, m_offs):
    n_list = [-a * m_base for a in a_list]
    t_list = [eye + n for n in n_list]
    power = 2
    while power < GDN_INV_BASE:
        n_list = [_mm_bf16(n, n) for n in n_list]
        t_list = [t + _mm_bf16(t, n) for t, n in zip(t_list, n_list)]
        power *= 2
    for m_off in m_offs:
        u_list = [_mm_bf16(a * m_off, t) for a, t in zip(a_list, t_list)]
        t_list = [t - _mm_bf16(t, u) for t, u in zip(t_list, u_list)]
    return t_list


def _gdn_kernel(mask_ref, q_ref, k_ref, v_ref, beta_ref, g_ref, gt_ref, o_ref,
                state, vnew, val_s, kcd_s, qd_s, kd_s, attn_s, *, rev):
    bf16 = jnp.bfloat16
    hd = HEAD_DIM
    nt = (((1,), (1,)), ((), ()))
    tn = (((0,), (0,)), ((), ()))

    @pl.when(pl.program_id(1) == 0)
    def _():
        state[...] = jnp.zeros(state.shape, state.dtype)
        vnew[...] = jnp.zeros(vnew.shape, vnew.dtype)

    incl, ones, strict, eye, m_base = (mask_ref[t] for t in range(5))
    m_offs = [mask_ref[t] for t in range(5, mask_ref.shape[0])]
    g = g_ref[0, 0]
    beta = beta_ref[0, 0]
    gc = _dot_f32(incl, g)
    gl = _dot_f32(ones, g)
    gct = lax.dot_general(gt_ref[0, 0], incl, nt, precision=lax.Precision.HIGHEST,
                          preferred_element_type=jnp.float32)
    eg = jnp.exp(gc)
    ekd = jnp.exp(gl - gc)
    cd = jnp.exp(gl)
    a_list, rhs_list = [], []
    for h in range(N_HEADS):
        sl = slice(h * hd, (h + 1) * hd)
        q, k, v = q_ref[0][:, sl], k_ref[0][:, sl], v_ref[0][:, sl]
        bh = beta[:, h:h + 1]
        kb = k * bh
        kbf = k.astype(bf16)
        dec = jnp.exp(jnp.minimum(gc[:, h:h + 1] - gct[h:h + 1, :], 0.0)) * incl
        a_list.append(lax.dot_general(kb.astype(bf16), kbf, nt, preferred_element_type=jnp.float32)
                      * dec * strict)
        attn = lax.dot_general(q.astype(bf16), kbf, nt, preferred_element_type=jnp.float32) * dec
        attn_s[h] = attn.astype(bf16)
        rhs_list.append(jnp.concatenate([v * bh, kb * eg[:, h:h + 1]], axis=1))
        qd_s[h] = (q * eg[:, h:h + 1]).astype(bf16)
        kd_s[h] = (k * ekd[:, h:h + 1]).astype(bf16)
    t_list = _unit_triangular_inverses(a_list, eye, m_base, m_offs)
    for h in range(N_HEADS):
        sol = _mm_bf16(t_list[h], rhs_list[h])
        val_s[h] = sol[:, :hd]
        kcd_s[h] = sol[:, hd:].astype(bf16)
    n_chunks = GDN_SB // GDN_CHUNK
    for c in (reversed(range(n_chunks)) if rev else range(n_chunks)):
        rows = slice(c * GDN_CHUNK, (c + 1) * GDN_CHUNK)
        for h in range(N_HEADS):
            s_old = state[h]
            s_bf = s_old.astype(bf16)
            v_new = val_s[h, rows, :] - jnp.dot(kcd_s[h, rows, :], s_bf, preferred_element_type=jnp.float32)
            v_bf = v_new.astype(bf16)
            vnew[h, rows, :] = v_bf
            o = (jnp.dot(qd_s[h, rows, :], s_bf, preferred_element_type=jnp.float32)
                 + jnp.dot(attn_s[h, rows, :], vnew[h], preferred_element_type=jnp.float32))
            state[h] = (s_old * cd[c * GDN_CHUNK:c * GDN_CHUNK + 1, h:h + 1]
                        + lax.dot_general(kd_s[h, rows, :], v_bf, tn, preferred_element_type=jnp.float32))
            o_ref[0, rows, h * hd:(h + 1) * hd] = o


def _gdn_scan(q, k, v, beta, g, rev):
    b_, s_, w_ = q.shape
    n_sb = s_ // GDN_SB
    masks = jnp.asarray(_gdn_masks(rev))
    step = (lambda i: n_sb - 1 - i) if rev else (lambda i: i)
    tok = pl.BlockSpec((1, GDN_SB, w_), lambda b, i: (b, step(i), 0))
    gate = pl.BlockSpec((1, 1, GDN_SB, N_HEADS), lambda b, i: (b, 0, step(i), 0))
    gate_t = pl.BlockSpec((1, 1, N_HEADS, GDN_SB), lambda b, i: (b, 0, 0, step(i)))
    per_head = lambda width, dt: pltpu.VMEM((N_HEADS, GDN_SB, width), dt)
    return pl.pallas_call(
        functools.partial(_gdn_kernel, rev=rev),
        grid=(b_, n_sb),
        in_specs=[pl.BlockSpec(masks.shape, lambda b, i: (0, 0, 0)), tok, tok, tok, gate, gate, gate_t],
        out_specs=tok,
        out_shape=jax.ShapeDtypeStruct((b_, s_, w_), jnp.float32),
        scratch_shapes=[pltpu.VMEM((N_HEADS, HEAD_DIM, HEAD_DIM), jnp.float32),
                        per_head(HEAD_DIM, jnp.bfloat16), per_head(HEAD_DIM, jnp.float32),
                        per_head(HEAD_DIM, jnp.bfloat16), per_head(HEAD_DIM, jnp.bfloat16),
                        per_head(HEAD_DIM, jnp.bfloat16), per_head(GDN_SB, jnp.bfloat16)],
        compiler_params=pltpu.CompilerParams(dimension_semantics=("arbitrary", "arbitrary"),
                                             vmem_limit_bytes=GDN_VMEM_LIMIT_BYTES),
        name="gdn_scan_rev" if rev else "gdn_scan_fwd",
    )(masks, q, k, v, beta, g, g.transpose(0, 1, 3, 2))


def gdn_mixer(proj, conv_w, a_log, dt_bias, norm_g):
    b_, s_, _ = proj.shape
    q, k, v = _conv_prep(proj, PROJ_DEST["a_qkv"][1], conv_w, ("silu_l2_scaled", "silu_l2", "silu"))
    gates = proj[:, :, PROJ_DEST["a_beta"][1]:PROJ_DEST["a_beta"][1] + 4 * N_HEADS]
    b_in = gates[:, :, :2 * N_HEADS].reshape(b_, s_, 2, N_HEADS)
    a_in = gates[:, :, 2 * N_HEADS:].reshape(b_, s_, 2, N_HEADS)
    beta = jax.nn.sigmoid(b_in).transpose(0, 2, 1, 3)
    g = -jnp.exp(a_log.astype(jnp.float32)) * jax.nn.softplus(a_in + dt_bias.astype(jnp.float32))
    g = g.transpose(0, 2, 1, 3)
    o_fwd = _gdn_scan(q, k, v, beta[:, 0:1], g[:, 0:1], False)
    o_bwd = _gdn_scan(q, k, v, beta[:, 1:2], g[:, 1:2], True)
    return _gdn_post(o_fwd, o_bwd, proj, PROJ_DEST["a_z"][1], norm_g)


def hyena_pos_features(length):
    t = jnp.linspace(0.0, 1.0, length, dtype=jnp.float32)[:, None]
    w = 2.0 * math.pi * jnp.arange(length, dtype=jnp.float32) / length
    f = jnp.linspace(1e-4, HY_BANDS - 1, HY_BANDS, dtype=jnp.float32)
    fw = w[:, None] * f[None, :]
    return jnp.concatenate([t, jnp.cos(fw), -jnp.sin(fw)], axis=-1)


def hyena_filters_f(z, w1, b1, f1, w2, b2, f2, w3, deltas):
    f32 = jnp.float32
    length = z.shape[0]
    t = z[:, :1]
    h = jnp.sin(f1.astype(f32) * (z @ w1.astype(f32) + b1.astype(f32)))
    h = jnp.sin(f2.astype(f32) * (h @ w2.astype(f32) + b2.astype(f32)))
    h = (h @ w3.astype(f32)) * jnp.exp(-t * jnp.abs(deltas.astype(f32)))
    h = h.reshape(length, HY_ORDER, 2, GROUP_W)
    kern = jnp.concatenate([h[:, :, 0], jnp.zeros((1, HY_ORDER, GROUP_W), f32),
                            h[:0:-1, :, 1]], axis=0)
    return kern / (jnp.sum(jnp.abs(kern), axis=0, keepdims=True) + EPS)


def _dft_tables(n_fft):
    n1 = n_fft // FFT_N2
    def dft(n):
        kk = (np.arange(n)[:, None] * np.arange(n)[None, :]) % n
        ang = -2.0 * np.pi * kk / n
        return np.cos(ang), np.sin(ang)
    f1r, f1i = dft(n1)
    f2r, f2i = dft(FFT_N2)
    kk = (np.arange(n1)[:, None] * np.arange(FFT_N2)[None, :]) % n_fft
    tw = -2.0 * np.pi * kk / n_fft
    f32 = np.float32
    return dict(
        f1=np.concatenate([f1r, f1i], axis=0).astype(f32),
        f1_inv=(np.concatenate([f1r[:n1 // 2], f1i[:n1 // 2]], axis=0) / n_fft).astype(f32),
        f2=np.concatenate([f2r, f2i], axis=0).astype(f32),
        twr=np.cos(tw).astype(f32)[:, :, None], twi=np.sin(tw).astype(f32)[:, :, None])


def _dot_f32(a, b):
    return jnp.dot(a, b, precision=lax.Precision.HIGHEST, preferred_element_type=jnp.float32)


def _split_lhs(f):
    f = jnp.asarray(f, jnp.float32)
    hi = f.astype(jnp.bfloat16)
    lo = (f - hi.astype(jnp.float32)).astype(jnp.bfloat16)
    return jnp.concatenate([hi, hi, lo], axis=1)


def _dot_split(f3, x):
    hi = x.astype(jnp.bfloat16)
    lo = (x - hi.astype(jnp.float32)).astype(jnp.bfloat16)
    return jnp.dot(f3, jnp.concatenate([hi, lo, hi], axis=0), preferred_element_type=jnp.float32)


def _fft_stage1_kernel(f_ref, zr_ref, zi_ref, yr_ref, yi_ref, *, n1):
    f = f_ref[...]
    p = _dot_split(f, zr_ref[0])
    if zi_ref is None:
        yr_ref[...] = p[:n1]
        yi_ref[...] = p[n1:]
    else:
        q = _dot_split(f, zi_ref[0])
        yr_ref[...] = p[:n1] - q[n1:]
        yi_ref[...] = q[:n1] + p[n1:]


def _fft_stage1_real_kernel(f_ref, zr_ref, yr_ref, yi_ref, *, n1):
    _fft_stage1_kernel(f_ref, zr_ref, None, yr_ref, yi_ref, n1=n1)


def _fft_mid_kernel(f_ref, twr_ref, twi_ref, yr_ref, yi_ref, kr_ref, ki_ref, qr_ref, qi_ref):
    n2 = FFT_N2
    twr, twi = twr_ref[0], twi_ref[0]
    yr, yi = yr_ref[0], yi_ref[0]
    f = f_ref[...]
    p = _dot_split(f, yr * twr - yi * twi)
    q = _dot_split(f, yr * twi + yi * twr)
    xr = p[:n2] - q[n2:]
    xi = q[:n2] + p[n2:]
    if kr_ref is None:
        qr_ref[0] = xr
        qi_ref[0] = xi
        return
    kr, ki = kr_ref[0], ki_ref[0]
    p = _dot_split(f, xr * kr - xi * ki)
    q = _dot_split(f, xr * ki + xi * kr)
    wr = p[:n2] + q[n2:]
    wi = q[:n2] - p[n2:]
    qr_ref[0] = wr * twr + wi * twi
    qi_ref[0] = wi * twr - wr * twi


def _fft_mid_spectrum_kernel(f_ref, twr_ref, twi_ref, yr_ref, yi_ref, qr_ref, qi_ref):
    _fft_mid_kernel(f_ref, twr_ref, twi_ref, yr_ref, yi_ref, None, None, qr_ref, qi_ref)


def _fft_last_kernel(f_ref, qr_ref, qi_ref, u_ref, gate_ref, bias_ref, o_ref, *, nh):
    f = f_ref[...]
    p = _dot_split(f, qr_ref[...])
    q = _dot_split(f, qi_ref[...])
    bias = bias_ref[...]
    o_ref[0] = gate_ref[0] * (p[:nh] + q[nh:] + u_ref[0] * bias)
    o_ref[1] = gate_ref[1] * (q[:nh] - p[nh:] + u_ref[1] * bias)


def _fft_params(n_axes):
    return pltpu.CompilerParams(dimension_semantics=("arbitrary",) * n_axes,
                                vmem_limit_bytes=FFT_VMEM_LIMIT_BYTES)


def _fft_forward(tab, z, n_ch):
    n1 = tab["f1"].shape[1]
    parts, rows, cols = z.shape
    tn = min(FFT_COL_TILE, cols)
    f1 = _split_lhs(tab["f1"][:, :rows])
    y_shape = jax.ShapeDtypeStruct((n1, cols), jnp.float32)
    col_spec = pl.BlockSpec((n1, tn), lambda j: (0, j))
    z_specs = [pl.BlockSpec((1, rows, tn), lambda j, p=p: (p, 0, j)) for p in range(parts)]
    body = _fft_stage1_kernel if parts == 2 else _fft_stage1_real_kernel
    yr, yi = pl.pallas_call(
        functools.partial(body, n1=n1), grid=(cols // tn,),
        in_specs=[pl.BlockSpec(f1.shape, lambda j: (0, 0))] + z_specs,
        out_specs=[col_spec, col_spec], out_shape=[y_shape, y_shape],
        compiler_params=_fft_params(1), name="fft_stage1",
    )(f1, *([z] * parts))
    return yr.reshape(n1, FFT_N2, n_ch), yi.reshape(n1, FFT_N2, n_ch)


def _fft_mid(tab, yr, yi, kr=None, ki=None):
    n1, n2, n_ch = yr.shape
    slab = pl.BlockSpec((1, n2, n_ch), lambda i: (i, 0, 0))
    tw_spec = pl.BlockSpec((1, n2, 1), lambda i: (i, 0, 0))
    f2 = _split_lhs(tab["f2"])
    ops = [f2, jnp.asarray(tab["twr"]), jnp.asarray(tab["twi"]), yr, yi]
    specs = [pl.BlockSpec(f2.shape, lambda i: (0, 0)), tw_spec, tw_spec, slab, slab]
    body = _fft_mid_spectrum_kernel
    if kr is not None:
        ops += [kr, ki]
        specs += [slab, slab]
        body = _fft_mid_kernel
    shape = jax.ShapeDtypeStruct((n1, n2, n_ch), jnp.float32)
    return pl.pallas_call(
        body, grid=(n1,), in_specs=specs, out_specs=[slab, slab], out_shape=[shape, shape],
        compiler_params=_fft_params(1), name="fft_mid",
    )(*ops)


def _fft_conv_gate(tab, u, gate, bias, kr, ki):
    b_, length, n_ch = u.shape
    assert b_ == 2
    n1 = tab["f1"].shape[1]
    nh = n1 // 2
    cols = FFT_N2 * n_ch
    uv = u.reshape(b_, nh, cols)
    yr, yi = _fft_forward(tab, uv, n_ch)
    qr, qi = _fft_mid(tab, yr, yi, kr, ki)
    tn = min(FFT_COL_TILE, cols)
    f1_inv = _split_lhs(tab["f1_inv"])
    q_spec = pl.BlockSpec((n1, tn), lambda j: (0, j))
    u_spec = pl.BlockSpec((b_, nh, tn), lambda j: (0, 0, j))
    out = pl.pallas_call(
        functools.partial(_fft_last_kernel, nh=nh), grid=(cols // tn,),
        in_specs=[pl.BlockSpec(f1_inv.shape, lambda j: (0, 0)), q_spec, q_spec, u_spec, u_spec,
                  pl.BlockSpec((1, tn), lambda j: (0, 0))],
        out_specs=u_spec, out_shape=jax.ShapeDtypeStruct((b_, nh, cols), jnp.float32),
        compiler_params=_fft_params(1), name="fft_last",
    )(f1_inv, qr.reshape(n1, cols), qi.reshape(n1, cols), uv, gate.reshape(b_, nh, cols),
      jnp.tile(bias.astype(jnp.float32), tn // n_ch).reshape(1, tn))
    return out.reshape(b_, length, n_ch)


def hyena_mixer(proj, conv_w, kern, bias):
    dtype = proj.dtype
    length = proj.shape[1]
    x1, x2, v = _conv_prep(proj, PROJ_DEST["b_in"][1], conv_w, ("plain",) * 3)
    tab = _dft_tables(2 * length)
    n1 = tab["f1"].shape[1]
    n_filt = HY_ORDER * GROUP_W
    kr, ki = _fft_mid(tab, *_fft_forward(tab, kern.reshape(1, n1, FFT_N2 * n_filt), n_filt))
    y = _fft_conv_gate(tab, v, x1, bias[0], kr[:, :, :GROUP_W], ki[:, :, :GROUP_W])
    y = _fft_conv_gate(tab, y, x2, bias[1], kr[:, :, GROUP_W:], ki[:, :, GROUP_W:])
    return y.astype(dtype)


def alibi_slopes(n):
    return 2.0 ** (-8.0 * jnp.arange(1, n + 1, dtype=jnp.float32) / n)


def _window_kernel(q_ref, kp_ref, kc_ref, kn_ref, bias_ref, sink_ref, g_ref, o_ref, *, nb):
    bf16 = jnp.bfloat16
    n = pl.program_id(1)
    nt = (((1,), (1,)), ((), ()))
    col = lax.broadcasted_iota(jnp.int32, (1, 3 * BLOCK), 1)
    edge = jnp.where(((n == 0) & (col < BLOCK)) | ((n == nb - 1) & (col >= 2 * BLOCK)), WINDOW_MASK, 0.0)
    q = q_ref[0] * (HEAD_DIM ** -0.5)
    kv = jnp.concatenate([kp_ref[0], kc_ref[0], kn_ref[0]], axis=0)
    outs = []
    for j in range(N_KV_HEADS):
        q4 = jnp.concatenate([q[:, (j * GQA_GROUP + g) * HEAD_DIM:(j * GQA_GROUP + g + 1) * HEAD_DIM]
                              for g in range(GQA_GROUP)], axis=0)
        k = kv[:, j * HEAD_DIM:(j + 1) * HEAD_DIM]
        v = kv[:, KV_W + j * HEAD_DIM:KV_W + (j + 1) * HEAD_DIM]
        s = lax.dot_general(q4.astype(bf16), k.astype(bf16), nt, preferred_element_type=jnp.float32)
        s = s + bias_ref[j] + edge
        sink = sink_ref[j]
        m = jnp.maximum(jnp.max(s, axis=-1, keepdims=True), sink)
        p = jnp.exp(s - m)
        denom = jnp.sum(p, axis=-1, keepdims=True) + jnp.exp(sink - m)
        o = jnp.dot(p.astype(bf16), v.astype(bf16), preferred_element_type=jnp.float32) / denom
        outs += [o[g * BLOCK:(g + 1) * BLOCK] for g in range(GQA_GROUP)]
    o = jnp.concatenate(outs, axis=-1)
    o_ref[0] = o * lax.rsqrt(jnp.mean(o * o, axis=-1, keepdims=True) + EPS) * g_ref[...]


def window_mixer(proj, q_col, kv_col, sink, slopes, norm_g):
    b_, s_, _ = proj.shape
    assert q_col % GROUP_W == 0 and kv_col % (2 * KV_W) == 0
    nb = s_ // BLOCK
    rel = BLOCK + jnp.arange(BLOCK)[:, None] - jnp.arange(3 * BLOCK)[None, :]
    dist = jnp.abs(rel).astype(jnp.float32)
    bias = jnp.where(jnp.abs(rel) <= WINDOW, -slopes[:, None, None] * dist, WINDOW_MASK)
    bias = bias.reshape(N_KV_HEADS, GQA_GROUP * BLOCK, 3 * BLOCK)
    sink_rows = jnp.repeat(sink.astype(jnp.float32), BLOCK).reshape(N_KV_HEADS, GQA_GROUP * BLOCK, 1)
    kv_spec = lambda shift: pl.BlockSpec(
        (1, BLOCK, 2 * KV_W), lambda b, n: (b, jnp.clip(n + shift, 0, nb - 1), kv_col // (2 * KV_W)))
    const = lambda shape: pl.BlockSpec(shape, lambda b, n: (0,) * len(shape))
    return pl.pallas_call(
        functools.partial(_window_kernel, nb=nb),
        grid=(b_, nb),
        in_specs=[pl.BlockSpec((1, BLOCK, GROUP_W), lambda b, n: (b, n, q_col // GROUP_W)),
                  kv_spec(-1), kv_spec(0), kv_spec(1),
                  const(bias.shape), const(sink_rows.shape), const((1, GROUP_W))],
        out_specs=pl.BlockSpec((1, BLOCK, GROUP_W), lambda b, n: (b, n, 0)),
        out_shape=jax.ShapeDtypeStruct((b_, s_, GROUP_W), jnp.float32),
        compiler_params=pltpu.CompilerParams(dimension_semantics=("arbitrary", "arbitrary")),
        name="window_attention",
    )(proj, proj, proj, proj, bias, sink_rows, norm_g.astype(jnp.float32).reshape(1, GROUP_W))


def _rope_tables(row_idx, col_idx, n_heads):
    half = HEAD_DIM // 2
    inv = ROPE_THETA ** (-jnp.arange(0, half, 2, dtype=jnp.float32) / half)

    def tabs(pos):
        ang = pos.astype(jnp.float32)[:, None] * inv[None, :]
        c, sn = jnp.cos(ang), jnp.sin(ang)
        z = jnp.zeros_like(sn)
        return (jnp.concatenate([c, c], -1), jnp.concatenate([-sn, z], -1), jnp.concatenate([z, sn], -1))

    per_head = [jnp.concatenate([r, c], -1) for r, c in zip(tabs(row_idx), tabs(col_idx))]
    return tuple(jnp.tile(t, (1, n_heads)) for t in per_head)


def _global_prep_kernel(q_ref, kv_ref, c_ref, s1_ref, s2_ref, gq_ref, gk_ref, bdq_ref, bdk_ref,
                        qa_ref, kt_ref, va_ref, kn2_ref):
    f32 = jnp.float32
    bf16 = jnp.bfloat16
    hd = HEAD_DIM
    quarter = hd // 4

    def norm_rope(x, g, bd):
        w = x.shape[1]
        y = x * lax.rsqrt(_group_sum(x * x, bd) * (1.0 / hd) + EPS) * g
        return (y * c_ref[:, :w] + pltpu.roll(y, w - quarter, 1) * s1_ref[:, :w]
                + pltpu.roll(y, quarter, 1) * s2_ref[:, :w])

    lane = lax.broadcasted_iota(jnp.int32, (1, 2 * hd), 1)

    def pair_slot(x2, j, tail):
        first = x2 if j == 0 else pltpu.roll(x2, hd, 1)
        return jnp.where(lane < hd, first, tail)

    q = norm_rope(q_ref[0], gq_ref[...], bdq_ref[...]) * (LOG2E * hd ** -0.5)
    qf = q.astype(bf16).astype(f32)
    qn = jnp.sqrt(_group_sum(qf * qf, bdq_ref[...]))
    for h in range(N_HEADS):
        p2 = slice((h // 2) * 2 * hd, (h // 2 + 1) * 2 * hd)
        n2 = qn[:, p2] if h % 2 == 1 else pltpu.roll(qn[:, p2], hd, 1)
        tail = jnp.where(lane == hd, -n2, 0.0)
        qa_ref[0, h // GQA_GROUP, h % GQA_GROUP] = pair_slot(qf[:, p2], h % 2, tail).astype(bf16)
    kv = kv_ref[0]
    k = norm_rope(kv[:, :KV_W], gk_ref[...], bdk_ref[...])
    kf = k.astype(bf16).astype(f32)
    kn2_ref[0] = _group_sum(kf * kf, bdk_ref[...])
    v = kv[:, KV_W:]
    one = jnp.where(lane == hd, 1.0, 0.0)
    for j in range(N_KV_HEADS):
        kt_ref[0, j] = pair_slot(kf, j, 0.0).T.astype(bf16)
        va_ref[0, j] = pair_slot(v, j, one).astype(bf16)


def _global_prep(proj, q_col, kv_col, tables, q_norm_g, k_norm_g):
    b_, s_, _ = proj.shape
    assert KV_W == 2 * HEAD_DIM and q_col % GROUP_W == 0 and kv_col % (2 * KV_W) == 0
    tm = CONV_TM
    wide = 2 * HEAD_DIM
    bf16 = jnp.bfloat16
    const = lambda a: pl.BlockSpec(a.shape, lambda b, i: (0,) * a.ndim)
    tab = pl.BlockSpec((tm, GROUP_W), lambda b, i: (i, 0))
    gq = jnp.tile(q_norm_g.astype(jnp.float32), N_HEADS).reshape(1, GROUP_W)
    gk = jnp.tile(k_norm_g.astype(jnp.float32), N_KV_HEADS).reshape(1, KV_W)
    bdq, bdk = _group_ones(GROUP_W), _group_ones(KV_W)
    return pl.pallas_call(
        _global_prep_kernel, grid=(b_, s_ // tm),
        in_specs=[pl.BlockSpec((1, tm, GROUP_W), lambda b, i: (b, i, q_col // GROUP_W)),
                  pl.BlockSpec((1, tm, 2 * KV_W), lambda b, i: (b, i, kv_col // (2 * KV_W))),
                  tab, tab, tab, const(gq), const(gk), const(bdq), const(bdk)],
        out_specs=[pl.BlockSpec((1, N_KV_HEADS, GQA_GROUP, tm, wide), lambda b, i: (b, 0, 0, i, 0)),
                   pl.BlockSpec((1, N_KV_HEADS, wide, tm), lambda b, i: (b, 0, 0, i)),
                   pl.BlockSpec((1, N_KV_HEADS, tm, wide), lambda b, i: (b, 0, i, 0)),
                   pl.BlockSpec((1, tm, KV_W), lambda b, i: (b, i, 0))],
        out_shape=[jax.ShapeDtypeStruct((b_, N_KV_HEADS, GQA_GROUP, s_, wide), bf16),
                   jax.ShapeDtypeStruct((b_, N_KV_HEADS, wide, s_), bf16),
                   jax.ShapeDtypeStruct((b_, N_KV_HEADS, s_, wide), bf16),
                   jax.ShapeDtypeStruct((b_, s_, KV_W), jnp.float32)],
        compiler_params=pltpu.CompilerParams(dimension_semantics=("arbitrary", "arbitrary")),
        name="global_prep",
    )(proj, proj, *tables, gq, gk, bdq, bdk)


def _flash_kernel(q_ref, kt_ref, v_ref, o_ref, *, tq, tk, n_kc):
    m_rows = GQA_GROUP * tq
    q = q_ref[0, 0].reshape(m_rows, 2 * HEAD_DIM)

    def body(c, acc):
        off = pl.multiple_of(c * tk, tk)
        s = jnp.dot(q, kt_ref[0, 0, :, pl.ds(off, tk)], preferred_element_type=jnp.float32)
        p = jnp.exp2(s).astype(jnp.bfloat16)
        return acc + jnp.dot(p, v_ref[0, 0, pl.ds(off, tk), :], preferred_element_type=jnp.float32)

    acc = lax.fori_loop(0, n_kc, body, jnp.zeros((m_rows, 2 * HEAD_DIM), jnp.float32))
    o = acc[:, :HEAD_DIM] / acc[:, HEAD_DIM:HEAD_DIM + 1]
    o_ref[0] = jnp.concatenate([o[g * tq:(g + 1) * tq] for g in range(GQA_GROUP)], axis=-1)


def _rowmax_kernel(q_ref, kt_ref, m_ref, *, tq, tk, n_kc):
    m_rows = GQA_GROUP * tq
    q = q_ref[0, 0].reshape(m_rows, 2 * HEAD_DIM)

    def body(c, mx):
        off = pl.multiple_of(c * tk, tk)
        s = jnp.dot(q, kt_ref[0, 0, :, pl.ds(off, tk)], preferred_element_type=jnp.float32)
        for j in range(tk // 128):
            mx = jnp.maximum(mx, s[:, j * 128:(j + 1) * 128])
        return mx

    mx = lax.fori_loop(0, n_kc, body, jnp.full((m_rows, 128), -jnp.inf, jnp.float32))
    m_ref[0, 0] = jnp.max(mx, axis=-1, keepdims=True).reshape(GQA_GROUP, tq, 1)


def _attn_call(body, q, kt, v, name, *, tq=ATTN_TQ, tk=ATTN_TK):
    b_, _, _, s_, _ = q.shape
    wide = 2 * HEAD_DIM
    q_spec = pl.BlockSpec((1, 1, GQA_GROUP, tq, wide), lambda b, h, i: (b, h, 0, i, 0))
    kt_spec = pl.BlockSpec((1, 1, wide, s_), lambda b, h, i: (b, h, 0, 0))
    v_spec = pl.BlockSpec((1, 1, s_, wide), lambda b, h, i: (b, h, 0, 0))
    if v is None:
        operands, in_specs = (q, kt), [q_spec, kt_spec]
        o_spec = pl.BlockSpec((1, 1, GQA_GROUP, tq, 1), lambda b, h, i: (b, h, 0, i, 0))
        o_shape = (b_, N_KV_HEADS, GQA_GROUP, s_, 1)
    else:
        operands, in_specs = (q, kt, v), [q_spec, kt_spec, v_spec]
        o_spec = pl.BlockSpec((1, tq, GQA_GROUP * HEAD_DIM), lambda b, h, i: (b, i, h))
        o_shape = (b_, s_, N_HEADS * HEAD_DIM)
    return pl.pallas_call(
        functools.partial(body, tq=tq, tk=tk, n_kc=s_ // tk),
        out_shape=jax.ShapeDtypeStruct(o_shape, jnp.float32),
        grid=(b_, N_KV_HEADS, s_ // tq),
        in_specs=in_specs, out_specs=o_spec,
        compiler_params=pltpu.CompilerParams(
            dimension_semantics=("arbitrary", "arbitrary", "arbitrary"),
            vmem_limit_bytes=ATTN_VMEM_LIMIT_BYTES),
        name=name,
    )(*operands)


def global_mixer(proj, tables, q_norm_g, k_norm_g):
    bf16 = jnp.bfloat16
    hd = HEAD_DIM
    qa, kt, va, kn2 = _global_prep(proj, PROJ_DEST["d_q"][1], PROJ_DEST["d_kv"][1], tables,
                                   q_norm_g, k_norm_g)
    kn = jnp.sqrt(jnp.max(kn2, axis=1)[:, ::hd])
    c = (kn * ATTN_ROUND_UP).astype(bf16)
    qmax = jnp.max(-qa[:, :, :, :, hd].astype(jnp.float32), axis=(2, 3))
    with_key_row = lambda row: kt.at[:, :, hd, :].set(jnp.broadcast_to(row[:, :, None], kt.shape[:2] + kt.shape[3:]))

    def exact_shift():
        m = _attn_call(_rowmax_kernel, qa, kt, None, "global_attention_rowmax")
        return qa.at[:, :, :, :, hd].set((-m[..., 0]).astype(bf16)), with_key_row(jnp.ones_like(c))

    qa, kta = lax.cond(jnp.max(qmax * c.astype(jnp.float32)) < ATTN_SAFE_SHIFT,
                       lambda: (qa, with_key_row(c)), exact_shift)
    return _attn_call(_flash_kernel, qa, kta, va, "global_flash_attention")


def _row_copy(src, src_row, dst, dst_row, sem):
    return pltpu.make_async_copy(src.at[pl.ds(src_row, 1)], dst.at[pl.ds(dst_row, 1)], sem)


def _moe_dispatch_kernel(pos_ref, x_ref, g_ref, zeros_hbm, xs_hbm, xn_scr, sem, *, tb):
    del zeros_hbm
    i = pl.program_id(0)
    x = x_ref[...]
    xn_scr[...] = x * lax.rsqrt(jnp.mean(x * x, axis=-1, keepdims=True) + EPS) * g_ref[...]

    def issue(r, carry):
        for k in range(TOP_K):
            _row_copy(xn_scr, r, xs_hbm, pos_ref[(i * tb + r) * TOP_K + k], sem).start()
        return carry

    lax.fori_loop(0, tb, issue, 0, unroll=8)
    for _ in range(TOP_K):
        pltpu.make_async_copy(xn_scr, xn_scr, sem).wait()


def _moe_expert_kernel(blk_e_ref, blk_rows_ref, xs_ref, wg_ref, wu_ref, wd_ref, o_ref, *, tm):
    del blk_e_ref
    rows = blk_rows_ref[pl.program_id(0)]

    @pl.when(rows > 0)
    def _():
        x = xs_ref[...].astype(jnp.bfloat16)
        g = jnp.dot(x, wg_ref[0], preferred_element_type=jnp.float32)
        u = jnp.dot(x, wu_ref[0], preferred_element_type=jnp.float32)
        h = (g * jax.nn.sigmoid(g) * u).astype(jnp.bfloat16)
        o_ref[...] = jnp.dot(h, wd_ref[0], preferred_element_type=jnp.float32)

    @pl.when(rows == 0)
    def _():
        o_ref[...] = jnp.zeros((tm, o_ref.shape[1]), jnp.float32)


def _moe_combine_kernel(pos_ref, x_ref, w_ref, g_ref, os_hbm, y_ref, buf, sem, *, tb, out_norm):
    i = pl.program_id(0)

    def issue(r, carry):
        for k in range(TOP_K):
            _row_copy(os_hbm, pos_ref[(i * tb + r) * TOP_K + k], buf.at[k], r, sem).start()
        return carry

    lax.fori_loop(0, tb, issue, 0, unroll=8)
    for k in range(TOP_K):
        pltpu.make_async_copy(buf.at[k], buf.at[k], sem).wait()
    w = w_ref[...]
    y = x_ref[...]
    for k in range(TOP_K):
        y = y + w[:, k:k + 1] * buf[k]
    if out_norm:
        y = y * lax.rsqrt(jnp.mean(y * y, axis=-1, keepdims=True) + EPS) * g_ref[...]
    y_ref[...] = y


def _router_kernel(x_ref, g_ref, w_ref, o_ref):
    x = x_ref[...]
    xn = x * lax.rsqrt(jnp.mean(x * x, axis=-1, keepdims=True) + EPS) * g_ref[...]
    o_ref[...] = jnp.dot(xn.astype(jnp.bfloat16), w_ref[...], preferred_element_type=jnp.float32)


def _router_logits(xt, norm_g, w_group, w_expert):
    n_tok, d_ = xt.shape
    n_log = N_EXPERT_GROUPS + N_EXPERTS
    w = jnp.concatenate([w_group, w_expert, jnp.zeros((d_, ROUTER_W - n_log), w_group.dtype)], axis=1)
    return pl.pallas_call(
        _router_kernel, grid=(n_tok // CONV_TM,),
        in_specs=[pl.BlockSpec((CONV_TM, d_), lambda i: (i, 0)), pl.BlockSpec((1, d_), lambda i: (0, 0)),
                  pl.BlockSpec((d_, ROUTER_W), lambda i: (0, 0))],
        out_specs=pl.BlockSpec((CONV_TM, ROUTER_W), lambda i: (i, 0)),
        out_shape=jax.ShapeDtypeStruct((n_tok, ROUTER_W), jnp.float32),
        compiler_params=pltpu.CompilerParams(dimension_semantics=("arbitrary",)),
        name="moe_router",
    )(xt, norm_g.astype(jnp.float32).reshape(1, d_), w.astype(jnp.bfloat16))


def _moe_route(logits, b_group, b_expert, tm):
    n_tok = logits.shape[0]
    gp = jax.nn.softmax(logits[:, :N_EXPERT_GROUPS] + b_group.astype(jnp.float32), axis=-1)
    g_idx = jnp.argmax(gp, axis=-1, keepdims=True)
    g_w = jnp.max(gp, axis=-1, keepdims=True)
    elog = logits[:, N_EXPERT_GROUPS:N_EXPERT_GROUPS + N_EXPERTS] + b_expert.astype(jnp.float32)
    elog = elog.reshape(n_tok, N_EXPERT_GROUPS, EXPERTS_PER_GROUP)
    elog_sel = jnp.take_along_axis(elog, g_idx[:, :, None], axis=1)[:, 0]
    e_w, e_idx = lax.top_k(jax.nn.softmax(elog_sel, axis=-1), TOP_K)
    e_w = e_w / jnp.sum(e_w, axis=-1, keepdims=True)
    weights = g_w * e_w
    experts = (g_idx * EXPERTS_PER_GROUP + e_idx).reshape(-1)
    onehot = (experts[:, None] == jnp.arange(N_EXPERTS)[None, :]).astype(jnp.int32)
    csum = jnp.cumsum(onehot, axis=0)
    counts = csum[-1]
    rank = jnp.sum(onehot * csum, axis=1) - 1
    padded = ((counts + tm - 1) // tm) * tm
    pend = jnp.cumsum(padded)
    pstart = pend - padded
    pos = (pstart[experts] + rank).astype(jnp.int32)
    n_blk = (n_tok * TOP_K) // tm + N_EXPERTS
    blk_start = jnp.arange(n_blk) * tm
    blk_e = jnp.minimum(jnp.sum(pend[None, :] <= blk_start[:, None], axis=1), N_EXPERTS - 1).astype(jnp.int32)
    blk_rows = jnp.clip(counts[blk_e] - (blk_start - pstart[blk_e]), 0, tm).astype(jnp.int32)
    return weights, pos, blk_e, blk_rows, n_blk


def hier_moe_residual(x, norm_g, w_group, b_group, w_expert, b_expert, w_gate, w_up, w_down,
                      out_norm_g=None, *, tm=MOE_TM, tb=MOE_TB):
    b_, s_, d_ = x.shape
    n_tok = b_ * s_
    xt = x.reshape(n_tok, d_)
    weights, pos, blk_e, blk_rows, n_blk = _moe_route(_router_logits(xt, norm_g, w_group, w_expert),
                                                      b_group, b_expert, tm)
    n_pad = n_blk * tm
    vmem = pltpu.CompilerParams(dimension_semantics=("arbitrary",),
                                vmem_limit_bytes=MOE_VMEM_LIMIT_BYTES)
    xs = pl.pallas_call(
        functools.partial(_moe_dispatch_kernel, tb=tb),
        grid_spec=pltpu.PrefetchScalarGridSpec(
            num_scalar_prefetch=1, grid=(n_tok // tb,),
            in_specs=[pl.BlockSpec((tb, d_), lambda i, pos: (i, 0)),
                      pl.BlockSpec((1, d_), lambda i, pos: (0, 0)),
                      pl.BlockSpec(memory_space=pl.ANY)],
            out_specs=pl.BlockSpec(memory_space=pl.ANY),
            scratch_shapes=[pltpu.VMEM((tb, d_), jnp.float32), pltpu.SemaphoreType.DMA(())]),
        out_shape=jax.ShapeDtypeStruct((n_pad, d_), jnp.float32),
        input_output_aliases={3: 0},
        compiler_params=vmem, name="moe_dispatch",
    )(pos, xt, norm_g.astype(jnp.float32).reshape(1, d_), jnp.zeros((n_pad, d_), jnp.float32))
    bf16 = jnp.bfloat16
    outs = pl.pallas_call(
        functools.partial(_moe_expert_kernel, tm=tm),
        grid_spec=pltpu.PrefetchScalarGridSpec(
            num_scalar_prefetch=2, grid=(n_blk,),
            in_specs=[pl.BlockSpec((tm, d_), lambda i, be, br: (i, 0)),
                      pl.BlockSpec((1, d_, D_EXPERT), lambda i, be, br: (be[i], 0, 0)),
                      pl.BlockSpec((1, d_, D_EXPERT), lambda i, be, br: (be[i], 0, 0)),
                      pl.BlockSpec((1, D_EXPERT, d_), lambda i, be, br: (be[i], 0, 0))],
            out_specs=pl.BlockSpec((tm, d_), lambda i, be, br: (i, 0))),
        out_shape=jax.ShapeDtypeStruct((n_pad, d_), jnp.float32),
        compiler_params=vmem, name="moe_experts",
    )(blk_e, blk_rows, xs, w_gate.astype(bf16), w_up.astype(bf16), w_down.astype(bf16))
    y = pl.pallas_call(
        functools.partial(_moe_combine_kernel, tb=tb, out_norm=out_norm_g is not None),
        grid_spec=pltpu.PrefetchScalarGridSpec(
            num_scalar_prefetch=1, grid=(n_tok // tb,),
            in_specs=[pl.BlockSpec((tb, d_), lambda i, pos: (i, 0)),
                      pl.BlockSpec((tb, TOP_K), lambda i, pos: (i, 0)),
                      pl.BlockSpec((1, d_), lambda i, pos: (0, 0)),
                      pl.BlockSpec(memory_space=pl.ANY)],
            out_specs=pl.BlockSpec((tb, d_), lambda i, pos: (i, 0)),
            scratch_shapes=[pltpu.VMEM((TOP_K, tb, d_), jnp.float32), pltpu.SemaphoreType.DMA(())]),
        out_shape=jax.ShapeDtypeStruct((n_tok, d_), jnp.float32),
        compiler_params=vmem, name="moe_combine",
    )(pos, xt, weights, (norm_g if out_norm_g is None else out_norm_g).astype(jnp.float32).reshape(1, d_), outs)
    return y.reshape(b_, s_, d_)


PROJ_DEST = {"a_qkv": (0, 0), "a_z": (1, 1536), "b_in": (4, 2048), "c_q": (5, 3584), "d_q": (7, 4096),
             "c_kv": (6, 4608), "d_kv": (8, 4864), "a_beta": (2, 5120), "a_alpha": (3, 5136)}
PROJ_WIDTH = 5632
assert PROJ_DEST["a_alpha"][1] == PROJ_DEST["a_beta"][1] + 2 * N_HEADS


def _in_proj_weight(w):
    starts = [0] + _split_points()
    cols = jnp.zeros((w.shape[0], PROJ_WIDTH), jnp.bfloat16)
    for seg, dest in PROJ_DEST.values():
        cols = lax.dynamic_update_slice(
            cols, w[:, starts[seg]:starts[seg] + IN_SPLIT_SIZES[seg]].astype(jnp.bfloat16), (0, dest))
    return cols


def _in_proj_kernel(x_ref, g_ref, w_ref, o_ref, xn_scr):
    @pl.when(pl.program_id(1) == 0)
    def _():
        x = x_ref[...]
        xn = x * lax.rsqrt(jnp.mean(x * x, axis=-1, keepdims=True) + EPS) * g_ref[...]
        xn_scr[...] = xn.astype(jnp.bfloat16)

    o_ref[...] = jnp.dot(xn_scr[...], w_ref[...], preferred_element_type=jnp.float32)


def in_proj(x, norm_g, w):
    n_tok, d_ = x.shape
    return pl.pallas_call(
        _in_proj_kernel,
        grid=(n_tok // PROJ_TM, PROJ_WIDTH // PROJ_TILE),
        in_specs=[pl.BlockSpec((PROJ_TM, d_), lambda i, j: (i, 0)),
                  pl.BlockSpec((1, d_), lambda i, j: (0, 0)),
                  pl.BlockSpec((d_, PROJ_TILE), lambda i, j: (0, j))],
        out_specs=pl.BlockSpec((PROJ_TM, PROJ_TILE), lambda i, j: (i, j)),
        out_shape=jax.ShapeDtypeStruct((n_tok, PROJ_WIDTH), jnp.float32),
        scratch_shapes=[pltpu.VMEM((PROJ_TM, d_), jnp.bfloat16)],
        compiler_params=pltpu.CompilerParams(dimension_semantics=("arbitrary", "arbitrary"),
                                             vmem_limit_bytes=PROJ_VMEM_LIMIT_BYTES),
        name="in_proj",
    )(x, norm_g.astype(jnp.float32).reshape(1, d_), _in_proj_weight(w))


def _out_proj_kernel(x_ref, *refs, normed):
    n = len(normed)
    y_refs, g_ref, w_ref, o_ref = refs[:n], refs[n], refs[n + 1], refs[n + 2]
    acc = x_ref[...]
    for k, y_ref in enumerate(y_refs):
        y = y_ref[...]
        if normed[k]:
            y = y * lax.rsqrt(jnp.mean(y * y, axis=-1, keepdims=True) + EPS) * g_ref[k:k + 1, :]
        acc = acc + jnp.dot(y.astype(jnp.bfloat16), w_ref[k * GROUP_W:(k + 1) * GROUP_W, :],
                            preferred_element_type=jnp.float32)
    o_ref[...] = acc


def out_proj_residual(x, ys, gains, w):
    n_tok, d_ = x.shape
    row = lambda width: pl.BlockSpec((OUT_TM, width), lambda i: (i, 0))
    g = jnp.stack([jnp.ones((GROUP_W,), jnp.float32) if gk is None else gk.astype(jnp.float32)
                   for gk in gains])
    return pl.pallas_call(
        functools.partial(_out_proj_kernel, normed=tuple(gk is not None for gk in gains)),
        grid=(n_tok // OUT_TM,),
        in_specs=[row(d_)] + [row(GROUP_W)] * len(ys) + [pl.BlockSpec(g.shape, lambda i: (0, 0)),
                                                          pl.BlockSpec(w.shape, lambda i: (0, 0))],
        out_specs=row(d_),
        out_shape=jax.ShapeDtypeStruct((n_tok, d_), jnp.float32),
        compiler_params=pltpu.CompilerParams(dimension_semantics=("arbitrary",),
                                             vmem_limit_bytes=PROJ_VMEM_LIMIT_BYTES),
        name="out_proj",
    )(x, *ys, g, w.astype(jnp.bfloat16))


def kernel(x, norm_mix, w_in, gdn_conv, gdn_a_log, gdn_dt_bias, gdn_norm, hy_conv, hy_w1, hy_b1, hy_freq1, hy_w2, hy_b2, hy_freq2, hy_w3, hy_deltas, hy_bias, hy_norm, swa_sink, swa_norm, ga_q_norm, ga_k_norm, ga_norm, w_out, norm_ffn, moe_w_group, moe_b_group, moe_w_expert, moe_b_expert, moe_w_gate, moe_w_up, moe_w_down, norm_final):
    b_, s_, _ = x.shape
    rows = s_ // GRID_W
    row_idx = jnp.repeat(jnp.arange(rows), GRID_W)
    col_idx = jnp.tile(jnp.arange(GRID_W), rows)
    rope_tables = _rope_tables(row_idx, col_idx, N_HEADS)
    pos_feat = hyena_pos_features(s_)
    slopes = alibi_slopes(N_HEADS)
    n_tok = b_ * s_
    for l in range(DEPTH):
        proj = in_proj(x.reshape(n_tok, D_MODEL), norm_mix[l], w_in[l]).reshape(b_, s_, PROJ_WIDTH)
        y_a = gdn_mixer(proj, gdn_conv[l], gdn_a_log[l], gdn_dt_bias[l], gdn_norm[l])
        kf = hyena_filters_f(pos_feat, hy_w1[l], hy_b1[l], hy_freq1[l], hy_w2[l], hy_b2[l],
                             hy_freq2[l], hy_w3[l], hy_deltas[l])
        y_b = hyena_mixer(proj, hy_conv[l], kf, hy_bias[l])
        y_c = window_mixer(proj, PROJ_DEST["c_q"][1], PROJ_DEST["c_kv"][1], swa_sink[l], slopes, swa_norm[l])
        y_d = global_mixer(proj, rope_tables, ga_q_norm[l], ga_k_norm[l])
        x = out_proj_residual(x.reshape(n_tok, D_MODEL),
                              [y.reshape(n_tok, GROUP_W) for y in (y_a, y_b, y_c, y_d)],
                              [None, hy_norm[l], None, ga_norm[l]], w_out[l]).reshape(b_, s_, D_MODEL)
        x = hier_moe_residual(x, norm_ffn[l], moe_w_group[l], moe_b_group[l], moe_w_expert[l],
                              moe_b_expert[l], moe_w_gate[l], moe_w_up[l], moe_w_down[l],
                              norm_final if l == DEPTH - 1 else None)
    return x
```

```python
import functools
import math

import jax
import jax.numpy as jnp
import numpy as np
from jax import lax
from jax.experimental import pallas as pl
from jax.experimental.pallas import tpu as pltpu

D_MODEL = 2048
DEPTH = 2
N_MIXERS = 4
GROUP_W = D_MODEL // N_MIXERS
HEAD_DIM = 64
N_HEADS = GROUP_W // HEAD_DIM
N_KV_HEADS = 2
GQA_GROUP = N_HEADS // N_KV_HEADS
KV_W = N_KV_HEADS * HEAD_DIM
SHORT_CONV = 3
GDN_CHUNK = 64
HY_ORDER = 2
HY_EMB = 33
HY_BANDS = (HY_EMB - 1) // 2
WINDOW = 128
BLOCK = 128
GRID_W = 64
ROPE_THETA = 10000.0
N_EXPERT_GROUPS = 4
EXPERTS_PER_GROUP = 8
N_EXPERTS = N_EXPERT_GROUPS * EXPERTS_PER_GROUP
TOP_K = 2
D_EXPERT = 512
MOE_BLOCK = 128
EPS = 1e-6
IN_SPLIT_SIZES = (3 * GROUP_W, GROUP_W, 2 * N_HEADS, 2 * N_HEADS, 3 * GROUP_W,
                  GROUP_W, 2 * KV_W, GROUP_W, 2 * KV_W)

ATTN_VMEM_LIMIT_BYTES = 48 * 1024 * 1024
ATTN_TQ = 256
ATTN_TK = 2048
WINDOW_SUB = 4
WINDOW_MASK = -1e30
GDN_VMEM_LIMIT_BYTES = 48 * 1024 * 1024
PROJ_VMEM_LIMIT_BYTES = 48 * 1024 * 1024
CONV_TM = 512
CONV_HALO = 8
PROJ_TM = 1024
PROJ_TILE = 512
OUT_TM = 256
GDN_SB = 256
GDN_INV_BASE = 8
GDN_KCHUNK = 64
GDN_HG = 4
FFT_VMEM_LIMIT_BYTES = 48 * 1024 * 1024
FFT_N2 = 256
FFT_COL_TILE = 4096
MOE_VMEM_LIMIT_BYTES = 48 * 1024 * 1024
MOE_TM = 512
MOE_TB = 256
ROUTER_W = 128
LOG2E = 1.4426950408889634
ATTN_SAFE_SHIFT = 50.0
ATTN_ROUND_UP = 1.01


def _split_points():
    return [int(v) for v in np.cumsum(IN_SPLIT_SIZES)[:-1]]


def _group_ones(width):
    r = np.arange(width) // HEAD_DIM
    return jnp.asarray(r[:, None] == r[None, :], jnp.bfloat16)


def _group_sum(x, bd):
    hi = x.astype(jnp.bfloat16)
    lo = (x - hi.astype(jnp.float32)).astype(jnp.bfloat16)
    return (jnp.dot(hi, bd, preferred_element_type=jnp.float32)
            + jnp.dot(lo, bd, preferred_element_type=jnp.float32))


def _conv_prep_kernel(*refs, modes, n_steps):
    n = len(modes)
    w_ref, bd_ref = refs[3 * n], refs[3 * n + 1]
    o_refs = refs[3 * n + 2:]
    i = pl.program_id(1)
    tm = refs[0].shape[1]
    row = lax.broadcasted_iota(jnp.int32, (tm, 1), 0)
    for t, mode in enumerate(modes):
        x_ref, prev_ref, next_ref = refs[3 * t:3 * t + 3]
        x = x_ref[0]
        prev = jnp.where(i > 0, prev_ref[0][CONV_HALO - 1:CONV_HALO], 0.0)
        nxt = jnp.where(i < n_steps - 1, next_ref[0][0:1], 0.0)
        x_prev = jnp.where(row == 0, prev, pltpu.roll(x, 1, 0))
        x_next = jnp.where(row == tm - 1, nxt, pltpu.roll(x, tm - 1, 0))
        w = w_ref[:, t * GROUP_W:(t + 1) * GROUP_W]
        y = x_prev * w[0:1] + x * w[1:2] + x_next * w[2:3]
        if mode != "plain":
            y = y * jax.nn.sigmoid(y)
        if mode.startswith("silu_l2"):
            y = y * lax.rsqrt(_group_sum(y * y, bd_ref[...]) + EPS)
        if mode == "silu_l2_scaled":
            y = y * (HEAD_DIM ** -0.5)
        o_refs[t][0] = y


def _conv_prep(proj, col, conv_w, modes):
    b_, s_, _ = proj.shape
    assert col % GROUP_W == 0
    tm = CONV_TM
    n_steps = s_ // tm
    hb = tm // CONV_HALO
    specs = []
    for t in range(len(modes)):
        c = col // GROUP_W + t
        specs += [pl.BlockSpec((1, tm, GROUP_W), lambda b, i, c=c: (b, i, c)),
                  pl.BlockSpec((1, CONV_HALO, GROUP_W), lambda b, i, c=c: (b, jnp.maximum(i * hb - 1, 0), c)),
                  pl.BlockSpec((1, CONV_HALO, GROUP_W),
                               lambda b, i, c=c: (b, jnp.minimum((i + 1) * hb, s_ // CONV_HALO - 1), c))]
    bd = _group_ones(GROUP_W)
    w = conv_w.astype(jnp.float32)
    specs += [pl.BlockSpec(w.shape, lambda b, i: (0, 0)), pl.BlockSpec(bd.shape, lambda b, i: (0, 0))]
    out = jax.ShapeDtypeStruct((b_, s_, GROUP_W), jnp.float32)
    return pl.pallas_call(
        functools.partial(_conv_prep_kernel, modes=modes, n_steps=n_steps),
        grid=(b_, n_steps), in_specs=specs,
        out_specs=[pl.BlockSpec((1, tm, GROUP_W), lambda b, i: (b, i, 0))] * len(modes),
        out_shape=[out] * len(modes),
        compiler_params=pltpu.CompilerParams(dimension_semantics=("arbitrary", "arbitrary")),
        name="conv_prep",
    )(*([proj] * (3 * len(modes))), w, bd)


def _gdn_post_kernel(of_ref, ob_ref, z_ref, g_ref, bd_ref, y_ref):
    o = of_ref[0] + ob_ref[0]
    z = z_ref[0]
    ms = _group_sum(o * o, bd_ref[...]) * (1.0 / HEAD_DIM)
    y_ref[0] = o * lax.rsqrt(ms + EPS) * g_ref[...] * (z * jax.nn.sigmoid(z))


def _gdn_post(o_fwd, o_bwd, proj, z_col, norm_g):
    b_, s_, w_ = o_fwd.shape
    tm = CONV_TM
    tok = pl.BlockSpec((1, tm, w_), lambda b, i: (b, i, 0))
    bd = _group_ones(w_)
    g = jnp.tile(norm_g.astype(jnp.float32), N_HEADS).reshape(1, w_)
    return pl.pallas_call(
        _gdn_post_kernel, grid=(b_, s_ // tm),
        in_specs=[tok, tok, pl.BlockSpec((1, tm, w_), lambda b, i: (b, i, z_col // w_)),
                  pl.BlockSpec(g.shape, lambda b, i: (0, 0)), pl.BlockSpec(bd.shape, lambda b, i: (0, 0))],
        out_specs=tok, out_shape=jax.ShapeDtypeStruct((b_, s_, w_), jnp.float32),
        compiler_params=pltpu.CompilerParams(dimension_semantics=("arbitrary", "arbitrary")),
        name="gdn_post",
    )(o_fwd, o_bwd, proj, g, bd)


def _gdn_masks(rev):
    r = np.arange(GDN_SB)
    i, j = r[:, None], r[None, :]
    same = lambda s: (i // s) == (j // s)
    before = (i < j) if rev else (i > j)
    chunk = same(GDN_KCHUNK)
    masks = [chunk & (before | (i == j)),
             chunk,
             chunk & before,
             i == j,
             same(GDN_INV_BASE),
             same(HEAD_DIM)]
    s = GDN_INV_BASE
    while s < GDN_KCHUNK:
        masks.append(same(2 * s) & ~same(s))
        s *= 2
    return np.stack(masks).astype(np.float32)


def _mm_bf16(a, b):
    return jnp.dot(a.astype(jnp.bfloat16), b.astype(jnp.bfloat16), preferred_element_type=jnp.float32)


def _unit_triangular_inverses(a_list, eye, m_base, m_offs):
    n_list = [-a * m_base for a in a_list]
    t_list = [eye + n for n in n_list]
    power = 2
    while power < GDN_INV_BASE:
        n_list = [_mm_bf16(n, n) for n in n_list]
        t_list = [t + _mm_bf16(t, n) for t, n in zip(t_list, n_list)]
        power *= 2
    for m_off in m_offs:
        u_list = [_mm_bf16(a * m_off, t) for a, t in zip(a_list, t_list)]
        t_list = [t - _mm_bf16(t, u) for t, u in zip(t_list, u_list)]
    return t_list


def _gdn_kernel(mask_ref, expand_ref, q_ref, k_ref, v_ref, beta_ref, g_ref, gt_ref, o_ref,
                state, vstack, val_s, kcd_s, qd_s, kd_s, attn_s, *, rev):
    bf16 = jnp.bfloat16
    f32 = jnp.float32
    hd = HEAD_DIM
    gw = GDN_HG * hd
    n_groups = N_HEADS // GDN_HG
    rr = GDN_SB
    nt = (((1,), (1,)), ((), ()))
    tn = (((0,), (0,)), ((), ()))

    @pl.when(pl.program_id(1) == 0)
    def _():
        state[...] = jnp.zeros(state.shape, state.dtype)
        vstack[...] = jnp.zeros(vstack.shape, vstack.dtype)

    incl, ones, strict, eye, m_base, head_blk = (mask_ref[t] for t in range(6))
    m_offs = [mask_ref[t] for t in range(6, mask_ref.shape[0])]
    g = g_ref[0, 0]
    gc = _dot_f32(incl, g)
    gl = _dot_f32(ones, g)
    gct = lax.dot_general(gt_ref[0, 0], incl, nt, precision=lax.Precision.HIGHEST,
                          preferred_element_type=jnp.float32)
    expand = expand_ref[...]
    beta_w = _dot_f32(beta_ref[0, 0], expand)
    eg_w = jnp.exp(_dot_f32(gc, expand))
    ekd_w = jnp.exp(_dot_f32(gl - gc, expand))
    cd_w = jnp.exp(_dot_f32(gl, expand))
    lane = lax.broadcasted_iota(jnp.int32, (1, gw), 1)
    head_mask = [(lane // hd == j).astype(f32) for j in range(GDN_HG)]
    a_list = []
    for grp in range(n_groups):
        sl = slice(grp * gw, (grp + 1) * gw)
        q, k = q_ref[0][:, sl], k_ref[0][:, sl]
        kb = k * beta_w[:, sl]
        kbf = k.astype(bf16)
        qd_s[grp] = (q * eg_w[:, sl]).astype(bf16)
        kd_s[grp] = (k * ekd_w[:, sl]).astype(bf16)
        for j in range(GDN_HG):
            h = grp * GDN_HG + j
            dec = jnp.exp(jnp.minimum(gc[:, h:h + 1] - gct[h:h + 1, :], 0.0)) * incl
            a_list.append(lax.dot_general((kb * head_mask[j]).astype(bf16), kbf, nt,
                                          preferred_element_type=f32) * dec * strict)
            attn = lax.dot_general((q * head_mask[j]).astype(bf16), kbf, nt, preferred_element_type=f32) * dec
            attn_s[h] = attn.astype(bf16)
    t_list = _unit_triangular_inverses(a_list, eye, m_base, m_offs)
    for grp in range(n_groups):
        sl = slice(grp * gw, (grp + 1) * gw)
        vb = v_ref[0][:, sl] * beta_w[:, sl]
        kbg = k_ref[0][:, sl] * beta_w[:, sl] * eg_w[:, sl]
        val = jnp.zeros((rr, gw), f32)
        kcd = jnp.zeros((rr, gw), f32)
        for j in range(GDN_HG):
            t = t_list[grp * GDN_HG + j]
            val = val + _mm_bf16(t, vb * head_mask[j])
            kcd = kcd + _mm_bf16(t, kbg * head_mask[j])
        val_s[grp] = val
        kcd_s[grp] = kcd.astype(bf16)
    n_chunks = GDN_SB // GDN_KCHUNK
    for c in (reversed(range(n_chunks)) if rev else range(n_chunks)):
        rows = slice(c * GDN_KCHUNK, (c + 1) * GDN_KCHUNK)
        for grp in range(n_groups):
            s_old = state[grp]
            s_bf = s_old.astype(bf16)
            both = jnp.dot(jnp.concatenate([kcd_s[grp, rows, :], qd_s[grp, rows, :]], axis=0), s_bf,
                           preferred_element_type=f32)
            v_new = val_s[grp, rows, :] - both[:GDN_KCHUNK]
            for j in range(GDN_HG):
                vstack[grp, j * rr + c * GDN_KCHUNK:j * rr + (c + 1) * GDN_KCHUNK, :] = (
                    v_new * head_mask[j]).astype(bf16)
            attn_cat = jnp.concatenate([attn_s[grp * GDN_HG + j, rows, :] for j in range(GDN_HG)], axis=1)
            o = both[GDN_KCHUNK:] + jnp.dot(attn_cat, vstack[grp], preferred_element_type=f32)
            upd = lax.dot_general(kd_s[grp, rows, :], v_new.astype(bf16), tn, preferred_element_type=f32)
            state[grp] = s_old * cd_w[c * GDN_KCHUNK:c * GDN_KCHUNK + 1, grp * gw:(grp + 1) * gw] + upd * head_blk
            o_ref[0, rows, grp * gw:(grp + 1) * gw] = o


def _gdn_scan(q, k, v, beta, g, rev):
    b_, s_, w_ = q.shape
    assert GDN_SB == GDN_HG * HEAD_DIM and GDN_SB % GDN_KCHUNK == 0 and GDN_KCHUNK % GDN_CHUNK == 0
    n_sb = s_ // GDN_SB
    n_groups = N_HEADS // GDN_HG
    gw = GDN_HG * HEAD_DIM
    masks = jnp.asarray(_gdn_masks(rev))
    expand = jnp.asarray(np.arange(N_HEADS)[:, None] == np.arange(w_)[None, :] // HEAD_DIM, jnp.float32)
    step = (lambda i: n_sb - 1 - i) if rev else (lambda i: i)
    tok = pl.BlockSpec((1, GDN_SB, w_), lambda b, i: (b, step(i), 0))
    gate = pl.BlockSpec((1, 1, GDN_SB, N_HEADS), lambda b, i: (b, 0, step(i), 0))
    gate_t = pl.BlockSpec((1, 1, N_HEADS, GDN_SB), lambda b, i: (b, 0, 0, step(i)))
    per_group = lambda dt: pltpu.VMEM((n_groups, GDN_SB, gw), dt)
    return pl.pallas_call(
        functools.partial(_gdn_kernel, rev=rev),
        grid=(b_, n_sb),
        in_specs=[pl.BlockSpec(masks.shape, lambda b, i: (0, 0, 0)),
                  pl.BlockSpec(expand.shape, lambda b, i: (0, 0)), tok, tok, tok, gate, gate, gate_t],
        out_specs=tok,
        out_shape=jax.ShapeDtypeStruct((b_, s_, w_), jnp.float32),
        scratch_shapes=[pltpu.VMEM((n_groups, gw, gw), jnp.float32),
                        pltpu.VMEM((n_groups, GDN_HG * GDN_SB, gw), jnp.bfloat16),
                        per_group(jnp.float32), per_group(jnp.bfloat16), per_group(jnp.bfloat16),
                        per_group(jnp.bfloat16),
                        pltpu.VMEM((N_HEADS, GDN_SB, GDN_SB), jnp.bfloat16)],
        compiler_params=pltpu.CompilerParams(dimension_semantics=("arbitrary", "arbitrary"),
                                             vmem_limit_bytes=GDN_VMEM_LIMIT_BYTES),
        name="gdn_scan_rev" if rev else "gdn_scan_fwd",
    )(masks, expand, q, k, v, beta, g, g.transpose(0, 1, 3, 2))


def gdn_mixer(proj, conv_w, a_log, dt_bias, norm_g):
    b_, s_, _ = proj.shape
    q, k, v = _conv_prep(proj, PROJ_DEST["a_qkv"][1], conv_w, ("silu_l2_scaled", "silu_l2", "silu"))
    gates = proj[:, :, PROJ_DEST["a_beta"][1]:PROJ_DEST["a_beta"][1] + 4 * N_HEADS]
    b_in = gates[:, :, :2 * N_HEADS].reshape(b_, s_, 2, N_HEADS)
    a_in = gates[:, :, 2 * N_HEADS:].reshape(b_, s_, 2, N_HEADS)
    beta = jax.nn.sigmoid(b_in).transpose(0, 2, 1, 3)
    g = -jnp.exp(a_log.astype(jnp.float32)) * jax.nn.softplus(a_in + dt_bias.astype(jnp.float32))
    g = g.transpose(0, 2, 1, 3)
    o_fwd = _gdn_scan(q, k, v, beta[:, 0:1], g[:, 0:1], False)
    o_bwd = _gdn_scan(q, k, v, beta[:, 1:2], g[:, 1:2], True)
    return _gdn_post(o_fwd, o_bwd, proj, PROJ_DEST["a_z"][1], norm_g)


def hyena_pos_features(length):
    t = jnp.linspace(0.0, 1.0, length, dtype=jnp.float32)[:, None]
    w = 2.0 * math.pi * jnp.arange(length, dtype=jnp.float32) / length
    f = jnp.linspace(1e-4, HY_BANDS - 1, HY_BANDS, dtype=jnp.float32)
    fw = w[:, None] * f[None, :]
    return jnp.concatenate([t, jnp.cos(fw), -jnp.sin(fw)], axis=-1)


def hyena_filters_f(z, w1, b1, f1, w2, b2, f2, w3, deltas):
    f32 = jnp.float32
    length = z.shape[0]
    t = z[:, :1]
    h = jnp.sin(f1.astype(f32) * (z @ w1.astype(f32) + b1.astype(f32)))
    h = jnp.sin(f2.astype(f32) * (h @ w2.astype(f32) + b2.astype(f32)))
    h = (h @ w3.astype(f32)) * jnp.exp(-t * jnp.abs(deltas.astype(f32)))
    h = h.reshape(length, HY_ORDER, 2, GROUP_W)
    kern = jnp.concatenate([h[:, :, 0], jnp.zeros((1, HY_ORDER, GROUP_W), f32),
                            h[:0:-1, :, 1]], axis=0)
    return kern / (jnp.sum(jnp.abs(kern), axis=0, keepdims=True) + EPS)


def _dft_tables(n_fft):
    n1 = n_fft // FFT_N2
    def dft(n):
        kk = (np.arange(n)[:, None] * np.arange(n)[None, :]) % n
        ang = -2.0 * np.pi * kk / n
        return np.cos(ang), np.sin(ang)
    f1r, f1i = dft(n1)
    f2r, f2i = dft(FFT_N2)
    kk = (np.arange(n1)[:, None] * np.arange(FFT_N2)[None, :]) % n_fft
    tw = -2.0 * np.pi * kk / n_fft
    f32 = np.float32
    return dict(
        f1=np.concatenate([f1r, f1i], axis=0).astype(f32),
        f1_inv=(np.concatenate([f1r[:n1 // 2], f1i[:n1 // 2]], axis=0) / n_fft).astype(f32),
        f2=np.concatenate([f2r, f2i], axis=0).astype(f32),
        twr=np.cos(tw).astype(f32)[:, :, None], twi=np.sin(tw).astype(f32)[:, :, None])


def _dot_f32(a, b):
    return jnp.dot(a, b, precision=lax.Precision.HIGHEST, preferred_element_type=jnp.float32)


def _split_lhs(f):
    f = jnp.asarray(f, jnp.float32)
    hi = f.astype(jnp.bfloat16)
    lo = (f - hi.astype(jnp.float32)).astype(jnp.bfloat16)
    return jnp.concatenate([hi, hi, lo], axis=1)


def _dot_split(f3, x):
    hi = x.astype(jnp.bfloat16)
    lo = (x - hi.astype(jnp.float32)).astype(jnp.bfloat16)
    return jnp.dot(f3, jnp.concatenate([hi, lo, hi], axis=0), preferred_element_type=jnp.float32)


def _fft_stage1_kernel(f_ref, zr_ref, zi_ref, yr_ref, yi_ref, *, n1):
    f = f_ref[...]
    p = _dot_split(f, zr_ref[0])
    if zi_ref is None:
        yr_ref[...] = p[:n1]
        yi_ref[...] = p[n1:]
    else:
        q = _dot_split(f, zi_ref[0])
        yr_ref[...] = p[:n1] - q[n1:]
        yi_ref[...] = q[:n1] + p[n1:]


def _fft_stage1_real_kernel(f_ref, zr_ref, yr_ref, yi_ref, *, n1):
    _fft_stage1_kernel(f_ref, zr_ref, None, yr_ref, yi_ref, n1=n1)


def _fft_mid_kernel(f_ref, twr_ref, twi_ref, yr_ref, yi_ref, kr_ref, ki_ref, qr_ref, qi_ref):
    n2 = FFT_N2
    twr, twi = twr_ref[0], twi_ref[0]
    yr, yi = yr_ref[0], yi_ref[0]
    f = f_ref[...]
    p = _dot_split(f, yr * twr - yi * twi)
    q = _dot_split(f, yr * twi + yi * twr)
    xr = p[:n2] - q[n2:]
    xi = q[:n2] + p[n2:]
    if kr_ref is None:
        qr_ref[0] = xr
        qi_ref[0] = xi
        return
    kr, ki = kr_ref[0], ki_ref[0]
    p = _dot_split(f, xr * kr - xi * ki)
    q = _dot_split(f, xr * ki + xi * kr)
    wr = p[:n2] + q[n2:]
    wi = q[:n2] - p[n2:]
    qr_ref[0] = wr * twr + wi * twi
    qi_ref[0] = wi * twr - wr * twi


def _fft_mid_spectrum_kernel(f_ref, twr_ref, twi_ref, yr_ref, yi_ref, qr_ref, qi_ref):
    _fft_mid_kernel(f_ref, twr_ref, twi_ref, yr_ref, yi_ref, None, None, qr_ref, qi_ref)


def _fft_last_kernel(f_ref, qr_ref, qi_ref, u_ref, gate_ref, bias_ref, o_ref, *, nh):
    f = f_ref[...]
    p = _dot_split(f, qr_ref[...])
    q = _dot_split(f, qi_ref[...])
    bias = bias_ref[...]
    o_ref[0] = gate_ref[0] * (p[:nh] + q[nh:] + u_ref[0] * bias)
    o_ref[1] = gate_ref[1] * (q[:nh] - p[nh:] + u_ref[1] * bias)


def _fft_params(n_axes):
    return pltpu.CompilerParams(dimension_semantics=("arbitrary",) * n_axes,
                                vmem_limit_bytes=FFT_VMEM_LIMIT_BYTES)


def _fft_forward(tab, z, n_ch):
    n1 = tab["f1"].shape[1]
    parts, rows, cols = z.shape
    tn = min(FFT_COL_TILE, cols)
    f1 = _split_lhs(tab["f1"][:, :rows])
    y_shape = jax.ShapeDtypeStruct((n1, cols), jnp.float32)
    col_spec = pl.BlockSpec((n1, tn), lambda j: (0, j))
    z_specs = [pl.BlockSpec((1, rows, tn), lambda j, p=p: (p, 0, j)) for p in range(parts)]
    body = _fft_stage1_kernel if parts == 2 else _fft_stage1_real_kernel
    yr, yi = pl.pallas_call(
        functools.partial(body, n1=n1), grid=(cols // tn,),
        in_specs=[pl.BlockSpec(f1.shape, lambda j: (0, 0))] + z_specs,
        out_specs=[col_spec, col_spec], out_shape=[y_shape, y_shape],
        compiler_params=_fft_params(1), name="fft_stage1",
    )(f1, *([z] * parts))
    return yr.reshape(n1, FFT_N2, n_ch), yi.reshape(n1, FFT_N2, n_ch)


def _fft_mid(tab, yr, yi, kr=None, ki=None):
    n1, n2, n_ch = yr.shape
    slab = pl.BlockSpec((1, n2, n_ch), lambda i: (i, 0, 0))
    tw_spec = pl.BlockSpec((1, n2, 1), lambda i: (i, 0, 0))
    f2 = _split_lhs(tab["f2"])
    ops = [f2, jnp.asarray(tab["twr"]), jnp.asarray(tab["twi"]), yr, yi]
    specs = [pl.BlockSpec(f2.shape, lambda i: (0, 0)), tw_spec, tw_spec, slab, slab]
    body = _fft_mid_spectrum_kernel
    if kr is not None:
        ops += [kr, ki]
        specs += [slab, slab]
        body = _fft_mid_kernel
    shape = jax.ShapeDtypeStruct((n1, n2, n_ch), jnp.float32)
    return pl.pallas_call(
        body, grid=(n1,), in_specs=specs, out_specs=[slab, slab], out_shape=[shape, shape],
        compiler_params=_fft_params(1), name="fft_mid",
    )(*ops)


def _fft_conv_gate(tab, u, gate, bias, kr, ki):
    b_, length, n_ch = u.shape
    assert b_ == 2
    n1 = tab["f1"].shape[1]
    nh = n1 // 2
    cols = FFT_N2 * n_ch
    uv = u.reshape(b_, nh, cols)
    yr, yi = _fft_forward(tab, uv, n_ch)
    qr, qi = _fft_mid(tab, yr, yi, kr, ki)
    tn = min(FFT_COL_TILE, cols)
    f1_inv = _split_lhs(tab["f1_inv"])
    q_spec = pl.BlockSpec((n1, tn), lambda j: (0, j))
    u_spec = pl.BlockSpec((b_, nh, tn), lambda j: (0, 0, j))
    out = pl.pallas_call(
        functools.partial(_fft_last_kernel, nh=nh), grid=(cols // tn,),
        in_specs=[pl.BlockSpec(f1_inv.shape, lambda j: (0, 0)), q_spec, q_spec, u_spec, u_spec,
                  pl.BlockSpec((1, tn), lambda j: (0, 0))],
        out_specs=u_spec, out_shape=jax.ShapeDtypeStruct((b_, nh, cols), jnp.float32),
        compiler_params=_fft_params(1), name="fft_last",
    )(f1_inv, qr.reshape(n1, cols), qi.reshape(n1, cols), uv, gate.reshape(b_, nh, cols),
      jnp.tile(bias.astype(jnp.float32), tn // n_ch).reshape(1, tn))
    return out.reshape(b_, length, n_ch)


def hyena_mixer(proj, conv_w, kern, bias):
    dtype = proj.dtype
    length = proj.shape[1]
    x1, x2, v = _conv_prep(proj, PROJ_DEST["b_in"][1], conv_w, ("plain",) * 3)
    tab = _dft_tables(2 * length)
    n1 = tab["f1"].shape[1]
    n_filt = HY_ORDER * GROUP_W
    kr, ki = _fft_mid(tab, *_fft_forward(tab, kern.reshape(1, n1, FFT_N2 * n_filt), n_filt))
    y = _fft_conv_gate(tab, v, x1, bias[0], kr[:, :, :GROUP_W], ki[:, :, :GROUP_W])
    y = _fft_conv_gate(tab, y, x2, bias[1], kr[:, :, GROUP_W:], ki[:, :, GROUP_W:])
    return y.astype(dtype)


def alibi_slopes(n):
    return 2.0 ** (-8.0 * jnp.arange(1, n + 1, dtype=jnp.float32) / n)


def _window_kernel(q_ref, kp_ref, kc_ref, kn_ref, bias_ref, sink_ref, g_ref, o_ref, *, nb):
    bf16 = jnp.bfloat16
    nt = (((1,), (1,)), ((), ()))
    col = lax.broadcasted_iota(jnp.int32, (1, 3 * BLOCK), 1)
    q_all = q_ref[0] * (HEAD_DIM ** -0.5)
    kv_all = jnp.concatenate([kp_ref[0], kc_ref[0], kn_ref[0]], axis=0)
    chains = [(u, j) for u in range(WINDOW_SUB) for j in range(N_KV_HEADS)]
    s_list, v_list = [], []
    for u, j in chains:
        n = pl.program_id(1) * WINDOW_SUB + u
        edge = jnp.where(((n == 0) & (col < BLOCK)) | ((n == nb - 1) & (col >= 2 * BLOCK)), WINDOW_MASK, 0.0)
        q = q_all[u * BLOCK:(u + 1) * BLOCK]
        kv = kv_all[u * BLOCK:(u + 3) * BLOCK]
        q4 = jnp.concatenate([q[:, (j * GQA_GROUP + g) * HEAD_DIM:(j * GQA_GROUP + g + 1) * HEAD_DIM]
                              for g in range(GQA_GROUP)], axis=0)
        k = kv[:, j * HEAD_DIM:(j + 1) * HEAD_DIM]
        v_list.append(kv[:, KV_W + j * HEAD_DIM:KV_W + (j + 1) * HEAD_DIM].astype(bf16))
        s = lax.dot_general(q4.astype(bf16), k.astype(bf16), nt, preferred_element_type=jnp.float32)
        s_list.append(s + bias_ref[j] + edge)
    m_list = [jnp.maximum(jnp.max(s, axis=-1, keepdims=True), sink_ref[j]) for s, (u, j) in zip(s_list, chains)]
    p_list = [jnp.exp(s - m) for s, m in zip(s_list, m_list)]
    d_list = [jnp.sum(p, axis=-1, keepdims=True) + jnp.exp(sink_ref[j] - m)
              for p, m, (u, j) in zip(p_list, m_list, chains)]
    o_list = [jnp.dot(p.astype(bf16), v, preferred_element_type=jnp.float32) / d
              for p, v, d in zip(p_list, v_list, d_list)]
    for u in range(WINDOW_SUB):
        outs = []
        for j in range(N_KV_HEADS):
            o = o_list[u * N_KV_HEADS + j]
            outs += [o[g * BLOCK:(g + 1) * BLOCK] for g in range(GQA_GROUP)]
        o = jnp.concatenate(outs, axis=-1)
        o_ref[0, u * BLOCK:(u + 1) * BLOCK, :] = (
            o * lax.rsqrt(jnp.mean(o * o, axis=-1, keepdims=True) + EPS) * g_ref[...])


def window_mixer(proj, q_col, kv_col, sink, slopes, norm_g):
    b_, s_, _ = proj.shape
    assert q_col % GROUP_W == 0 and kv_col % (2 * KV_W) == 0
    nb = s_ // BLOCK
    sub = WINDOW_SUB
    n_steps = nb // sub
    rel = BLOCK + jnp.arange(BLOCK)[:, None] - jnp.arange(3 * BLOCK)[None, :]
    dist = jnp.abs(rel).astype(jnp.float32)
    bias = jnp.where(jnp.abs(rel) <= WINDOW, -slopes[:, None, None] * dist, WINDOW_MASK)
    bias = bias.reshape(N_KV_HEADS, GQA_GROUP * BLOCK, 3 * BLOCK)
    sink_rows = jnp.repeat(sink.astype(jnp.float32), BLOCK).reshape(N_KV_HEADS, GQA_GROUP * BLOCK, 1)
    kc = kv_col // (2 * KV_W)
    const = lambda shape: pl.BlockSpec(shape, lambda b, n: (0,) * len(shape))
    rows = pl.BlockSpec((1, sub * BLOCK, GROUP_W), lambda b, n: (b, n, q_col // GROUP_W))
    return pl.pallas_call(
        functools.partial(_window_kernel, nb=nb),
        grid=(b_, n_steps),
        in_specs=[rows,
                  pl.BlockSpec((1, BLOCK, 2 * KV_W), lambda b, n: (b, jnp.maximum(n * sub - 1, 0), kc)),
                  pl.BlockSpec((1, sub * BLOCK, 2 * KV_W), lambda b, n: (b, n, kc)),
                  pl.BlockSpec((1, BLOCK, 2 * KV_W), lambda b, n: (b, jnp.minimum((n + 1) * sub, nb - 1), kc)),
                  const(bias.shape), const(sink_rows.shape), const((1, GROUP_W))],
        out_specs=pl.BlockSpec((1, sub * BLOCK, GROUP_W), lambda b, n: (b, n, 0)),
        out_shape=jax.ShapeDtypeStruct((b_, s_, GROUP_W), jnp.float32),
        compiler_params=pltpu.CompilerParams(dimension_semantics=("arbitrary", "arbitrary")),
        name="window_attention",
    )(proj, proj, proj, proj, bias, sink_rows, norm_g.astype(jnp.float32).reshape(1, GROUP_W))


def _rope_tables(row_idx, col_idx, n_heads):
    half = HEAD_DIM // 2
    inv = ROPE_THETA ** (-jnp.arange(0, half, 2, dtype=jnp.float32) / half)

    def tabs(pos):
        ang = pos.astype(jnp.float32)[:, None] * inv[None, :]
        c, sn = jnp.cos(ang), jnp.sin(ang)
        z = jnp.zeros_like(sn)
        return (jnp.concatenate([c, c], -1), jnp.concatenate([-sn, z], -1), jnp.concatenate([z, sn], -1))

    per_head = [jnp.concatenate([r, c], -1) for r, c in zip(tabs(row_idx), tabs(col_idx))]
    return tuple(jnp.tile(t, (1, n_heads)) for t in per_head)


def _global_prep_kernel(q_ref, kv_ref, c_ref, s1_ref, s2_ref, gq_ref, gk_ref, bdq_ref, bdk_ref,
                        qa_ref, kt_ref, va_ref, kn2_ref):
    f32 = jnp.float32
    bf16 = jnp.bfloat16
    hd = HEAD_DIM
    quarter = hd // 4

    def norm_rope(x, g, bd):
        w = x.shape[1]
        y = x * lax.rsqrt(_group_sum(x * x, bd) * (1.0 / hd) + EPS) * g
        return (y * c_ref[:, :w] + pltpu.roll(y, w - quarter, 1) * s1_ref[:, :w]
                + pltpu.roll(y, quarter, 1) * s2_ref[:, :w])

    lane = lax.broadcasted_iota(jnp.int32, (1, 2 * hd), 1)

    def pair_slot(x2, j, tail):
        first = x2 if j == 0 else pltpu.roll(x2, hd, 1)
        return jnp.where(lane < hd, first, tail)

    q = norm_rope(q_ref[0], gq_ref[...], bdq_ref[...]) * (LOG2E * hd ** -0.5)
    qf = q.astype(bf16).astype(f32)
    qn = jnp.sqrt(_group_sum(qf * qf, bdq_ref[...]))
    for h in range(N_HEADS):
        p2 = slice((h // 2) * 2 * hd, (h // 2 + 1) * 2 * hd)
        n2 = qn[:, p2] if h % 2 == 1 else pltpu.roll(qn[:, p2], hd, 1)
        tail = jnp.where(lane == hd, -n2, 0.0)
        qa_ref[0, h // GQA_GROUP, h % GQA_GROUP] = pair_slot(qf[:, p2], h % 2, tail).astype(bf16)
    kv = kv_ref[0]
    k = norm_rope(kv[:, :KV_W], gk_ref[...], bdk_ref[...])
    kf = k.astype(bf16).astype(f32)
    kn2_ref[0] = _group_sum(kf * kf, bdk_ref[...])
    v = kv[:, KV_W:]
    one = jnp.where(lane == hd, 1.0, 0.0)
    for j in range(N_KV_HEADS):
        kt_ref[0, j] = pair_slot(kf, j, 0.0).T.astype(bf16)
        va_ref[0, j] = pair_slot(v, j, one).astype(bf16)


def _global_prep(proj, q_col, kv_col, tables, q_norm_g, k_norm_g):
    b_, s_, _ = proj.shape
    assert KV_W == 2 * HEAD_DIM and q_col % GROUP_W == 0 and kv_col % (2 * KV_W) == 0
    tm = CONV_TM
    wide = 2 * HEAD_DIM
    bf16 = jnp.bfloat16
    const = lambda a: pl.BlockSpec(a.shape, lambda b, i: (0,) * a.ndim)
    tab = pl.BlockSpec((tm, GROUP_W), lambda b, i: (i, 0))
    gq = jnp.tile(q_norm_g.astype(jnp.float32), N_HEADS).reshape(1, GROUP_W)
    gk = jnp.tile(k_norm_g.astype(jnp.float32), N_KV_HEADS).reshape(1, KV_W)
    bdq, bdk = _group_ones(GROUP_W), _group_ones(KV_W)
    return pl.pallas_call(
        _global_prep_kernel, grid=(b_, s_ // tm),
        in_specs=[pl.BlockSpec((1, tm, GROUP_W), lambda b, i: (b, i, q_col // GROUP_W)),
                  pl.BlockSpec((1, tm, 2 * KV_W), lambda b, i: (b, i, kv_col // (2 * KV_W))),
                  tab, tab, tab, const(gq), const(gk), const(bdq), const(bdk)],
        out_specs=[pl.BlockSpec((1, N_KV_HEADS, GQA_GROUP, tm, wide), lambda b, i: (b, 0, 0, i, 0)),
                   pl.BlockSpec((1, N_KV_HEADS, wide, tm), lambda b, i: (b, 0, 0, i)),
                   pl.BlockSpec((1, N_KV_HEADS, tm, wide), lambda b, i: (b, 0, i, 0)),
                   pl.BlockSpec((1, tm, KV_W), lambda b, i: (b, i, 0))],
        out_shape=[jax.ShapeDtypeStruct((b_, N_KV_HEADS, GQA_GROUP, s_, wide), bf16),
                   jax.ShapeDtypeStruct((b_, N_KV_HEADS, wide, s_), bf16),
                   jax.ShapeDtypeStruct((b_, N_KV_HEADS, s_, wide), bf16),
                   jax.ShapeDtypeStruct((b_, s_, KV_W), jnp.float32)],
        compiler_params=pltpu.CompilerParams(dimension_semantics=("arbitrary", "arbitrary")),
        name="global_prep",
    )(proj, proj, *tables, gq, gk, bdq, bdk)


def _flash_kernel(q_ref, kt_ref, v_ref, o_ref, *, tq, tk, n_kc):
    m_rows = GQA_GROUP * tq
    q = q_ref[0, 0].reshape(m_rows, 2 * HEAD_DIM)

    def body(c, acc):
        off = pl.multiple_of(c * tk, tk)
        s = jnp.dot(q, kt_ref[0, 0, :, pl.ds(off, tk)], preferred_element_type=jnp.float32)
        p = jnp.exp2(s).astype(jnp.bfloat16)
        return acc + jnp.dot(p, v_ref[0, 0, pl.ds(off, tk), :], preferred_element_type=jnp.float32)

    acc = lax.fori_loop(0, n_kc, body, jnp.zeros((m_rows, 2 * HEAD_DIM), jnp.float32))
    o = acc[:, :HEAD_DIM] / acc[:, HEAD_DIM:HEAD_DIM + 1]
    o_ref[0] = jnp.concatenate([o[g * tq:(g + 1) * tq] for g in range(GQA_GROUP)], axis=-1)


def _rowmax_kernel(q_ref, kt_ref, m_ref, *, tq, tk, n_kc):
    m_rows = GQA_GROUP * tq
    q = q_ref[0, 0].reshape(m_rows, 2 * HEAD_DIM)

    def body(c, mx):
        off = pl.multiple_of(c * tk, tk)
        s = jnp.dot(q, kt_ref[0, 0, :, pl.ds(off, tk)], preferred_element_type=jnp.float32)
        for j in range(tk // 128):
            mx = jnp.maximum(mx, s[:, j * 128:(j + 1) * 128])
        return mx

    mx = lax.fori_loop(0, n_kc, body, jnp.full((m_rows, 128), -jnp.inf, jnp.float32))
    m_ref[0, 0] = jnp.max(mx, axis=-1, keepdims=True).reshape(GQA_GROUP, tq, 1)


def _attn_call(body, q, kt, v, name, *, tq=ATTN_TQ, tk=ATTN_TK):
    b_, _, _, s_, _ = q.shape
    wide = 2 * HEAD_DIM
    q_spec = pl.BlockSpec((1, 1, GQA_GROUP, tq, wide), lambda b, h, i: (b, h, 0, i, 0))
    kt_spec = pl.BlockSpec((1, 1, wide, s_), lambda b, h, i: (b, h, 0, 0))
    v_spec = pl.BlockSpec((1, 1, s_, wide), lambda b, h, i: (b, h, 0, 0))
    if v is None:
        operands, in_specs = (q, kt), [q_spec, kt_spec]
        o_spec = pl.BlockSpec((1, 1, GQA_GROUP, tq, 1), lambda b, h, i: (b, h, 0, i, 0))
        o_shape = (b_, N_KV_HEADS, GQA_GROUP, s_, 1)
    else:
        operands, in_specs = (q, kt, v), [q_spec, kt_spec, v_spec]
        o_spec = pl.BlockSpec((1, tq, GQA_GROUP * HEAD_DIM), lambda b, h, i: (b, i, h))
        o_shape = (b_, s_, N_HEADS * HEAD_DIM)
    return pl.pallas_call(
        functools.partial(body, tq=tq, tk=tk, n_kc=s_ // tk),
        out_shape=jax.ShapeDtypeStruct(o_shape, jnp.float32),
        grid=(b_, N_KV_HEADS, s_ // tq),
        in_specs=in_specs, out_specs=o_spec,
        compiler_params=pltpu.CompilerParams(
            dimension_semantics=("arbitrary", "arbitrary", "arbitrary"),
            vmem_limit_bytes=ATTN_VMEM_LIMIT_BYTES),
        name=name,
    )(*operands)


def global_mixer(proj, tables, q_norm_g, k_norm_g):
    bf16 = jnp.bfloat16
    hd = HEAD_DIM
    qa, kt, va, kn2 = _global_prep(proj, PROJ_DEST["d_q"][1], PROJ_DEST["d_kv"][1], tables,
                                   q_norm_g, k_norm_g)
    kn = jnp.sqrt(jnp.max(kn2, axis=1)[:, ::hd])
    c = (kn * ATTN_ROUND_UP).astype(bf16)
    qmax = jnp.max(-qa[:, :, :, :, hd].astype(jnp.float32), axis=(2, 3))
    with_key_row = lambda row: kt.at[:, :, hd, :].set(jnp.broadcast_to(row[:, :, None], kt.shape[:2] + kt.shape[3:]))

    def exact_shift():
        m = _attn_call(_rowmax_kernel, qa, kt, None, "global_attention_rowmax")
        return qa.at[:, :, :, :, hd].set((-m[..., 0]).astype(bf16)), with_key_row(jnp.ones_like(c))

    qa, kta = lax.cond(jnp.max(qmax * c.astype(jnp.float32)) < ATTN_SAFE_SHIFT,
                       lambda: (qa, with_key_row(c)), exact_shift)
    return _attn_call(_flash_kernel, qa, kta, va, "global_flash_attention")


def _row_copy(src, src_row, dst, dst_row, sem):
    return pltpu.make_async_copy(src.at[pl.ds(src_row, 1)], dst.at[pl.ds(dst_row, 1)], sem)


def _moe_dispatch_kernel(pos_ref, x_ref, g_ref, zeros_hbm, xs_hbm, xn_scr, sem, *, tb, n_steps):
    del zeros_hbm
    i = pl.program_id(0)
    slot = i % 2

    def wait_slot(s):
        for _ in range(TOP_K):
            pltpu.make_async_copy(xn_scr.at[s], xn_scr.at[s], sem.at[s]).wait()

    @pl.when(i >= 2)
    def _():
        wait_slot(slot)

    x = x_ref[...]
    xn_scr[slot] = x * lax.rsqrt(jnp.mean(x * x, axis=-1, keepdims=True) + EPS) * g_ref[...]

    def issue(r, carry):
        for k in range(TOP_K):
            _row_copy(xn_scr.at[slot], r, xs_hbm, pos_ref[(i * tb + r) * TOP_K + k], sem.at[slot]).start()
        return carry

    lax.fori_loop(0, tb, issue, 0, unroll=8)

    @pl.when(i == n_steps - 1)
    def _():
        if n_steps >= 2:
            wait_slot(1 - slot)
        wait_slot(slot)


def _moe_expert_kernel(blk_e_ref, blk_rows_ref, xs_ref, wg_ref, wu_ref, wd_ref, o_ref, *, tm):
    del blk_e_ref
    rows = blk_rows_ref[pl.program_id(0)]

    @pl.when(rows > 0)
    def _():
        x = xs_ref[...].astype(jnp.bfloat16)
        g = jnp.dot(x, wg_ref[0], preferred_element_type=jnp.float32)
        u = jnp.dot(x, wu_ref[0], preferred_element_type=jnp.float32)
        h = (g * jax.nn.sigmoid(g) * u).astype(jnp.bfloat16)
        o_ref[...] = jnp.dot(h, wd_ref[0], preferred_element_type=jnp.float32)

    @pl.when(rows == 0)
    def _():
        o_ref[...] = jnp.zeros((tm, o_ref.shape[1]), jnp.float32)


def _moe_combine_kernel(pos_ref, x_ref, w_ref, g_ref, os_hbm, y_ref, buf, sem, *, tb, n_steps, out_norm):
    i = pl.program_id(0)
    slot = i % 2

    def fetch(step, s):
        def issue(r, carry):
            for k in range(TOP_K):
                _row_copy(os_hbm, pos_ref[(step * tb + r) * TOP_K + k], buf.at[s, k], r, sem.at[s]).start()
            return carry

        lax.fori_loop(0, tb, issue, 0, unroll=8)

    @pl.when(i == 0)
    def _():
        fetch(0, 0)

    @pl.when(i + 1 < n_steps)
    def _():
        fetch(i + 1, 1 - slot)

    for k in range(TOP_K):
        pltpu.make_async_copy(buf.at[slot, k], buf.at[slot, k], sem.at[slot]).wait()
    w = w_ref[...]
    y = x_ref[...]
    for k in range(TOP_K):
        y = y + w[:, k:k + 1] * buf[slot, k]
    if out_norm:
        y = y * lax.rsqrt(jnp.mean(y * y, axis=-1, keepdims=True) + EPS) * g_ref[...]
    y_ref[...] = y


def _router_kernel(x_ref, g_ref, w_ref, o_ref):
    x = x_ref[...]
    xn = x * lax.rsqrt(jnp.mean(x * x, axis=-1, keepdims=True) + EPS) * g_ref[...]
    o_ref[...] = jnp.dot(xn.astype(jnp.bfloat16), w_ref[...], preferred_element_type=jnp.float32)


def _router_logits(xt, norm_g, w_group, w_expert):
    n_tok, d_ = xt.shape
    n_log = N_EXPERT_GROUPS + N_EXPERTS
    w = jnp.concatenate([w_group, w_expert, jnp.zeros((d_, ROUTER_W - n_log), w_group.dtype)], axis=1)
    return pl.pallas_call(
        _router_kernel, grid=(n_tok // CONV_TM,),
        in_specs=[pl.BlockSpec((CONV_TM, d_), lambda i: (i, 0)), pl.BlockSpec((1, d_), lambda i: (0, 0)),
                  pl.BlockSpec((d_, ROUTER_W), lambda i: (0, 0))],
        out_specs=pl.BlockSpec((CONV_TM, ROUTER_W), lambda i: (i, 0)),
        out_shape=jax.ShapeDtypeStruct((n_tok, ROUTER_W), jnp.float32),
        compiler_params=pltpu.CompilerParams(dimension_semantics=("arbitrary",)),
        name="moe_router",
    )(xt, norm_g.astype(jnp.float32).reshape(1, d_), w.astype(jnp.bfloat16))


def _moe_route(logits, b_group, b_expert, tm):
    n_tok = logits.shape[0]
    gp = jax.nn.softmax(logits[:, :N_EXPERT_GROUPS] + b_group.astype(jnp.float32), axis=-1)
    g_idx = jnp.argmax(gp, axis=-1, keepdims=True)
    g_w = jnp.max(gp, axis=-1, keepdims=True)
    elog = logits[:, N_EXPERT_GROUPS:N_EXPERT_GROUPS + N_EXPERTS] + b_expert.astype(jnp.float32)
    elog = elog.reshape(n_tok, N_EXPERT_GROUPS, EXPERTS_PER_GROUP)
    elog_sel = jnp.take_along_axis(elog, g_idx[:, :, None], axis=1)[:, 0]
    e_w, e_idx = lax.top_k(jax.nn.softmax(elog_sel, axis=-1), TOP_K)
    e_w = e_w / jnp.sum(e_w, axis=-1, keepdims=True)
    weights = g_w * e_w
    experts = (g_idx * EXPERTS_PER_GROUP + e_idx).reshape(-1)
    onehot = (experts[:, None] == jnp.arange(N_EXPERTS)[None, :]).astype(jnp.int32)
    csum = jnp.cumsum(onehot, axis=0)
    counts = csum[-1]
    rank = jnp.sum(onehot * csum, axis=1) - 1
    padded = ((counts + tm - 1) // tm) * tm
    pend = jnp.cumsum(padded)
    pstart = pend - padded
    pos = (pstart[experts] + rank).astype(jnp.int32)
    n_blk = (n_tok * TOP_K) // tm + N_EXPERTS
    blk_start = jnp.arange(n_blk) * tm
    blk_e = jnp.minimum(jnp.sum(pend[None, :] <= blk_start[:, None], axis=1), N_EXPERTS - 1).astype(jnp.int32)
    blk_rows = jnp.clip(counts[blk_e] - (blk_start - pstart[blk_e]), 0, tm).astype(jnp.int32)
    return weights, pos, blk_e, blk_rows, n_blk


def hier_moe_residual(x, norm_g, w_group, b_group, w_expert, b_expert, w_gate, w_up, w_down,
                      out_norm_g=None, *, tm=MOE_TM, tb=MOE_TB):
    b_, s_, d_ = x.shape
    n_tok = b_ * s_
    xt = x.reshape(n_tok, d_)
    weights, pos, blk_e, blk_rows, n_blk = _moe_route(_router_logits(xt, norm_g, w_group, w_expert),
                                                      b_group, b_expert, tm)
    n_pad = n_blk * tm
    vmem = pltpu.CompilerParams(dimension_semantics=("arbitrary",),
                                vmem_limit_bytes=MOE_VMEM_LIMIT_BYTES)
    xs = pl.pallas_call(
        functools.partial(_moe_dispatch_kernel, tb=tb, n_steps=n_tok // tb),
        grid_spec=pltpu.PrefetchScalarGridSpec(
            num_scalar_prefetch=1, grid=(n_tok // tb,),
            in_specs=[pl.BlockSpec((tb, d_), lambda i, pos: (i, 0)),
                      pl.BlockSpec((1, d_), lambda i, pos: (0, 0)),
                      pl.BlockSpec(memory_space=pl.ANY)],
            out_specs=pl.BlockSpec(memory_space=pl.ANY),
            scratch_shapes=[pltpu.VMEM((2, tb, d_), jnp.float32), pltpu.SemaphoreType.DMA((2,))]),
        out_shape=jax.ShapeDtypeStruct((n_pad, d_), jnp.float32),
        input_output_aliases={3: 0},
        compiler_params=vmem, name="moe_dispatch",
    )(pos, xt, norm_g.astype(jnp.float32).reshape(1, d_), jnp.zeros((n_pad, d_), jnp.float32))
    bf16 = jnp.bfloat16
    outs = pl.pallas_call(
        functools.partial(_moe_expert_kernel, tm=tm),
        grid_spec=pltpu.PrefetchScalarGridSpec(
            num_scalar_prefetch=2, grid=(n_blk,),
            in_specs=[pl.BlockSpec((tm, d_), lambda i, be, br: (i, 0)),
                      pl.BlockSpec((1, d_, D_EXPERT), lambda i, be, br: (be[i], 0, 0)),
                      pl.BlockSpec((1, d_, D_EXPERT), lambda i, be, br: (be[i], 0, 0)),
                      pl.BlockSpec((1, D_EXPERT, d_), lambda i, be, br: (be[i], 0, 0))],
            out_specs=pl.BlockSpec((tm, d_), lambda i, be, br: (i, 0))),
        out_shape=jax.ShapeDtypeStruct((n_pad, d_), jnp.float32),
        compiler_params=vmem, name="moe_experts",
    )(blk_e, blk_rows, xs, w_gate.astype(bf16), w_up.astype(bf16), w_down.astype(bf16))
    y = pl.pallas_call(
        functools.partial(_moe_combine_kernel, tb=tb, n_steps=n_tok // tb, out_norm=out_norm_g is not None),
        grid_spec=pltpu.PrefetchScalarGridSpec(
            num_scalar_prefetch=1, grid=(n_tok // tb,),
            in_specs=[pl.BlockSpec((tb, d_), lambda i, pos: (i, 0)),
                      pl.BlockSpec((tb, TOP_K), lambda i, pos: (i, 0)),
                      pl.BlockSpec((1, d_), lambda i, pos: (0, 0)),
                      pl.BlockSpec(memory_space=pl.ANY)],
            out_specs=pl.BlockSpec((tb, d_), lambda i, pos: (i, 0)),
            scratch_shapes=[pltpu.VMEM((2, TOP_K, tb, d_), jnp.float32), pltpu.SemaphoreType.DMA((2,))]),
        out_shape=jax.ShapeDtypeStruct((n_tok, d_), jnp.float32),
        compiler_params=vmem, name="moe_combine",
    )(pos, xt, weights, (norm_g if out_norm_g is None else out_norm_g).astype(jnp.float32).reshape(1, d_), outs)
    return y.reshape(b_, s_, d_)


PROJ_DEST = {"a_qkv": (0, 0), "a_z": (1, 1536), "b_in": (4, 2048), "c_q": (5, 3584), "d_q": (7, 4096),
             "c_kv": (6, 4608), "d_kv": (8, 4864), "a_beta": (2, 5120), "a_alpha": (3, 5136)}
PROJ_WIDTH = 5632
assert PROJ_DEST["a_alpha"][1] == PROJ_DEST["a_beta"][1] + 2 * N_HEADS


def _in_proj_weight(w):
    starts = [0] + _split_points()
    cols = jnp.zeros((w.shape[0], PROJ_WIDTH), jnp.bfloat16)
    for seg, dest in PROJ_DEST.values():
        cols = lax.dynamic_update_slice(
            cols, w[:, starts[seg]:starts[seg] + IN_SPLIT_SIZES[seg]].astype(jnp.bfloat16), (0, dest))
    return cols


def _in_proj_kernel(x_ref, g_ref, w_ref, o_ref, xn_scr):
    @pl.when(pl.program_id(1) == 0)
    def _():
        x = x_ref[...]
        xn = x * lax.rsqrt(jnp.mean(x * x, axis=-1, keepdims=True) + EPS) * g_ref[...]
        xn_scr[...] = xn.astype(jnp.bfloat16)

    o_ref[...] = jnp.dot(xn_scr[...], w_ref[...], preferred_element_type=jnp.float32)


def in_proj(x, norm_g, w):
    n_tok, d_ = x.shape
    return pl.pallas_call(
        _in_proj_kernel,
        grid=(n_tok // PROJ_TM, PROJ_WIDTH // PROJ_TILE),
        in_specs=[pl.BlockSpec((PROJ_TM, d_), lambda i, j: (i, 0)),
                  pl.BlockSpec((1, d_), lambda i, j: (0, 0)),
                  pl.BlockSpec((d_, PROJ_TILE), lambda i, j: (0, j))],
        out_specs=pl.BlockSpec((PROJ_TM, PROJ_TILE), lambda i, j: (i, j)),
        out_shape=jax.ShapeDtypeStruct((n_tok, PROJ_WIDTH), jnp.float32),
        scratch_shapes=[pltpu.VMEM((PROJ_TM, d_), jnp.bfloat16)],
        compiler_params=pltpu.CompilerParams(dimension_semantics=("arbitrary", "arbitrary"),
                                             vmem_limit_bytes=PROJ_VMEM_LIMIT_BYTES),
        name="in_proj",
    )(x, norm_g.astype(jnp.float32).reshape(1, d_), _in_proj_weight(w))


def _out_proj_kernel(x_ref, *refs, normed):
    n = len(normed)
    y_refs, g_ref, w_ref, o_ref = refs[:n], refs[n], refs[n + 1], refs[n + 2]
    acc = x_ref[...]
    for k, y_ref in enumerate(y_refs):
        y = y_ref[...]
        if normed[k]:
            y = y * lax.rsqrt(jnp.mean(y * y, axis=-1, keepdims=True) + EPS) * g_ref[k:k + 1, :]
        acc = acc + jnp.dot(y.astype(jnp.bfloat16), w_ref[k * GROUP_W:(k + 1) * GROUP_W, :],
                            preferred_element_type=jnp.float32)
    o_ref[...] = acc


def out_proj_residual(x, ys, gains, w):
    n_tok, d_ = x.shape
    row = lambda width: pl.BlockSpec((OUT_TM, width), lambda i: (i, 0))
    g = jnp.stack([jnp.ones((GROUP_W,), jnp.float32) if gk is None else gk.astype(jnp.float32)
                   for gk in gains])
    return pl.pallas_call(
        functools.partial(_out_proj_kernel, normed=tuple(gk is not None for gk in gains)),
        grid=(n_tok // OUT_TM,),
        in_specs=[row(d_)] + [row(GROUP_W)] * len(ys) + [pl.BlockSpec(g.shape, lambda i: (0, 0)),
                                                          pl.BlockSpec(w.shape, lambda i: (0, 0))],
        out_specs=row(d_),
        out_shape=jax.ShapeDtypeStruct((n_tok, d_), jnp.float32),
        compiler_params=pltpu.CompilerParams(dimension_semantics=("arbitrary",),
                                             vmem_limit_bytes=PROJ_VMEM_LIMIT_BYTES),
        name="out_proj",
    )(x, *ys, g, w.astype(jnp.bfloat16))


def kernel(x, norm_mix, w_in, gdn_conv, gdn_a_log, gdn_dt_bias, gdn_norm, hy_conv, hy_w1, hy_b1, hy_freq1, hy_w2, hy_b2, hy_freq2, hy_w3, hy_deltas, hy_bias, hy_norm, swa_sink, swa_norm, ga_q_norm, ga_k_norm, ga_norm, w_out, norm_ffn, moe_w_group, moe_b_group, moe_w_expert, moe_b_expert, moe_w_gate, moe_w_up, moe_w_down, norm_final):
    b_, s_, _ = x.shape
    rows = s_ // GRID_W
    row_idx = jnp.repeat(jnp.arange(rows), GRID_W)
    col_idx = jnp.tile(jnp.arange(GRID_W), rows)
    rope_tables = _rope_tables(row_idx, col_idx, N_HEADS)
    pos_feat = hyena_pos_features(s_)
    slopes = alibi_slopes(N_HEADS)
    n_tok = b_ * s_
    for l in range(DEPTH):
        proj = in_proj(x.reshape(n_tok, D_MODEL), norm_mix[l], w_in[l]).reshape(b_, s_, PROJ_WIDTH)
        y_a = gdn_mixer(proj, gdn_conv[l], gdn_a_log[l], gdn_dt_bias[l], gdn_norm[l])
        kf = hyena_filters_f(pos_feat, hy_w1[l], hy_b1[l], hy_freq1[l], hy_w2[l], hy_b2[l],
                             hy_freq2[l], hy_w3[l], hy_deltas[l])
        y_b = hyena_mixer(proj, hy_conv[l], kf, hy_bias[l])
        y_c = window_mixer(proj, PROJ_DEST["c_q"][1], PROJ_DEST["c_kv"][1], swa_sink[l], slopes, swa_norm[l])
        y_d = global_mixer(proj, rope_tables, ga_q_norm[l], ga_k_norm[l])
        x = out_proj_residual(x.reshape(n_tok, D_MODEL),
                              [y.reshape(n_tok, GROUP_W) for y in (y_a, y_b, y_c, y_d)],
                              [None, hy_norm[l], None, ga_norm[l]], w_out[l]).reshape(b_, s_, D_MODEL)
        x = hier_moe_residual(x, norm_ffn[l], moe_w_group[l], moe_b_group[l], moe_w_expert[l],
                              moe_b_expert[l], moe_w_gate[l], moe_w_up[l], moe_w_down[l],
                              norm_final if l == DEPTH - 1 else None)
    return x
```

```python
import functools
import math

import jax
import jax.numpy as jnp
import numpy as np
from jax import lax
from jax.experimental import pallas as pl
from jax.experimental.pallas import tpu as pltpu

D_MODEL = 2048
DEPTH = 2
N_MIXERS = 4
GROUP_W = D_MODEL // N_MIXERS
HEAD_DIM = 64
N_HEADS = GROUP_W // HEAD_DIM
N_KV_HEADS = 2
GQA_GROUP = N_HEADS // N_KV_HEADS
KV_W = N_KV_HEADS * HEAD_DIM
SHORT_CONV = 3
GDN_CHUNK = 64
HY_ORDER = 2
HY_EMB = 33
HY_BANDS = (HY_EMB - 1) // 2
WINDOW = 128
BLOCK = 128
GRID_W = 64
ROPE_THETA = 10000.0
N_EXPERT_GROUPS = 4
EXPERTS_PER_GROUP = 8
N_EXPERTS = N_EXPERT_GROUPS * EXPERTS_PER_GROUP
TOP_K = 2
D_EXPERT = 512
MOE_BLOCK = 128
EPS = 1e-6
IN_SPLIT_SIZES = (3 * GROUP_W, GROUP_W, 2 * N_HEADS, 2 * N_HEADS, 3 * GROUP_W,
                  GROUP_W, 2 * KV_W, GROUP_W, 2 * KV_W)

ATTN_VMEM_LIMIT_BYTES = 48 * 1024 * 1024
ATTN_TQ = 256
ATTN_TK = 2048
WINDOW_SUB = 4
WINDOW_MASK = -1e30
GDN_VMEM_LIMIT_BYTES = 48 * 1024 * 1024
PROJ_VMEM_LIMIT_BYTES = 48 * 1024 * 1024
CONV_TM = 512
CONV_HALO = 8
PROJ_TM = 1024
PROJ_TILE = 512
OUT_TM = 256
GDN_SB = 256
GDN_INV_BASE = 8
GDN_KCHUNK = 64
GDN_HG = 4
FFT_VMEM_LIMIT_BYTES = 48 * 1024 * 1024
FFT_N2 = 256
FFT_COL_TILE = 4096
MOE_VMEM_LIMIT_BYTES = 48 * 1024 * 1024
MOE_TM = 512
MOE_TB = 256
ROUTER_W = 128
LOG2E = 1.4426950408889634
ATTN_SAFE_SHIFT = 50.0
ATTN_ROUND_UP = 1.01


def _split_points():
    return [int(v) for v in np.cumsum(IN_SPLIT_SIZES)[:-1]]


def _group_ones(width):
    r = np.arange(width) // HEAD_DIM
    return jnp.asarray(r[:, None] == r[None, :], jnp.bfloat16)


def _group_sum(x, bd):
    hi = x.astype(jnp.bfloat16)
    lo = (x - hi.astype(jnp.float32)).astype(jnp.bfloat16)
    return (jnp.dot(hi, bd, preferred_element_type=jnp.float32)
            + jnp.dot(lo, bd, preferred_element_type=jnp.float32))


def _conv_prep_kernel(*refs, modes, n_steps):
    n = len(modes)
    w_ref, bd_ref = refs[3 * n], refs[3 * n + 1]
    o_refs = refs[3 * n + 2:]
    i = pl.program_id(1)
    tm = refs[0].shape[1]
    row = lax.broadcasted_iota(jnp.int32, (tm, 1), 0)
    for t, mode in enumerate(modes):
        x_ref, prev_ref, next_ref = refs[3 * t:3 * t + 3]
        x = x_ref[0]
        prev = jnp.where(i > 0, prev_ref[0][CONV_HALO - 1:CONV_HALO], 0.0)
        nxt = jnp.where(i < n_steps - 1, next_ref[0][0:1], 0.0)
        x_prev = jnp.where(row == 0, prev, pltpu.roll(x, 1, 0))
        x_next = jnp.where(row == tm - 1, nxt, pltpu.roll(x, tm - 1, 0))
        w = w_ref[:, t * GROUP_W:(t + 1) * GROUP_W]
        y = x_prev * w[0:1] + x * w[1:2] + x_next * w[2:3]
        if mode != "plain":
            y = y * jax.nn.sigmoid(y)
        if mode.startswith("silu_l2"):
            y = y * lax.rsqrt(_group_sum(y * y, bd_ref[...]) + EPS)
        if mode == "silu_l2_scaled":
            y = y * (HEAD_DIM ** -0.5)
        o_refs[t][0] = y


def _conv_prep(proj, col, conv_w, modes):
    b_, s_, _ = proj.shape
    assert col % GROUP_W == 0
    tm = CONV_TM
    n_steps = s_ // tm
    hb = tm // CONV_HALO
    specs = []
    for t in range(len(modes)):
        c = col // GROUP_W + t
        specs += [pl.BlockSpec((1, tm, GROUP_W), lambda b, i, c=c: (b, i, c)),
                  pl.BlockSpec((1, CONV_HALO, GROUP_W), lambda b, i, c=c: (b, jnp.maximum(i * hb - 1, 0), c)),
                  pl.BlockSpec((1, CONV_HALO, GROUP_W),
                               lambda b, i, c=c: (b, jnp.minimum((i + 1) * hb, s_ // CONV_HALO - 1), c))]
    bd = _group_ones(GROUP_W)
    w = conv_w.astype(jnp.float32)
    specs += [pl.BlockSpec(w.shape, lambda b, i: (0, 0)), pl.BlockSpec(bd.shape, lambda b, i: (0, 0))]
    out = jax.ShapeDtypeStruct((b_, s_, GROUP_W), jnp.float32)
    return pl.pallas_call(
        functools.partial(_conv_prep_kernel, modes=modes, n_steps=n_steps),
        grid=(b_, n_steps), in_specs=specs,
        out_specs=[pl.BlockSpec((1, tm, GROUP_W), lambda b, i: (b, i, 0))] * len(modes),
        out_shape=[out] * len(modes),
        compiler_params=pltpu.CompilerParams(dimension_semantics=("arbitrary", "arbitrary")),
        name="conv_prep",
    )(*([proj] * (3 * len(modes))), w, bd)


def _gdn_post_kernel(of_ref, ob_ref, z_ref, g_ref, bd_ref, y_ref):
    o = of_ref[0] + ob_ref[0]
    z = z_ref[0]
    ms = _group_sum(o * o, bd_ref[...]) * (1.0 / HEAD_DIM)
    y_ref[0] = o * lax.rsqrt(ms + EPS) * g_ref[...] * (z * jax.nn.sigmoid(z))


def _gdn_post(o_fwd, o_bwd, proj, z_col, norm_g):
    b_, s_, w_ = o_fwd.shape
    tm = CONV_TM
    tok = pl.BlockSpec((1, tm, w_), lambda b, i: (b, i, 0))
    bd = _group_ones(w_)
    g = jnp.tile(norm_g.astype(jnp.float32), N_HEADS).reshape(1, w_)
    return pl.pallas_call(
        _gdn_post_kernel, grid=(b_, s_ // tm),
        in_specs=[tok, tok, pl.BlockSpec((1, tm, w_), lambda b, i: (b, i, z_col // w_)),
                  pl.BlockSpec(g.shape, lambda b, i: (0, 0)), pl.BlockSpec(bd.shape, lambda b, i: (0, 0))],
        out_specs=tok, out_shape=jax.ShapeDtypeStruct((b_, s_, w_), jnp.float32),
        compiler_params=pltpu.CompilerParams(dimension_semantics=("arbitrary", "arbitrary")),
        name="gdn_post",
    )(o_fwd, o_bwd, proj, g, bd)


def _gdn_masks(rev):
    r = np.arange(GDN_SB)
    i, j = r[:, None], r[None, :]
    same = lambda s: (i // s) == (j // s)
    before = (i < j) if rev else (i > j)
    chunk = same(GDN_KCHUNK)
    masks = [chunk & (before | (i == j)),
             chunk,
             chunk & before,
             i == j,
             same(GDN_INV_BASE),
             same(HEAD_DIM)]
    s = GDN_INV_BASE
    while s < GDN_KCHUNK:
        masks.append(same(2 * s) & ~same(s))
        s *= 2
    return np.stack(masks).astype(np.float32)


def _mm_bf16(a, b):
    return jnp.dot(a.astype(jnp.bfloat16), b.astype(jnp.bfloat16), preferred_element_type=jnp.float32)


def _unit_triangular_inverses(a_list, eye, m_base, m_offs):
    n_list = [-a * m_base for a in a_list]
    t_list = [eye + n for n in n_list]
    power = 2
    while power < GDN_INV_BASE:
        n_list = [_mm_bf16(n, n) for n in n_list]
        t_list = [t + _mm_bf16(t, n) for t, n in zip(t_list, n_list)]
        power *= 2
    for m_off in m_offs:
        u_list = [_mm_bf16(a * m_off, t) for a, t in zip(a_list, t_list)]
        t_list = [t - _mm_bf16(t, u) for t, u in zip(t_list, u_list)]
    return t_list


def _gdn_kernel(mask_ref, expand_ref, q_ref, k_ref, v_ref, beta_ref, g_ref, gt_ref, o_ref,
                state, vstack, val_s, kcd_s, qd_s, kd_s, attn_s, *, rev):
    bf16 = jnp.bfloat16
    f32 = jnp.float32
    hd = HEAD_DIM
    gw = GDN_HG * hd
    n_groups = N_HEADS // GDN_HG
    rr = GDN_SB
    nt = (((1,), (1,)), ((), ()))
    tn = (((0,), (0,)), ((), ()))

    @pl.when(pl.program_id(1) == 0)
    def _():
        state[...] = jnp.zeros(state.shape, state.dtype)
        vstack[...] = jnp.zeros(vstack.shape, vstack.dtype)

    incl, ones, strict, eye, m_base, head_blk = (mask_ref[t] for t in range(6))
    m_offs = [mask_ref[t] for t in range(6, mask_ref.shape[0])]
    g = g_ref[0, 0]
    gc = _dot_f32(incl, g)
    gl = _dot_f32(ones, g)
    gct = lax.dot_general(gt_ref[0, 0], incl, nt, precision=lax.Precision.HIGHEST,
                          preferred_element_type=jnp.float32)
    expand = expand_ref[...]
    beta_w = _dot_f32(beta_ref[0, 0], expand)
    eg_w = jnp.exp(_dot_f32(gc, expand))
    ekd_w = jnp.exp(_dot_f32(gl - gc, expand))
    cd_w = jnp.exp(_dot_f32(gl, expand))
    lane = lax.broadcasted_iota(jnp.int32, (1, gw), 1)
    head_mask = [(lane // hd == j).astype(f32) for j in range(GDN_HG)]
    a_list = []
    for grp in range(n_groups):
        sl = slice(grp * gw, (grp + 1) * gw)
        q, k = q_ref[0][:, sl], k_ref[0][:, sl]
        kb = k * beta_w[:, sl]
        kbf = k.astype(bf16)
        qd_s[grp] = (q * eg_w[:, sl]).astype(bf16)
        kd_s[grp] = (k * ekd_w[:, sl]).astype(bf16)
        for j in range(GDN_HG):
            h = grp * GDN_HG + j
            dec = jnp.exp(jnp.minimum(gc[:, h:h + 1] - gct[h:h + 1, :], 0.0)) * incl
            a_list.append(lax.dot_general((kb * head_mask[j]).astype(bf16), kbf, nt,
                                          preferred_element_type=f32) * dec * strict)
            attn = lax.dot_general((q * head_mask[j]).astype(bf16), kbf, nt, preferred_element_type=f32) * dec
            attn_s[h] = attn.astype(bf16)
    t_list = _unit_triangular_inverses(a_list, eye, m_base, m_offs)
    for grp in range(n_groups):
        sl = slice(grp * gw, (grp + 1) * gw)
        vb = v_ref[0][:, sl] * beta_w[:, sl]
        kbg = k_ref[0][:, sl] * beta_w[:, sl] * eg_w[:, sl]
        val = jnp.zeros((rr, gw), f32)
        kcd = jnp.zeros((rr, gw), f32)
        for j in range(GDN_HG):
            t = t_list[grp * GDN_HG + j]
            val = val + _mm_bf16(t, vb * head_mask[j])
            kcd = kcd + _mm_bf16(t, kbg * head_mask[j])
        val_s[grp] = val
        kcd_s[grp] = kcd.astype(bf16)
    n_chunks = GDN_SB // GDN_KCHUNK
    for c in (reversed(range(n_chunks)) if rev else range(n_chunks)):
        rows = slice(c * GDN_KCHUNK, (c + 1) * GDN_KCHUNK)
        for grp in range(n_groups):
            s_old = state[grp]
            s_bf = s_old.astype(bf16)
            both = jnp.dot(jnp.concatenate([kcd_s[grp, rows, :], qd_s[grp, rows, :]], axis=0), s_bf,
                           preferred_element_type=f32)
            v_new = val_s[grp, rows, :] - both[:GDN_KCHUNK]
            for j in range(GDN_HG):
                vstack[grp, j * rr + c * GDN_KCHUNK:j * rr + (c + 1) * GDN_KCHUNK, :] = (
                    v_new * head_mask[j]).astype(bf16)
            attn_cat = jnp.concatenate([attn_s[grp * GDN_HG + j, rows, :] for j in range(GDN_HG)], axis=1)
            o = both[GDN_KCHUNK:] + jnp.dot(attn_cat, vstack[grp], preferred_element_type=f32)
            upd = lax.dot_general(kd_s[grp, rows, :], v_new.astype(bf16), tn, preferred_element_type=f32)
            state[grp] = s_old * cd_w[c * GDN_KCHUNK:c * GDN_KCHUNK + 1, grp * gw:(grp + 1) * gw] + upd * head_blk
            o_ref[0, rows, grp * gw:(grp + 1) * gw] = o


def _gdn_scan(q, k, v, beta, g, rev):
    b_, s_, w_ = q.shape
    assert GDN_SB == GDN_HG * HEAD_DIM and GDN_SB % GDN_KCHUNK == 0 and GDN_KCHUNK % GDN_CHUNK == 0
    n_sb = s_ // GDN_SB
    n_groups = N_HEADS // GDN_HG
    gw = GDN_HG * HEAD_DIM
    masks = jnp.asarray(_gdn_masks(rev))
    expand = jnp.asarray(np.arange(N_HEADS)[:, None] == np.arange(w_)[None, :] // HEAD_DIM, jnp.float32)
    step = (lambda i: n_sb - 1 - i) if rev else (lambda i: i)
    tok = pl.BlockSpec((1, GDN_SB, w_), lambda b, i: (b, step(i), 0))
    gate = pl.BlockSpec((1, 1, GDN_SB, N_HEADS), lambda b, i: (b, 0, step(i), 0))
    gate_t = pl.BlockSpec((1, 1, N_HEADS, GDN_SB), lambda b, i: (b, 0, 0, step(i)))
    per_group = lambda dt: pltpu.VMEM((n_groups, GDN_SB, gw), dt)
    return pl.pallas_call(
        functools.partial(_gdn_kernel, rev=rev),
        grid=(b_, n_sb),
        in_specs=[pl.BlockSpec(masks.shape, lambda b, i: (0, 0, 0)),
                  pl.BlockSpec(expand.shape, lambda b, i: (0, 0)), tok, tok, tok, gate, gate, gate_t],
        out_specs=tok,
        out_shape=jax.ShapeDtypeStruct((b_, s_, w_), jnp.float32),
        scratch_shapes=[pltpu.VMEM((n_groups, gw, gw), jnp.float32),
                        pltpu.VMEM((n_groups, GDN_HG * GDN_SB, gw), jnp.bfloat16),
                        per_group(jnp.float32), per_group(jnp.bfloat16), per_group(jnp.bfloat16),
                        per_group(jnp.bfloat16),
                        pltpu.VMEM((N_HEADS, GDN_SB, GDN_SB), jnp.bfloat16)],
        compiler_params=pltpu.CompilerParams(dimension_semantics=("arbitrary", "arbitrary"),
                                             vmem_limit_bytes=GDN_VMEM_LIMIT_BYTES),
        name="gdn_scan_rev" if rev else "gdn_scan_fwd",
    )(masks, expand, q, k, v, beta, g, g.transpose(0, 1, 3, 2))


def gdn_mixer(proj, conv_w, a_log, dt_bias, norm_g):
    b_, s_, _ = proj.shape
    q, k, v = _conv_prep(proj, PROJ_DEST["a_qkv"][1], conv_w, ("silu_l2_scaled", "silu_l2", "silu"))
    gates = proj[:, :, PROJ_DEST["a_beta"][1]:PROJ_DEST["a_beta"][1] + 4 * N_HEADS]
    b_in = gates[:, :, :2 * N_HEADS].reshape(b_, s_, 2, N_HEADS)
    a_in = gates[:, :, 2 * N_HEADS:].reshape(b_, s_, 2, N_HEADS)
    beta = jax.nn.sigmoid(b_in).transpose(0, 2, 1, 3)
    g = -jnp.exp(a_log.astype(jnp.float32)) * jax.nn.softplus(a_in + dt_bias.astype(jnp.float32))
    g = g.transpose(0, 2, 1, 3)
    o_fwd = _gdn_scan(q, k, v, beta[:, 0:1], g[:, 0:1], False)
    o_bwd = _gdn_scan(q, k, v, beta[:, 1:2], g[:, 1:2], True)
    return _gdn_post(o_fwd, o_bwd, proj, PROJ_DEST["a_z"][1], norm_g)


def hyena_pos_features(length):
    t = jnp.linspace(0.0, 1.0, length, dtype=jnp.float32)[:, None]
    w = 2.0 * math.pi * jnp.arange(length, dtype=jnp.float32) / length
    f = jnp.linspace(1e-4, HY_BANDS - 1, HY_BANDS, dtype=jnp.float32)
    fw = w[:, None] * f[None, :]
    return jnp.concatenate([t, jnp.cos(fw), -jnp.sin(fw)], axis=-1)


def hyena_filters_f(z, w1, b1, f1, w2, b2, f2, w3, deltas):
    f32 = jnp.float32
    length = z.shape[0]
    t = z[:, :1]
    h = jnp.sin(f1.astype(f32) * (z @ w1.astype(f32) + b1.astype(f32)))
    h = jnp.sin(f2.astype(f32) * (h @ w2.astype(f32) + b2.astype(f32)))
    h = (h @ w3.astype(f32)) * jnp.exp(-t * jnp.abs(deltas.astype(f32)))
    side = lambda d: jnp.concatenate(
        [h[:, (2 * o + d) * GROUP_W:(2 * o + d + 1) * GROUP_W] for o in range(HY_ORDER)], axis=1)
    kern = jnp.concatenate([side(0), jnp.zeros((1, HY_ORDER * GROUP_W), f32), side(1)[:0:-1]], axis=0)
    return kern / (jnp.sum(jnp.abs(kern), axis=0, keepdims=True) + EPS)


def _dft_tables(n_fft):
    n1 = n_fft // FFT_N2
    def dft(n):
        kk = (np.arange(n)[:, None] * np.arange(n)[None, :]) % n
        ang = -2.0 * np.pi * kk / n
        return np.cos(ang), np.sin(ang)
    f1r, f1i = dft(n1)
    f2r, f2i = dft(FFT_N2)
    kk = (np.arange(n1)[:, None] * np.arange(FFT_N2)[None, :]) % n_fft
    tw = -2.0 * np.pi * kk / n_fft
    f32 = np.float32
    return dict(
        f1=np.concatenate([f1r, f1i], axis=0).astype(f32),
        f1_inv=(np.concatenate([f1r[:n1 // 2], f1i[:n1 // 2]], axis=0) / n_fft).astype(f32),
        f2=np.concatenate([f2r, f2i], axis=0).astype(f32),
        twr=np.cos(tw).astype(f32)[:, :, None], twi=np.sin(tw).astype(f32)[:, :, None])


def _dot_f32(a, b):
    return jnp.dot(a, b, precision=lax.Precision.HIGHEST, preferred_element_type=jnp.float32)


def _split_lhs(f):
    f = jnp.asarray(f, jnp.float32)
    hi = f.astype(jnp.bfloat16)
    lo = (f - hi.astype(jnp.float32)).astype(jnp.bfloat16)
    return jnp.concatenate([hi, hi, lo], axis=1)


def _dot_split(f3, x):
    hi = x.astype(jnp.bfloat16)
    lo = (x - hi.astype(jnp.float32)).astype(jnp.bfloat16)
    return jnp.dot(f3, jnp.concatenate([hi, lo, hi], axis=0), preferred_element_type=jnp.float32)


def _fft_stage1_kernel(f_ref, zr_ref, zi_ref, yr_ref, yi_ref, *, n1):
    f = f_ref[...]
    p = _dot_split(f, zr_ref[0])
    if zi_ref is None:
        yr_ref[...] = p[:n1]
        yi_ref[...] = p[n1:]
    else:
        q = _dot_split(f, zi_ref[0])
        yr_ref[...] = p[:n1] - q[n1:]
        yi_ref[...] = q[:n1] + p[n1:]


def _fft_stage1_real_kernel(f_ref, zr_ref, yr_ref, yi_ref, *, n1):
    _fft_stage1_kernel(f_ref, zr_ref, None, yr_ref, yi_ref, n1=n1)


def _fft_mid_kernel(f_ref, twr_ref, twi_ref, yr_ref, yi_ref, kr_ref, ki_ref, qr_ref, qi_ref):
    n2 = FFT_N2
    twr, twi = twr_ref[0], twi_ref[0]
    yr, yi = yr_ref[0], yi_ref[0]
    f = f_ref[...]
    p = _dot_split(f, yr * twr - yi * twi)
    q = _dot_split(f, yr * twi + yi * twr)
    xr = p[:n2] - q[n2:]
    xi = q[:n2] + p[n2:]
    if kr_ref is None:
        qr_ref[0] = xr
        qi_ref[0] = xi
        return
    kr, ki = kr_ref[0], ki_ref[0]
    p = _dot_split(f, xr * kr - xi * ki)
    q = _dot_split(f, xr * ki + xi * kr)
    wr = p[:n2] + q[n2:]
    wi = q[:n2] - p[n2:]
    qr_ref[0] = wr * twr + wi * twi
    qi_ref[0] = wi * twr - wr * twi


def _fft_mid_spectrum_kernel(f_ref, twr_ref, twi_ref, yr_ref, yi_ref, qr_ref, qi_ref):
    _fft_mid_kernel(f_ref, twr_ref, twi_ref, yr_ref, yi_ref, None, None, qr_ref, qi_ref)


def _fft_last_kernel(f_ref, qr_ref, qi_ref, u_ref, gate_ref, bias_ref, o_ref, *, nh):
    f = f_ref[...]
    p = _dot_split(f, qr_ref[...])
    q = _dot_split(f, qi_ref[...])
    bias = bias_ref[...]
    o_ref[0] = gate_ref[0] * (p[:nh] + q[nh:] + u_ref[0] * bias)
    o_ref[1] = gate_ref[1] * (q[:nh] - p[nh:] + u_ref[1] * bias)


def _fft_params(n_axes):
    return pltpu.CompilerParams(dimension_semantics=("arbitrary",) * n_axes,
                                vmem_limit_bytes=FFT_VMEM_LIMIT_BYTES)


def _fft_forward(tab, z, n_ch):
    n1 = tab["f1"].shape[1]
    parts, rows, cols = z.shape
    tn = min(FFT_COL_TILE, cols)
    f1 = _split_lhs(tab["f1"][:, :rows])
    y_shape = jax.ShapeDtypeStruct((n1, cols), jnp.float32)
    col_spec = pl.BlockSpec((n1, tn), lambda j: (0, j))
    z_specs = [pl.BlockSpec((1, rows, tn), lambda j, p=p: (p, 0, j)) for p in range(parts)]
    body = _fft_stage1_kernel if parts == 2 else _fft_stage1_real_kernel
    yr, yi = pl.pallas_call(
        functools.partial(body, n1=n1), grid=(cols // tn,),
        in_specs=[pl.BlockSpec(f1.shape, lambda j: (0, 0))] + z_specs,
        out_specs=[col_spec, col_spec], out_shape=[y_shape, y_shape],
        compiler_params=_fft_params(1), name="fft_stage1",
    )(f1, *([z] * parts))
    return yr.reshape(n1, FFT_N2, n_ch), yi.reshape(n1, FFT_N2, n_ch)


def _fft_mid(tab, yr, yi, kr=None, ki=None, k_block=0):
    n1, n2, n_ch = yr.shape
    slab = pl.BlockSpec((1, n2, n_ch), lambda i: (i, 0, 0))
    tw_spec = pl.BlockSpec((1, n2, 1), lambda i: (i, 0, 0))
    f2 = _split_lhs(tab["f2"])
    ops = [f2, jnp.asarray(tab["twr"]), jnp.asarray(tab["twi"]), yr, yi]
    specs = [pl.BlockSpec(f2.shape, lambda i: (0, 0)), tw_spec, tw_spec, slab, slab]
    body = _fft_mid_spectrum_kernel
    if kr is not None:
        ops += [kr, ki]
        specs += [pl.BlockSpec((1, n2, n_ch), lambda i: (i, 0, k_block))] * 2
        body = _fft_mid_kernel
    shape = jax.ShapeDtypeStruct((n1, n2, n_ch), jnp.float32)
    return pl.pallas_call(
        body, grid=(n1,), in_specs=specs, out_specs=[slab, slab], out_shape=[shape, shape],
        compiler_params=_fft_params(1), name="fft_mid",
    )(*ops)


def _fft_conv_gate(tab, u, gate, bias, kr, ki, k_block):
    b_, length, n_ch = u.shape
    assert b_ == 2
    n1 = tab["f1"].shape[1]
    nh = n1 // 2
    cols = FFT_N2 * n_ch
    uv = u.reshape(b_, nh, cols)
    yr, yi = _fft_forward(tab, uv, n_ch)
    qr, qi = _fft_mid(tab, yr, yi, kr, ki, k_block)
    tn = min(FFT_COL_TILE, cols)
    f1_inv = _split_lhs(tab["f1_inv"])
    q_spec = pl.BlockSpec((n1, tn), lambda j: (0, j))
    u_spec = pl.BlockSpec((b_, nh, tn), lambda j: (0, 0, j))
    out = pl.pallas_call(
        functools.partial(_fft_last_kernel, nh=nh), grid=(cols // tn,),
        in_specs=[pl.BlockSpec(f1_inv.shape, lambda j: (0, 0)), q_spec, q_spec, u_spec, u_spec,
                  pl.BlockSpec((1, tn), lambda j: (0, 0))],
        out_specs=u_spec, out_shape=jax.ShapeDtypeStruct((b_, nh, cols), jnp.float32),
        compiler_params=_fft_params(1), name="fft_last",
    )(f1_inv, qr.reshape(n1, cols), qi.reshape(n1, cols), uv, gate.reshape(b_, nh, cols),
      jnp.tile(bias.astype(jnp.float32), tn // n_ch).reshape(1, tn))
    return out.reshape(b_, length, n_ch)


def hyena_mixer(proj, conv_w, kern, bias):
    dtype = proj.dtype
    length = proj.shape[1]
    x1, x2, v = _conv_prep(proj, PROJ_DEST["b_in"][1], conv_w, ("plain",) * 3)
    tab = _dft_tables(2 * length)
    n1 = tab["f1"].shape[1]
    n_filt = HY_ORDER * GROUP_W
    kr, ki = _fft_mid(tab, *_fft_forward(tab, kern.reshape(1, n1, FFT_N2 * n_filt), n_filt))
    y = _fft_conv_gate(tab, v, x1, bias[0], kr, ki, 0)
    y = _fft_conv_gate(tab, y, x2, bias[1], kr, ki, 1)
    return y.astype(dtype)


def alibi_slopes(n):
    return 2.0 ** (-8.0 * jnp.arange(1, n + 1, dtype=jnp.float32) / n)


def _window_kernel(q_ref, kp_ref, kc_ref, kn_ref, bias_ref, sink_ref, g_ref, o_ref, *, nb):
    bf16 = jnp.bfloat16
    nt = (((1,), (1,)), ((), ()))
    col = lax.broadcasted_iota(jnp.int32, (1, 3 * BLOCK), 1)
    q_all = q_ref[0] * (HEAD_DIM ** -0.5)
    kv_all = jnp.concatenate([kp_ref[0], kc_ref[0], kn_ref[0]], axis=0)
    chains = [(u, j) for u in range(WINDOW_SUB) for j in range(N_KV_HEADS)]
    s_list, v_list = [], []
    for u, j in chains:
        n = pl.program_id(1) * WINDOW_SUB + u
        edge = jnp.where(((n == 0) & (col < BLOCK)) | ((n == nb - 1) & (col >= 2 * BLOCK)), WINDOW_MASK, 0.0)
        q = q_all[u * BLOCK:(u + 1) * BLOCK]
        kv = kv_all[u * BLOCK:(u + 3) * BLOCK]
        q4 = jnp.concatenate([q[:, (j * GQA_GROUP + g) * HEAD_DIM:(j * GQA_GROUP + g + 1) * HEAD_DIM]
                              for g in range(GQA_GROUP)], axis=0)
        k = kv[:, j * HEAD_DIM:(j + 1) * HEAD_DIM]
        v_list.append(kv[:, KV_W + j * HEAD_DIM:KV_W + (j + 1) * HEAD_DIM].astype(bf16))
        s = lax.dot_general(q4.astype(bf16), k.astype(bf16), nt, preferred_element_type=jnp.float32)
        s_list.append(s + bias_ref[j] + edge)
    m_list = [jnp.maximum(jnp.max(s, axis=-1, keepdims=True), sink_ref[j]) for s, (u, j) in zip(s_list, chains)]
    p_list = [jnp.exp(s - m) for s, m in zip(s_list, m_list)]
    d_list = [jnp.sum(p, axis=-1, keepdims=True) + jnp.exp(sink_ref[j] - m)
              for p, m, (u, j) in zip(p_list, m_list, chains)]
    o_list = [jnp.dot(p.astype(bf16), v, preferred_element_type=jnp.float32) / d
              for p, v, d in zip(p_list, v_list, d_list)]
    for u in range(WINDOW_SUB):
        outs = []
        for j in range(N_KV_HEADS):
            o = o_list[u * N_KV_HEADS + j]
            outs += [o[g * BLOCK:(g + 1) * BLOCK] for g in range(GQA_GROUP)]
        o = jnp.concatenate(outs, axis=-1)
        o_ref[0, u * BLOCK:(u + 1) * BLOCK, :] = (
            o * lax.rsqrt(jnp.mean(o * o, axis=-1, keepdims=True) + EPS) * g_ref[...])


def window_mixer(proj, q_col, kv_col, sink, slopes, norm_g):
    b_, s_, _ = proj.shape
    assert q_col % GROUP_W == 0 and kv_col % (2 * KV_W) == 0
    nb = s_ // BLOCK
    sub = WINDOW_SUB
    n_steps = nb // sub
    rel = BLOCK + jnp.arange(BLOCK)[:, None] - jnp.arange(3 * BLOCK)[None, :]
    dist = jnp.abs(rel).astype(jnp.float32)
    bias = jnp.where(jnp.abs(rel) <= WINDOW, -slopes[:, None, None] * dist, WINDOW_MASK)
    bias = bias.reshape(N_KV_HEADS, GQA_GROUP * BLOCK, 3 * BLOCK)
    sink_rows = jnp.repeat(sink.astype(jnp.float32), BLOCK).reshape(N_KV_HEADS, GQA_GROUP * BLOCK, 1)
    kc = kv_col // (2 * KV_W)
    const = lambda shape: pl.BlockSpec(shape, lambda b, n: (0,) * len(shape))
    rows = pl.BlockSpec((1, sub * BLOCK, GROUP_W), lambda b, n: (b, n, q_col // GROUP_W))
    return pl.pallas_call(
        functools.partial(_window_kernel, nb=nb),
        grid=(b_, n_steps),
        in_specs=[rows,
                  pl.BlockSpec((1, BLOCK, 2 * KV_W), lambda b, n: (b, jnp.maximum(n * sub - 1, 0), kc)),
                  pl.BlockSpec((1, sub * BLOCK, 2 * KV_W), lambda b, n: (b, n, kc)),
                  pl.BlockSpec((1, BLOCK, 2 * KV_W), lambda b, n: (b, jnp.minimum((n + 1) * sub, nb - 1), kc)),
                  const(bias.shape), const(sink_rows.shape), const((1, GROUP_W))],
        out_specs=pl.BlockSpec((1, sub * BLOCK, GROUP_W), lambda b, n: (b, n, 0)),
        out_shape=jax.ShapeDtypeStruct((b_, s_, GROUP_W), jnp.float32),
        compiler_params=pltpu.CompilerParams(dimension_semantics=("arbitrary", "arbitrary")),
        name="window_attention",
    )(proj, proj, proj, proj, bias, sink_rows, norm_g.astype(jnp.float32).reshape(1, GROUP_W))


def _rope_tables(row_idx, col_idx, n_heads):
    half = HEAD_DIM // 2
    inv = ROPE_THETA ** (-jnp.arange(0, half, 2, dtype=jnp.float32) / half)

    def tabs(pos):
        ang = pos.astype(jnp.float32)[:, None] * inv[None, :]
        c, sn = jnp.cos(ang), jnp.sin(ang)
        z = jnp.zeros_like(sn)
        return (jnp.concatenate([c, c], -1), jnp.concatenate([-sn, z], -1), jnp.concatenate([z, sn], -1))

    per_head = [jnp.concatenate([r, c], -1) for r, c in zip(tabs(row_idx), tabs(col_idx))]
    return tuple(jnp.tile(t, (1, n_heads)) for t in per_head)


def _global_prep_kernel(q_ref, kv_ref, c_ref, s1_ref, s2_ref, gq_ref, gk_ref, bdq_ref, bdk_ref,
                        qa_ref, kt_ref, va_ref, kn2_ref):
    f32 = jnp.float32
    bf16 = jnp.bfloat16
    hd = HEAD_DIM
    quarter = hd // 4

    def norm_rope(x, g, bd):
        w = x.shape[1]
        y = x * lax.rsqrt(_group_sum(x * x, bd) * (1.0 / hd) + EPS) * g
        return (y * c_ref[:, :w] + pltpu.roll(y, w - quarter, 1) * s1_ref[:, :w]
                + pltpu.roll(y, quarter, 1) * s2_ref[:, :w])

    lane = lax.broadcasted_iota(jnp.int32, (1, 2 * hd), 1)

    def pair_slot(x2, j, tail):
        first = x2 if j == 0 else pltpu.roll(x2, hd, 1)
        return jnp.where(lane < hd, first, tail)

    q = norm_rope(q_ref[0], gq_ref[...], bdq_ref[...]) * (LOG2E * hd ** -0.5)
    qf = q.astype(bf16).astype(f32)
    qn = jnp.sqrt(_group_sum(qf * qf, bdq_ref[...]))
    for h in range(N_HEADS):
        p2 = slice((h // 2) * 2 * hd, (h // 2 + 1) * 2 * hd)
        n2 = qn[:, p2] if h % 2 == 1 else pltpu.roll(qn[:, p2], hd, 1)
        tail = jnp.where(lane == hd, -n2, 0.0)
        qa_ref[0, h // GQA_GROUP, h % GQA_GROUP] = pair_slot(qf[:, p2], h % 2, tail).astype(bf16)
    kv = kv_ref[0]
    k = norm_rope(kv[:, :KV_W], gk_ref[...], bdk_ref[...])
    kf = k.astype(bf16).astype(f32)
    kn2_ref[0] = _group_sum(kf * kf, bdk_ref[...])
    v = kv[:, KV_W:]
    one = jnp.where(lane == hd, 1.0, 0.0)
    for j in range(N_KV_HEADS):
        kt_ref[0, j] = pair_slot(kf, j, 0.0).T.astype(bf16)
        va_ref[0, j] = pair_slot(v, j, one).astype(bf16)


def _global_prep(proj, q_col, kv_col, tables, q_norm_g, k_norm_g):
    b_, s_, _ = proj.shape
    assert KV_W == 2 * HEAD_DIM and q_col % GROUP_W == 0 and kv_col % (2 * KV_W) == 0
    tm = CONV_TM
    wide = 2 * HEAD_DIM
    bf16 = jnp.bfloat16
    const = lambda a: pl.BlockSpec(a.shape, lambda b, i: (0,) * a.ndim)
    tab = pl.BlockSpec((tm, GROUP_W), lambda b, i: (i, 0))
    gq = jnp.tile(q_norm_g.astype(jnp.float32), N_HEADS).reshape(1, GROUP_W)
    gk = jnp.tile(k_norm_g.astype(jnp.float32), N_KV_HEADS).reshape(1, KV_W)
    bdq, bdk = _group_ones(GROUP_W), _group_ones(KV_W)
    return pl.pallas_call(
        _global_prep_kernel, grid=(b_, s_ // tm),
        in_specs=[pl.BlockSpec((1, tm, GROUP_W), lambda b, i: (b, i, q_col // GROUP_W)),
                  pl.BlockSpec((1, tm, 2 * KV_W), lambda b, i: (b, i, kv_col // (2 * KV_W))),
                  tab, tab, tab, const(gq), const(gk), const(bdq), const(bdk)],
        out_specs=[pl.BlockSpec((1, N_KV_HEADS, GQA_GROUP, tm, wide), lambda b, i: (b, 0, 0, i, 0)),
                   pl.BlockSpec((1, N_KV_HEADS, wide, tm), lambda b, i: (b, 0, 0, i)),
                   pl.BlockSpec((1, N_KV_HEADS, tm, wide), lambda b, i: (b, 0, i, 0)),
                   pl.BlockSpec((1, tm, KV_W), lambda b, i: (b, i, 0))],
        out_shape=[jax.ShapeDtypeStruct((b_, N_KV_HEADS, GQA_GROUP, s_, wide), bf16),
                   jax.ShapeDtypeStruct((b_, N_KV_HEADS, wide, s_), bf16),
                   jax.ShapeDtypeStruct((b_, N_KV_HEADS, s_, wide), bf16),
                   jax.ShapeDtypeStruct((b_, s_, KV_W), jnp.float32)],
        compiler_params=pltpu.CompilerParams(dimension_semantics=("arbitrary", "arbitrary")),
        name="global_prep",
    )(proj, proj, *tables, gq, gk, bdq, bdk)


def _flash_kernel(q_ref, kt_ref, v_ref, o_ref, *, tq, tk, n_kc):
    m_rows = GQA_GROUP * tq
    q = q_ref[0, 0].reshape(m_rows, 2 * HEAD_DIM)

    def body(c, acc):
        off = pl.multiple_of(c * tk, tk)
        s = jnp.dot(q, kt_ref[0, 0, :, pl.ds(off, tk)], preferred_element_type=jnp.float32)
        p = jnp.exp2(s).astype(jnp.bfloat16)
        return acc + jnp.dot(p, v_ref[0, 0, pl.ds(off, tk), :], preferred_element_type=jnp.float32)

    acc = lax.fori_loop(0, n_kc, body, jnp.zeros((m_rows, 2 * HEAD_DIM), jnp.float32))
    o = acc[:, :HEAD_DIM] / acc[:, HEAD_DIM:HEAD_DIM + 1]
    o_ref[0] = jnp.concatenate([o[g * tq:(g + 1) * tq] for g in range(GQA_GROUP)], axis=-1)


def _rowmax_kernel(q_ref, kt_ref, m_ref, *, tq, tk, n_kc):
    m_rows = GQA_GROUP * tq
    q = q_ref[0, 0].reshape(m_rows, 2 * HEAD_DIM)

    def body(c, mx):
        off = pl.multiple_of(c * tk, tk)
        s = jnp.dot(q, kt_ref[0, 0, :, pl.ds(off, tk)], preferred_element_type=jnp.float32)
        for j in range(tk // 128):
            mx = jnp.maximum(mx, s[:, j * 128:(j + 1) * 128])
        return mx

    mx = lax.fori_loop(0, n_kc, body, jnp.full((m_rows, 128), -jnp.inf, jnp.float32))
    m_ref[0, 0] = jnp.max(mx, axis=-1, keepdims=True).reshape(GQA_GROUP, tq, 1)


def _attn_call(body, q, kt, v, name, *, tq=ATTN_TQ, tk=ATTN_TK):
    b_, _, _, s_, _ = q.shape
    wide = 2 * HEAD_DIM
    q_spec = pl.BlockSpec((1, 1, GQA_GROUP, tq, wide), lambda b, h, i: (b, h, 0, i, 0))
    kt_spec = pl.BlockSpec((1, 1, wide, s_), lambda b, h, i: (b, h, 0, 0))
    v_spec = pl.BlockSpec((1, 1, s_, wide), lambda b, h, i: (b, h, 0, 0))
    if v is None:
        operands, in_specs = (q, kt), [q_spec, kt_spec]
        o_spec = pl.BlockSpec((1, 1, GQA_GROUP, tq, 1), lambda b, h, i: (b, h, 0, i, 0))
        o_shape = (b_, N_KV_HEADS, GQA_GROUP, s_, 1)
    else:
        operands, in_specs = (q, kt, v), [q_spec, kt_spec, v_spec]
        o_spec = pl.BlockSpec((1, tq, GQA_GROUP * HEAD_DIM), lambda b, h, i: (b, i, h))
        o_shape = (b_, s_, N_HEADS * HEAD_DIM)
    return pl.pallas_call(
        functools.partial(body, tq=tq, tk=tk, n_kc=s_ // tk),
        out_shape=jax.ShapeDtypeStruct(o_shape, jnp.float32),
        grid=(b_, N_KV_HEADS, s_ // tq),
        in_specs=in_specs, out_specs=o_spec,
        compiler_params=pltpu.CompilerParams(
            dimension_semantics=("arbitrary", "arbitrary", "arbitrary"),
            vmem_limit_bytes=ATTN_VMEM_LIMIT_BYTES),
        name=name,
    )(*operands)


def global_mixer(proj, tables, q_norm_g, k_norm_g):
    bf16 = jnp.bfloat16
    hd = HEAD_DIM
    qa, kt, va, kn2 = _global_prep(proj, PROJ_DEST["d_q"][1], PROJ_DEST["d_kv"][1], tables,
                                   q_norm_g, k_norm_g)
    kn = jnp.sqrt(jnp.max(kn2, axis=1)[:, ::hd])
    c = (kn * ATTN_ROUND_UP).astype(bf16)
    qmax = jnp.max(-qa[:, :, :, :, hd].astype(jnp.float32), axis=(2, 3))
    with_key_row = lambda row: kt.at[:, :, hd, :].set(jnp.broadcast_to(row[:, :, None], kt.shape[:2] + kt.shape[3:]))

    def exact_shift():
        m = _attn_call(_rowmax_kernel, qa, kt, None, "global_attention_rowmax")
        return qa.at[:, :, :, :, hd].set((-m[..., 0]).astype(bf16)), with_key_row(jnp.ones_like(c))

    qa, kta = lax.cond(jnp.max(qmax * c.astype(jnp.float32)) < ATTN_SAFE_SHIFT,
                       lambda: (qa, with_key_row(c)), exact_shift)
    return _attn_call(_flash_kernel, qa, kta, va, "global_flash_attention")


def _row_copy(src, src_row, dst, dst_row, sem):
    return pltpu.make_async_copy(src.at[pl.ds(src_row, 1)], dst.at[pl.ds(dst_row, 1)], sem)


def _moe_dispatch_kernel(pos_ref, x_ref, g_ref, zeros_hbm, xs_hbm, xn_scr, sem, *, tb, n_steps):
    del zeros_hbm
    i = pl.program_id(0)
    slot = i % 2

    def wait_slot(s):
        for _ in range(TOP_K):
            pltpu.make_async_copy(xn_scr.at[s], xn_scr.at[s], sem.at[s]).wait()

    @pl.when(i >= 2)
    def _():
        wait_slot(slot)

    x = x_ref[...]
    xn_scr[slot] = x * lax.rsqrt(jnp.mean(x * x, axis=-1, keepdims=True) + EPS) * g_ref[...]

    def issue(r, carry):
        for k in range(TOP_K):
            _row_copy(xn_scr.at[slot], r, xs_hbm, pos_ref[(i * tb + r) * TOP_K + k], sem.at[slot]).start()
        return carry

    lax.fori_loop(0, tb, issue, 0, unroll=8)

    @pl.when(i == n_steps - 1)
    def _():
        if n_steps >= 2:
            wait_slot(1 - slot)
        wait_slot(slot)


def _moe_expert_kernel(blk_e_ref, blk_rows_ref, xs_ref, wg_ref, wu_ref, wd_ref, o_ref, *, tm):
    del blk_e_ref
    rows = blk_rows_ref[pl.program_id(0)]

    @pl.when(rows > 0)
    def _():
        x = xs_ref[...].astype(jnp.bfloat16)
        g = jnp.dot(x, wg_ref[0], preferred_element_type=jnp.float32)
        u = jnp.dot(x, wu_ref[0], preferred_element_type=jnp.float32)
        h = (g * jax.nn.sigmoid(g) * u).astype(jnp.bfloat16)
        o_ref[...] = jnp.dot(h, wd_ref[0], preferred_element_type=jnp.float32)

    @pl.when(rows == 0)
    def _():
        o_ref[...] = jnp.zeros((tm, o_ref.shape[1]), jnp.float32)


def _moe_combine_kernel(pos_ref, x_ref, w_ref, g_ref, os_hbm, y_ref, buf, sem, *, tb, n_steps, out_norm):
    i = pl.program_id(0)
    slot = i % 2

    def fetch(step, s):
        def issue(r, carry):
            for k in range(TOP_K):
                _row_copy(os_hbm, pos_ref[(step * tb + r) * TOP_K + k], buf.at[s, k], r, sem.at[s]).start()
            return carry

        lax.fori_loop(0, tb, issue, 0, unroll=8)

    @pl.when(i == 0)
    def _():
        fetch(0, 0)

    @pl.when(i + 1 < n_steps)
    def _():
        fetch(i + 1, 1 - slot)

    for k in range(TOP_K):
        pltpu.make_async_copy(buf.at[slot, k], buf.at[slot, k], sem.at[slot]).wait()
    w = w_ref[...]
    y = x_ref[...]
    for k in range(TOP_K):
        y = y + w[:, k:k + 1] * buf[slot, k]
    if out_norm:
        y = y * lax.rsqrt(jnp.mean(y * y, axis=-1, keepdims=True) + EPS) * g_ref[...]
    y_ref[...] = y


def _router_kernel(x_ref, g_ref, w_ref, o_ref):
    x = x_ref[...]
    xn = x * lax.rsqrt(jnp.mean(x * x, axis=-1, keepdims=True) + EPS) * g_ref[...]
    o_ref[...] = jnp.dot(xn.astype(jnp.bfloat16), w_ref[...], preferred_element_type=jnp.float32)


def _router_logits(xt, norm_g, w_group, w_expert):
    n_tok, d_ = xt.shape
    n_log = N_EXPERT_GROUPS + N_EXPERTS
    w = jnp.concatenate([w_group, w_expert, jnp.zeros((d_, ROUTER_W - n_log), w_group.dtype)], axis=1)
    return pl.pallas_call(
        _router_kernel, grid=(n_tok // CONV_TM,),
        in_specs=[pl.BlockSpec((CONV_TM, d_), lambda i: (i, 0)), pl.BlockSpec((1, d_), lambda i: (0, 0)),
                  pl.BlockSpec((d_, ROUTER_W), lambda i: (0, 0))],
        out_specs=pl.BlockSpec((CONV_TM, ROUTER_W), lambda i: (i, 0)),
        out_shape=jax.ShapeDtypeStruct((n_tok, ROUTER_W), jnp.float32),
        compiler_params=pltpu.CompilerParams(dimension_semantics=("arbitrary",)),
        name="moe_router",
    )(xt, norm_g.astype(jnp.float32).reshape(1, d_), w.astype(jnp.bfloat16))


def _moe_route(logits, b_group, b_expert, tm):
    n_tok = logits.shape[0]
    gp = jax.nn.softmax(logits[:, :N_EXPERT_GROUPS] + b_group.astype(jnp.float32), axis=-1)
    g_idx = jnp.argmax(gp, axis=-1, keepdims=True)
    g_w = jnp.max(gp, axis=-1, keepdims=True)
    elog = logits[:, N_EXPERT_GROUPS:N_EXPERT_GROUPS + N_EXPERTS] + b_expert.astype(jnp.float32)
    elog = elog.reshape(n_tok, N_EXPERT_GROUPS, EXPERTS_PER_GROUP)
    elog_sel = jnp.take_along_axis(elog, g_idx[:, :, None], axis=1)[:, 0]
    e_w, e_idx = lax.top_k(jax.nn.softmax(elog_sel, axis=-1), TOP_K)
    e_w = e_w / jnp.sum(e_w, axis=-1, keepdims=True)
    weights = g_w * e_w
    experts = (g_idx * EXPERTS_PER_GROUP + e_idx).reshape(-1)
    onehot = (experts[:, None] == jnp.arange(N_EXPERTS)[None, :]).astype(jnp.int32)
    csum = jnp.cumsum(onehot, axis=0)
    counts = csum[-1]
    rank = jnp.sum(onehot * csum, axis=1) - 1
    padded = ((counts + tm - 1) // tm) * tm
    pend = jnp.cumsum(padded)
    pstart = pend - padded
    pos = (pstart[experts] + rank).astype(jnp.int32)
    n_blk = (n_tok * TOP_K) // tm + N_EXPERTS
    blk_start = jnp.arange(n_blk) * tm
    blk_e = jnp.minimum(jnp.sum(pend[None, :] <= blk_start[:, None], axis=1), N_EXPERTS - 1).astype(jnp.int32)
    blk_rows = jnp.clip(counts[blk_e] - (blk_start - pstart[blk_e]), 0, tm).astype(jnp.int32)
    return weights, pos, blk_e, blk_rows, n_blk


def hier_moe_residual(x, norm_g, w_group, b_group, w_expert, b_expert, w_gate, w_up, w_down,
                      out_norm_g=None, *, tm=MOE_TM, tb=MOE_TB):
    b_, s_, d_ = x.shape
    n_tok = b_ * s_
    xt = x.reshape(n_tok, d_)
    weights, pos, blk_e, blk_rows, n_blk = _moe_route(_router_logits(xt, norm_g, w_group, w_expert),
                                                      b_group, b_expert, tm)
    n_pad = n_blk * tm
    vmem = pltpu.CompilerParams(dimension_semantics=("arbitrary",),
                                vmem_limit_bytes=MOE_VMEM_LIMIT_BYTES)
    xs = pl.pallas_call(
        functools.partial(_moe_dispatch_kernel, tb=tb, n_steps=n_tok // tb),
        grid_spec=pltpu.PrefetchScalarGridSpec(
            num_scalar_prefetch=1, grid=(n_tok // tb,),
            in_specs=[pl.BlockSpec((tb, d_), lambda i, pos: (i, 0)),
                      pl.BlockSpec((1, d_), lambda i, pos: (0, 0)),
                      pl.BlockSpec(memory_space=pl.ANY)],
            out_specs=pl.BlockSpec(memory_space=pl.ANY),
            scratch_shapes=[pltpu.VMEM((2, tb, d_), jnp.float32), pltpu.SemaphoreType.DMA((2,))]),
        out_shape=jax.ShapeDtypeStruct((n_pad, d_), jnp.float32),
        input_output_aliases={3: 0},
        compiler_params=vmem, name="moe_dispatch",
    )(pos, xt, norm_g.astype(jnp.float32).reshape(1, d_), jnp.zeros((n_pad, d_), jnp.float32))
    bf16 = jnp.bfloat16
    outs = pl.pallas_call(
        functools.partial(_moe_expert_kernel, tm=tm),
        grid_spec=pltpu.PrefetchScalarGridSpec(
            num_scalar_prefetch=2, grid=(n_blk,),
            in_specs=[pl.BlockSpec((tm, d_), lambda i, be, br: (i, 0)),
                      pl.BlockSpec((1, d_, D_EXPERT), lambda i, be, br: (be[i], 0, 0)),
                      pl.BlockSpec((1, d_, D_EXPERT), lambda i, be, br: (be[i], 0, 0)),
                      pl.BlockSpec((1, D_EXPERT, d_), lambda i, be, br: (be[i], 0, 0))],
            out_specs=pl.BlockSpec((tm, d_), lambda i, be, br: (i, 0))),
        out_shape=jax.ShapeDtypeStruct((n_pad, d_), jnp.float32),
        compiler_params=vmem, name="moe_experts",
    )(blk_e, blk_rows, xs, w_gate.astype(bf16), w_up.astype(bf16), w_down.astype(bf16))
    y = pl.pallas_call(
        functools.partial(_moe_combine_kernel, tb=tb, n_steps=n_tok // tb, out_norm=out_norm_g is not None),
        grid_spec=pltpu.PrefetchScalarGridSpec(
            num_scalar_prefetch=1, grid=(n_tok // tb,),
            in_specs=[pl.BlockSpec((tb, d_), lambda i, pos: (i, 0)),
                      pl.BlockSpec((tb, TOP_K), lambda i, pos: (i, 0)),
                      pl.BlockSpec((1, d_), lambda i, pos: (0, 0)),
                      pl.BlockSpec(memory_space=pl.ANY)],
            out_specs=pl.BlockSpec((tb, d_), lambda i, pos: (i, 0)),
            scratch_shapes=[pltpu.VMEM((2, TOP_K, tb, d_), jnp.float32), pltpu.SemaphoreType.DMA((2,))]),
        out_shape=jax.ShapeDtypeStruct((n_tok, d_), jnp.float32),
        compiler_params=vmem, name="moe_combine",
    )(pos, xt, weights, (norm_g if out_norm_g is None else out_norm_g).astype(jnp.float32).reshape(1, d_), outs)
    return y.reshape(b_, s_, d_)


PROJ_DEST = {"a_qkv": (0, 0), "a_z": (1, 1536), "b_in": (4, 2048), "c_q": (5, 3584), "d_q": (7, 4096),
             "c_kv": (6, 4608), "d_kv": (8, 4864), "a_beta": (2, 5120), "a_alpha": (3, 5136)}
PROJ_WIDTH = 5632
assert PROJ_DEST["a_alpha"][1] == PROJ_DEST["a_beta"][1] + 2 * N_HEADS


def _in_proj_weight(w):
    starts = [0] + _split_points()
    cols = jnp.zeros((w.shape[0], PROJ_WIDTH), jnp.bfloat16)
    for seg, dest in PROJ_DEST.values():
        cols = lax.dynamic_update_slice(
            cols, w[:, starts[seg]:starts[seg] + IN_SPLIT_SIZES[seg]].astype(jnp.bfloat16), (0, dest))
    return cols


def _in_proj_kernel(x_ref, g_ref, w_ref, o_ref, xn_scr):
    @pl.when(pl.program_id(1) == 0)
    def _():
        x = x_ref[...]
        xn = x * lax.rsqrt(jnp.mean(x * x, axis=-1, keepdims=True) + EPS) * g_ref[...]
        xn_scr[...] = xn.astype(jnp.bfloat16)

    o_ref[...] = jnp.dot(xn_scr[...], w_ref[...], preferred_element_type=jnp.float32)


def in_proj(x, norm_g, w):
    n_tok, d_ = x.shape
    return pl.pallas_call(
        _in_proj_kernel,
        grid=(n_tok // PROJ_TM, PROJ_WIDTH // PROJ_TILE),
        in_specs=[pl.BlockSpec((PROJ_TM, d_), lambda i, j: (i, 0)),
                  pl.BlockSpec((1, d_), lambda i, j: (0, 0)),
                  pl.BlockSpec((d_, PROJ_TILE), lambda i, j: (0, j))],
        out_specs=pl.BlockSpec((PROJ_TM, PROJ_TILE), lambda i, j: (i, j)),
        out_shape=jax.ShapeDtypeStruct((n_tok, PROJ_WIDTH), jnp.float32),
        scratch_shapes=[pltpu.VMEM((PROJ_TM, d_), jnp.bfloat16)],
        compiler_params=pltpu.CompilerParams(dimension_semantics=("arbitrary", "arbitrary"),
                                             vmem_limit_bytes=PROJ_VMEM_LIMIT_BYTES),
        name="in_proj",
    )(x, norm_g.astype(jnp.float32).reshape(1, d_), _in_proj_weight(w))


def _out_proj_kernel(x_ref, *refs, normed):
    n = len(normed)
    y_refs, g_ref, w_ref, o_ref = refs[:n], refs[n], refs[n + 1], refs[n + 2]
    acc = x_ref[...]
    for k, y_ref in enumerate(y_refs):
        y = y_ref[...]
        if normed[k]:
            y = y * lax.rsqrt(jnp.mean(y * y, axis=-1, keepdims=True) + EPS) * g_ref[k:k + 1, :]
        acc = acc + jnp.dot(y.astype(jnp.bfloat16), w_ref[k * GROUP_W:(k + 1) * GROUP_W, :],
                            preferred_element_type=jnp.float32)
    o_ref[...] = acc


def out_proj_residual(x, ys, gains, w):
    n_tok, d_ = x.shape
    row = lambda width: pl.BlockSpec((OUT_TM, width), lambda i: (i, 0))
    g = jnp.stack([jnp.ones((GROUP_W,), jnp.float32) if gk is None else gk.astype(jnp.float32)
                   for gk in gains])
    return pl.pallas_call(
        functools.partial(_out_proj_kernel, normed=tuple(gk is not None for gk in gains)),
        grid=(n_tok // OUT_TM,),
        in_specs=[row(d_)] + [row(GROUP_W)] * len(ys) + [pl.BlockSpec(g.shape, lambda i: (0, 0)),
                                                          pl.BlockSpec(w.shape, lambda i: (0, 0))],
        out_specs=row(d_),
        out_shape=jax.ShapeDtypeStruct((n_tok, d_), jnp.float32),
        compiler_params=pltpu.CompilerParams(dimension_semantics=("arbitrary",),
                                             vmem_limit_bytes=PROJ_VMEM_LIMIT_BYTES),
        name="out_proj",
    )(x, *ys, g, w.astype(jnp.bfloat16))


def kernel(x, norm_mix, w_in, gdn_conv, gdn_a_log, gdn_dt_bias, gdn_norm, hy_conv, hy_w1, hy_b1, hy_freq1, hy_w2, hy_b2, hy_freq2, hy_w3, hy_deltas, hy_bias, hy_norm, swa_sink, swa_norm, ga_q_norm, ga_k_norm, ga_norm, w_out, norm_ffn, moe_w_group, moe_b_group, moe_w_expert, moe_b_expert, moe_w_gate, moe_w_up, moe_w_down, norm_final):
    b_, s_, _ = x.shape
    rows = s_ // GRID_W
    row_idx = jnp.repeat(jnp.arange(rows), GRID_W)
    col_idx = jnp.tile(jnp.arange(GRID_W), rows)
    rope_tables = _rope_tables(row_idx, col_idx, N_HEADS)
    pos_feat = hyena_pos_features(s_)
    slopes = alibi_slopes(N_HEADS)
    n_tok = b_ * s_
    for l in range(DEPTH):
        proj = in_proj(x.reshape(n_tok, D_MODEL), norm_mix[l], w_in[l]).reshape(b_, s_, PROJ_WIDTH)
        y_a = gdn_mixer(proj, gdn_conv[l], gdn_a_log[l], gdn_dt_bias[l], gdn_norm[l])
        kf = hyena_filters_f(pos_feat, hy_w1[l], hy_b1[l], hy_freq1[l], hy_w2[l], hy_b2[l],
                             hy_freq2[l], hy_w3[l], hy_deltas[l])
        y_b = hyena_mixer(proj, hy_conv[l], kf, hy_bias[l])
        y_c = window_mixer(proj, PROJ_DEST["c_q"][1], PROJ_DEST["c_kv"][1], swa_sink[l], slopes, swa_norm[l])
        y_d = global_mixer(proj, rope_tables, ga_q_norm[l], ga_k_norm[l])
        x = out_proj_residual(x.reshape(n_tok, D_MODEL),
                              [y.reshape(n_tok, GROUP_W) for y in (y_a, y_b, y_c, y_d)],
                              [None, hy_norm[l], None, ga_norm[l]], w_out[l]).reshape(b_, s_, D_MODEL)
        x = hier_moe_residual(x, norm_ffn[l], moe_w_group[l], moe_b_group[l], moe_w_expert[l],
                              moe_b_expert[l], moe_w_gate[l], moe_w_up[l], moe_w_down[l],
                              norm_final if l == DEPTH - 1 else None)
    return x
```

```python
import functools
import math

import jax
import jax.numpy as jnp
import numpy as np
from jax import lax
from jax.experimental import pallas as pl
from jax.experimental.pallas import tpu as pltpu

D_MODEL = 2048
DEPTH = 2
N_MIXERS = 4
GROUP_W = D_MODEL // N_MIXERS
HEAD_DIM = 64
N_HEADS = GROUP_W // HEAD_DIM
N_KV_HEADS = 2
GQA_GROUP = N_HEADS // N_KV_HEADS
KV_W = N_KV_HEADS * HEAD_DIM
SHORT_CONV = 3
GDN_CHUNK = 64
HY_ORDER = 2
HY_EMB = 33
HY_BANDS = (HY_EMB - 1) // 2
WINDOW = 128
BLOCK = 128
GRID_W = 64
ROPE_THETA = 10000.0
N_EXPERT_GROUPS = 4
EXPERTS_PER_GROUP = 8
N_EXPERTS = N_EXPERT_GROUPS * EXPERTS_PER_GROUP
TOP_K = 2
D_EXPERT = 512
MOE_BLOCK = 128
EPS = 1e-6
IN_SPLIT_SIZES = (3 * GROUP_W, GROUP_W, 2 * N_HEADS, 2 * N_HEADS, 3 * GROUP_W,
                  GROUP_W, 2 * KV_W, GROUP_W, 2 * KV_W)

ATTN_VMEM_LIMIT_BYTES = 48 * 1024 * 1024
ATTN_TQ = 256
ATTN_TK = 2048
WINDOW_SUB = 4
WINDOW_MASK = -1e30
GDN_VMEM_LIMIT_BYTES = 48 * 1024 * 1024
PROJ_VMEM_LIMIT_BYTES = 48 * 1024 * 1024
CONV_TM = 512
CONV_HALO = 8
PROJ_TM = 1024
PROJ_TILE = 512
OUT_TM = 256
GDN_SB = 256
GDN_INV_BASE = 8
GDN_KCHUNK = 64
GDN_HG = 4
FFT_VMEM_LIMIT_BYTES = 48 * 1024 * 1024
FFT_N2 = 256
FFT_COL_TILE = 4096
MOE_VMEM_LIMIT_BYTES = 48 * 1024 * 1024
MOE_TM = 512
MOE_TB = 256
ROUTER_W = 128
LOG2E = 1.4426950408889634
ATTN_SAFE_SHIFT = 50.0
ATTN_ROUND_UP = 1.01


def _split_points():
    return [int(v) for v in np.cumsum(IN_SPLIT_SIZES)[:-1]]


def _group_ones(width):
    r = np.arange(width) // HEAD_DIM
    return jnp.asarray(r[:, None] == r[None, :], jnp.bfloat16)


def _group_sum(x, bd):
    hi = x.astype(jnp.bfloat16)
    lo = (x - hi.astype(jnp.float32)).astype(jnp.bfloat16)
    return (jnp.dot(hi, bd, preferred_element_type=jnp.float32)
            + jnp.dot(lo, bd, preferred_element_type=jnp.float32))


def _conv_prep_kernel(*refs, modes, n_steps):
    n = len(modes)
    w_ref, bd_ref = refs[3 * n], refs[3 * n + 1]
    o_refs = refs[3 * n + 2:]
    i = pl.program_id(1)
    tm = refs[0].shape[1]
    row = lax.broadcasted_iota(jnp.int32, (tm, 1), 0)
    for t, mode in enumerate(modes):
        x_ref, prev_ref, next_ref = refs[3 * t:3 * t + 3]
        x = x_ref[0]
        prev = jnp.where(i > 0, prev_ref[0][CONV_HALO - 1:CONV_HALO], 0.0)
        nxt = jnp.where(i < n_steps - 1, next_ref[0][0:1], 0.0)
        x_prev = jnp.where(row == 0, prev, pltpu.roll(x, 1, 0))
        x_next = jnp.where(row == tm - 1, nxt, pltpu.roll(x, tm - 1, 0))
        w = w_ref[:, t * GROUP_W:(t + 1) * GROUP_W]
        y = x_prev * w[0:1] + x * w[1:2] + x_next * w[2:3]
        if mode != "plain":
            y = y * jax.nn.sigmoid(y)
        if mode.startswith("silu_l2"):
            y = y * lax.rsqrt(_group_sum(y * y, bd_ref[...]) + EPS)
        if mode == "silu_l2_scaled":
            y = y * (HEAD_DIM ** -0.5)
        o_refs[t][0] = y


def _conv_prep(proj, col, conv_w, modes):
    b_, s_, _ = proj.shape
    assert col % GROUP_W == 0
    tm = CONV_TM
    n_steps = s_ // tm
    hb = tm // CONV_HALO
    specs = []
    for t in range(len(modes)):
        c = col // GROUP_W + t
        specs += [pl.BlockSpec((1, tm, GROUP_W), lambda b, i, c=c: (b, i, c)),
                  pl.BlockSpec((1, CONV_HALO, GROUP_W), lambda b, i, c=c: (b, jnp.maximum(i * hb - 1, 0), c)),
                  pl.BlockSpec((1, CONV_HALO, GROUP_W),
                               lambda b, i, c=c: (b, jnp.minimum((i + 1) * hb, s_ // CONV_HALO - 1), c))]
    bd = _group_ones(GROUP_W)
    w = conv_w.astype(jnp.float32)
    specs += [pl.BlockSpec(w.shape, lambda b, i: (0, 0)), pl.BlockSpec(bd.shape, lambda b, i: (0, 0))]
    out = jax.ShapeDtypeStruct((b_, s_, GROUP_W), jnp.float32)
    return pl.pallas_call(
        functools.partial(_conv_prep_kernel, modes=modes, n_steps=n_steps),
        grid=(b_, n_steps), in_specs=specs,
        out_specs=[pl.BlockSpec((1, tm, GROUP_W), lambda b, i: (b, i, 0))] * len(modes),
        out_shape=[out] * len(modes),
        compiler_params=pltpu.CompilerParams(dimension_semantics=("arbitrary", "arbitrary")),
        name="conv_prep",
    )(*([proj] * (3 * len(modes))), w, bd)


def _gdn_post_kernel(of_ref, ob_ref, z_ref, g_ref, bd_ref, y_ref):
    o = of_ref[0] + ob_ref[0]
    z = z_ref[0]
    ms = _group_sum(o * o, bd_ref[...]) * (1.0 / HEAD_DIM)
    y_ref[0] = o * lax.rsqrt(ms + EPS) * g_ref[...] * (z * jax.nn.sigmoid(z))


def _gdn_post(o_fwd, o_bwd, proj, z_col, norm_g):
    b_, s_, w_ = o_fwd.shape
    tm = CONV_TM
    tok = pl.BlockSpec((1, tm, w_), lambda b, i: (b, i, 0))
    bd = _group_ones(w_)
    g = jnp.tile(norm_g.astype(jnp.float32), N_HEADS).reshape(1, w_)
    return pl.pallas_call(
        _gdn_post_kernel, grid=(b_, s_ // tm),
        in_specs=[tok, tok, pl.BlockSpec((1, tm, w_), lambda b, i: (b, i, z_col // w_)),
                  pl.BlockSpec(g.shape, lambda b, i: (0, 0)), pl.BlockSpec(bd.shape, lambda b, i: (0, 0))],
        out_specs=tok, out_shape=jax.ShapeDtypeStruct((b_, s_, w_), jnp.float32),
        compiler_params=pltpu.CompilerParams(dimension_semantics=("arbitrary", "arbitrary")),
        name="gdn_post",
    )(o_fwd, o_bwd, proj, g, bd)


def _gdn_masks(rev):
    r = np.arange(GDN_SB)
    i, j = r[:, None], r[None, :]
    same = lambda s: (i // s) == (j // s)
    before = (i < j) if rev else (i > j)
    chunk = same(GDN_KCHUNK)
    masks = [chunk & (before | (i == j)),
             chunk,
             chunk & before,
             i == j,
             same(GDN_INV_BASE),
             same(HEAD_DIM)]
    s = GDN_INV_BASE
    while s < GDN_KCHUNK:
        masks.append(same(2 * s) & ~same(s))
        s *= 2
    return np.stack(masks).astype(np.float32)


def _mm_bf16(a, b):
    return jnp.dot(a.astype(jnp.bfloat16), b.astype(jnp.bfloat16), preferred_element_type=jnp.float32)


def _unit_triangular_inverses(a_list, eye, m_base, m_offs):
    n_list = [-a * m_base for a in a_list]
    t_list = [eye + n for n in n_list]
    power = 2
    while power < GDN_INV_BASE:
        n_list = [_mm_bf16(n, n) for n in n_list]
        t_list = [t + _mm_bf16(t, n) for t, n in zip(t_list, n_list)]
        power *= 2
    for m_off in m_offs:
        u_list = [_mm_bf16(a * m_off, t) for a, t in zip(a_list, t_list)]
        t_list = [t - _mm_bf16(t, u) for t, u in zip(t_list, u_list)]
    return t_list


def _gdn_kernel(mask_ref, expand_ref, q_ref, k_ref, v_ref, beta_ref, g_ref, gt_ref, o_ref,
                state, vstack, val_s, kcd_s, qd_s, kd_s, attn_s, *, rev):
    bf16 = jnp.bfloat16
    f32 = jnp.float32
    hd = HEAD_DIM
    gw = GDN_HG * hd
    n_groups = N_HEADS // GDN_HG
    rr = GDN_SB
    nt = (((1,), (1,)), ((), ()))
    tn = (((0,), (0,)), ((), ()))

    @pl.when(pl.program_id(1) == 0)
    def _():
        state[...] = jnp.zeros(state.shape, state.dtype)
        vstack[...] = jnp.zeros(vstack.shape, vstack.dtype)

    incl, ones, strict, eye, m_base, head_blk = (mask_ref[t] for t in range(6))
    m_offs = [mask_ref[t] for t in range(6, mask_ref.shape[0])]
    g = g_ref[0, 0]
    gc = _dot_f32(incl, g)
    gl = _dot_f32(ones, g)
    gct = lax.dot_general(gt_ref[0, 0], incl, nt, precision=lax.Precision.HIGHEST,
                          preferred_element_type=jnp.float32)
    expand = expand_ref[...]
    beta_w = _dot_f32(beta_ref[0, 0], expand)
    eg_w = jnp.exp(_dot_f32(gc, expand))
    ekd_w = jnp.exp(_dot_f32(gl - gc, expand))
    cd_w = jnp.exp(_dot_f32(gl, expand))
    lane = lax.broadcasted_iota(jnp.int32, (1, gw), 1)
    head_mask = [(lane // hd == j).astype(f32) for j in range(GDN_HG)]
    a_list = []
    for grp in range(n_groups):
        sl = slice(grp * gw, (grp + 1) * gw)
        q, k = q_ref[0][:, sl], k_ref[0][:, sl]
        kb = k * beta_w[:, sl]
        kbf = k.astype(bf16)
        qd_s[grp] = (q * eg_w[:, sl]).astype(bf16)
        kd_s[grp] = (k * ekd_w[:, sl]).astype(bf16)
        for j in range(GDN_HG):
            h = grp * GDN_HG + j
            dec = jnp.exp(jnp.minimum(gc[:, h:h + 1] - gct[h:h + 1, :], 0.0)) * incl
            a_list.append(lax.dot_general((kb * head_mask[j]).astype(bf16), kbf, nt,
                                          preferred_element_type=f32) * dec * strict)
            attn = lax.dot_general((q * head_mask[j]).astype(bf16), kbf, nt, preferred_element_type=f32) * dec
            attn_s[h] = attn.astype(bf16)
    t_list = _unit_triangular_inverses(a_list, eye, m_base, m_offs)
    for grp in range(n_groups):
        sl = slice(grp * gw, (grp + 1) * gw)
        vb = v_ref[0][:, sl] * beta_w[:, sl]
        kbg = k_ref[0][:, sl] * beta_w[:, sl] * eg_w[:, sl]
        val = jnp.zeros((rr, gw), f32)
        kcd = jnp.zeros((rr, gw), f32)
        for j in range(GDN_HG):
            t = t_list[grp * GDN_HG + j]
            val = val + _mm_bf16(t, vb * head_mask[j])
            kcd = kcd + _mm_bf16(t, kbg * head_mask[j])
        val_s[grp] = val
        kcd_s[grp] = kcd.astype(bf16)
    n_chunks = GDN_SB // GDN_KCHUNK
    for c in (reversed(range(n_chunks)) if rev else range(n_chunks)):
        rows = slice(c * GDN_KCHUNK, (c + 1) * GDN_KCHUNK)
        for grp in range(n_groups):
            s_old = state[grp]
            s_bf = s_old.astype(bf16)
            both = jnp.dot(jnp.concatenate([kcd_s[grp, rows, :], qd_s[grp, rows, :]], axis=0), s_bf,
                           preferred_element_type=f32)
            v_new = val_s[grp, rows, :] - both[:GDN_KCHUNK]
            for j in range(GDN_HG):
                vstack[grp, j * rr + c * GDN_KCHUNK:j * rr + (c + 1) * GDN_KCHUNK, :] = (
                    v_new * head_mask[j]).astype(bf16)
            attn_cat = jnp.concatenate([attn_s[grp * GDN_HG + j, rows, :] for j in range(GDN_HG)], axis=1)
            o = both[GDN_KCHUNK:] + jnp.dot(attn_cat, vstack[grp], preferred_element_type=f32)
            upd = lax.dot_general(kd_s[grp, rows, :], v_new.astype(bf16), tn, preferred_element_type=f32)
            state[grp] = s_old * cd_w[c * GDN_KCHUNK:c * GDN_KCHUNK + 1, grp * gw:(grp + 1) * gw] + upd * head_blk
            o_ref[0, rows, grp * gw:(grp + 1) * gw] = o


def _gdn_scan(q, k, v, beta, g, rev):
    b_, s_, w_ = q.shape
    assert GDN_SB == GDN_HG * HEAD_DIM and GDN_SB % GDN_KCHUNK == 0 and GDN_KCHUNK % GDN_CHUNK == 0
    n_sb = s_ // GDN_SB
    n_groups = N_HEADS // GDN_HG
    gw = GDN_HG * HEAD_DIM
    masks = jnp.asarray(_gdn_masks(rev))
    expand = jnp.asarray(np.arange(N_HEADS)[:, None] == np.arange(w_)[None, :] // HEAD_DIM, jnp.float32)
    step = (lambda i: n_sb - 1 - i) if rev else (lambda i: i)
    tok = pl.BlockSpec((1, GDN_SB, w_), lambda b, i: (b, step(i), 0))
    gate = pl.BlockSpec((1, 1, GDN_SB, N_HEADS), lambda b, i: (b, 0, step(i), 0))
    gate_t = pl.BlockSpec((1, 1, N_HEADS, GDN_SB), lambda b, i: (b, 0, 0, step(i)))
    per_group = lambda dt: pltpu.VMEM((n_groups, GDN_SB, gw), dt)
    return pl.pallas_call(
        functools.partial(_gdn_kernel, rev=rev),
        grid=(b_, n_sb),
        in_specs=[pl.BlockSpec(masks.shape, lambda b, i: (0, 0, 0)),
                  pl.BlockSpec(expand.shape, lambda b, i: (0, 0)), tok, tok, tok, gate, gate, gate_t],
        out_specs=tok,
        out_shape=jax.ShapeDtypeStruct((b_, s_, w_), jnp.float32),
        scratch_shapes=[pltpu.VMEM((n_groups, gw, gw), jnp.float32),
                        pltpu.VMEM((n_groups, GDN_HG * GDN_SB, gw), jnp.bfloat16),
                        per_group(jnp.float32), per_group(jnp.bfloat16), per_group(jnp.bfloat16),
                        per_group(jnp.bfloat16),
                        pltpu.VMEM((N_HEADS, GDN_SB, GDN_SB), jnp.bfloat16)],
        compiler_params=pltpu.CompilerParams(dimension_semantics=("arbitrary", "arbitrary"),
                                             vmem_limit_bytes=GDN_VMEM_LIMIT_BYTES),
        name="gdn_scan_rev" if rev else "gdn_scan_fwd",
    )(masks, expand, q, k, v, beta, g, g.transpose(0, 1, 3, 2))


def gdn_mixer(proj, conv_w, a_log, dt_bias, norm_g):
    b_, s_, _ = proj.shape
    q, k, v = _conv_prep(proj, PROJ_DEST["a_qkv"][1], conv_w, ("silu_l2_scaled", "silu_l2", "silu"))
    gates = proj[:, :, PROJ_DEST["a_beta"][1]:PROJ_DEST["a_beta"][1] + 4 * N_HEADS]
    b_in = gates[:, :, :2 * N_HEADS].reshape(b_, s_, 2, N_HEADS)
    a_in = gates[:, :, 2 * N_HEADS:].reshape(b_, s_, 2, N_HEADS)
    beta = jax.nn.sigmoid(b_in).transpose(0, 2, 1, 3)
    g = -jnp.exp(a_log.astype(jnp.float32)) * jax.nn.softplus(a_in + dt_bias.astype(jnp.float32))
    g = g.transpose(0, 2, 1, 3)
    o_fwd = _gdn_scan(q, k, v, beta[:, 0:1], g[:, 0:1], False)
    o_bwd = _gdn_scan(q, k, v, beta[:, 1:2], g[:, 1:2], True)
    return _gdn_post(o_fwd, o_bwd, proj, PROJ_DEST["a_z"][1], norm_g)


def hyena_pos_features(length):
    t = jnp.linspace(0.0, 1.0, length, dtype=jnp.float32)[:, None]
    w = 2.0 * math.pi * jnp.arange(length, dtype=jnp.float32) / length
    f = jnp.linspace(1e-4, HY_BANDS - 1, HY_BANDS, dtype=jnp.float32)
    fw = w[:, None] * f[None, :]
    return jnp.concatenate([t, jnp.cos(fw), -jnp.sin(fw)], axis=-1)


def hyena_filters_f(z, w1, b1, f1, w2, b2, f2, w3, deltas):
    f32 = jnp.float32
    length = z.shape[0]
    t = z[:, :1]
    h = jnp.sin(f1.astype(f32) * (z @ w1.astype(f32) + b1.astype(f32)))
    h = jnp.sin(f2.astype(f32) * (h @ w2.astype(f32) + b2.astype(f32)))
    h = (h @ w3.astype(f32)) * jnp.exp(-t * jnp.abs(deltas.astype(f32)))
    tot = jnp.sum(jnp.abs(h), axis=0).reshape(HY_ORDER, 2, GROUP_W)
    tot = tot[:, 0] + tot[:, 1] - jnp.abs(h[0]).reshape(HY_ORDER, 2, GROUP_W)[:, 1]
    h = h / jnp.repeat(tot + EPS, 2, axis=0).reshape(1, HY_ORDER * 2 * GROUP_W)
    h = h.reshape(length, HY_ORDER, 2, GROUP_W)
    return jnp.concatenate([h[:, :, 0], jnp.zeros((1, HY_ORDER, GROUP_W), f32),
                            h[:0:-1, :, 1]], axis=0)


def _dft_tables(n_fft):
    n1 = n_fft // FFT_N2
    def dft(n):
        kk = (np.arange(n)[:, None] * np.arange(n)[None, :]) % n
        ang = -2.0 * np.pi * kk / n
        return np.cos(ang), np.sin(ang)
    f1r, f1i = dft(n1)
    f2r, f2i = dft(FFT_N2)
    kk = (np.arange(n1)[:, None] * np.arange(FFT_N2)[None, :]) % n_fft
    tw = -2.0 * np.pi * kk / n_fft
    f32 = np.float32
    return dict(
        f1=np.concatenate([f1r, f1i], axis=0).astype(f32),
        f1_inv=(np.concatenate([f1r[:n1 // 2], f1i[:n1 // 2]], axis=0) / n_fft).astype(f32),
        f2=np.concatenate([f2r, f2i], axis=0).astype(f32),
        twr=np.cos(tw).astype(f32)[:, :, None], twi=np.sin(tw).astype(f32)[:, :, None])


def _dot_f32(a, b):
    return jnp.dot(a, b, precision=lax.Precision.HIGHEST, preferred_element_type=jnp.float32)


def _split_lhs(f):
    f = jnp.asarray(f, jnp.float32)
    hi = f.astype(jnp.bfloat16)
    lo = (f - hi.astype(jnp.float32)).astype(jnp.bfloat16)
    return jnp.concatenate([hi, hi, lo], axis=1)


def _dot_split(f3, x):
    hi = x.astype(jnp.bfloat16)
    lo = (x - hi.astype(jnp.float32)).astype(jnp.bfloat16)
    return jnp.dot(f3, jnp.concatenate([hi, lo, hi], axis=0), preferred_element_type=jnp.float32)


def _fft_stage1_kernel(f_ref, zr_ref, zi_ref, yr_ref, yi_ref, *, n1):
    f = f_ref[...]
    p = _dot_split(f, zr_ref[0])
    if zi_ref is None:
        yr_ref[...] = p[:n1]
        yi_ref[...] = p[n1:]
    else:
        q = _dot_split(f, zi_ref[0])
        yr_ref[...] = p[:n1] - q[n1:]
        yi_ref[...] = q[:n1] + p[n1:]


def _fft_stage1_real_kernel(f_ref, zr_ref, yr_ref, yi_ref, *, n1):
    _fft_stage1_kernel(f_ref, zr_ref, None, yr_ref, yi_ref, n1=n1)


def _fft_mid_kernel(f_ref, twr_ref, twi_ref, yr_ref, yi_ref, kr_ref, ki_ref, qr_ref, qi_ref):
    n2 = FFT_N2
    twr, twi = twr_ref[0], twi_ref[0]
    yr, yi = yr_ref[0], yi_ref[0]
    f = f_ref[...]
    p = _dot_split(f, yr * twr - yi * twi)
    q = _dot_split(f, yr * twi + yi * twr)
    xr = p[:n2] - q[n2:]
    xi = q[:n2] + p[n2:]
    if kr_ref is None:
        qr_ref[0] = xr
        qi_ref[0] = xi
        return
    kr, ki = kr_ref[0], ki_ref[0]
    p = _dot_split(f, xr * kr - xi * ki)
    q = _dot_split(f, xr * ki + xi * kr)
    wr = p[:n2] + q[n2:]
    wi = q[:n2] - p[n2:]
    qr_ref[0] = wr * twr + wi * twi
    qi_ref[0] = wi * twr - wr * twi


def _fft_mid_spectrum_kernel(f_ref, twr_ref, twi_ref, yr_ref, yi_ref, qr_ref, qi_ref):
    _fft_mid_kernel(f_ref, twr_ref, twi_ref, yr_ref, yi_ref, None, None, qr_ref, qi_ref)


def _fft_last_kernel(f_ref, qr_ref, qi_ref, u_ref, gate_ref, bias_ref, o_ref, *, nh):
    f = f_ref[...]
    p = _dot_split(f, qr_ref[...])
    q = _dot_split(f, qi_ref[...])
    bias = bias_ref[...]
    o_ref[0] = gate_ref[0] * (p[:nh] + q[nh:] + u_ref[0] * bias)
    o_ref[1] = gate_ref[1] * (q[:nh] - p[nh:] + u_ref[1] * bias)


def _fft_params(n_axes):
    return pltpu.CompilerParams(dimension_semantics=("arbitrary",) * n_axes,
                                vmem_limit_bytes=FFT_VMEM_LIMIT_BYTES)


def _fft_forward(tab, z, n_ch):
    n1 = tab["f1"].shape[1]
    parts, rows, cols = z.shape
    tn = min(FFT_COL_TILE, cols)
    f1 = _split_lhs(tab["f1"][:, :rows])
    y_shape = jax.ShapeDtypeStruct((n1, cols), jnp.float32)
    col_spec = pl.BlockSpec((n1, tn), lambda j: (0, j))
    z_specs = [pl.BlockSpec((1, rows, tn), lambda j, p=p: (p, 0, j)) for p in range(parts)]
    body = _fft_stage1_kernel if parts == 2 else _fft_stage1_real_kernel
    yr, yi = pl.pallas_call(
        functools.partial(body, n1=n1), grid=(cols // tn,),
        in_specs=[pl.BlockSpec(f1.shape, lambda j: (0, 0))] + z_specs,
        out_specs=[col_spec, col_spec], out_shape=[y_shape, y_shape],
        compiler_params=_fft_params(1), name="fft_stage1",
    )(f1, *([z] * parts))
    return yr.reshape(n1, FFT_N2, n_ch), yi.reshape(n1, FFT_N2, n_ch)


def _fft_mid(tab, yr, yi, kr=None, ki=None, k_block=0):
    n1, n2, n_ch = yr.shape
    slab = pl.BlockSpec((1, n2, n_ch), lambda i: (i, 0, 0))
    tw_spec = pl.BlockSpec((1, n2, 1), lambda i: (i, 0, 0))
    f2 = _split_lhs(tab["f2"])
    ops = [f2, jnp.asarray(tab["twr"]), jnp.asarray(tab["twi"]), yr, yi]
    specs = [pl.BlockSpec(f2.shape, lambda i: (0, 0)), tw_spec, tw_spec, slab, slab]
    body = _fft_mid_spectrum_kernel
    if kr is not None:
        ops += [kr, ki]
        specs += [pl.BlockSpec((1, n2, n_ch), lambda i: (i, 0, k_block))] * 2
        body = _fft_mid_kernel
    shape = jax.ShapeDtypeStruct((n1, n2, n_ch), jnp.float32)
    return pl.pallas_call(
        body, grid=(n1,), in_specs=specs, out_specs=[slab, slab], out_shape=[shape, shape],
        compiler_params=_fft_params(1), name="fft_mid",
    )(*ops)


def _fft_conv_gate(tab, u, gate, bias, kr, ki, k_block):
    b_, length, n_ch = u.shape
    assert b_ == 2
    n1 = tab["f1"].shape[1]
    nh = n1 // 2
    cols = FFT_N2 * n_ch
    uv = u.reshape(b_, nh, cols)
    yr, yi = _fft_forward(tab, uv, n_ch)
    qr, qi = _fft_mid(tab, yr, yi, kr, ki, k_block)
    tn = min(FFT_COL_TILE, cols)
    f1_inv = _split_lhs(tab["f1_inv"])
    q_spec = pl.BlockSpec((n1, tn), lambda j: (0, j))
    u_spec = pl.BlockSpec((b_, nh, tn), lambda j: (0, 0, j))
    out = pl.pallas_call(
        functools.partial(_fft_last_kernel, nh=nh), grid=(cols // tn,),
        in_specs=[pl.BlockSpec(f1_inv.shape, lambda j: (0, 0)), q_spec, q_spec, u_spec, u_spec,
                  pl.BlockSpec((1, tn), lambda j: (0, 0))],
        out_specs=u_spec, out_shape=jax.ShapeDtypeStruct((b_, nh, cols), jnp.float32),
        compiler_params=_fft_params(1), name="fft_last",
    )(f1_inv, qr.reshape(n1, cols), qi.reshape(n1, cols), uv, gate.reshape(b_, nh, cols),
      jnp.tile(bias.astype(jnp.float32), tn // n_ch).reshape(1, tn))
    return out.reshape(b_, length, n_ch)


def hyena_mixer(proj, conv_w, kern, bias):
    dtype = proj.dtype
    length = proj.shape[1]
    x1, x2, v = _conv_prep(proj, PROJ_DEST["b_in"][1], conv_w, ("plain",) * 3)
    tab = _dft_tables(2 * length)
    n1 = tab["f1"].shape[1]
    n_filt = HY_ORDER * GROUP_W
    kr, ki = _fft_mid(tab, *_fft_forward(tab, kern.reshape(1, n1, FFT_N2 * n_filt), n_filt))
    y = _fft_conv_gate(tab, v, x1, bias[0], kr, ki, 0)
    y = _fft_conv_gate(tab, y, x2, bias[1], kr, ki, 1)
    return y.astype(dtype)


def alibi_slopes(n):
    return 2.0 ** (-8.0 * jnp.arange(1, n + 1, dtype=jnp.float32) / n)


def _window_kernel(q_ref, kp_ref, kc_ref, kn_ref, bias_ref, sink_ref, g_ref, o_ref, *, nb):
    bf16 = jnp.bfloat16
    nt = (((1,), (1,)), ((), ()))
    col = lax.broadcasted_iota(jnp.int32, (1, 3 * BLOCK), 1)
    q_all = q_ref[0] * (HEAD_DIM ** -0.5)
    kv_all = jnp.concatenate([kp_ref[0], kc_ref[0], kn_ref[0]], axis=0)
    chains = [(u, j) for u in range(WINDOW_SUB) for j in range(N_KV_HEADS)]
    s_list, v_list = [], []
    for u, j in chains:
        n = pl.program_id(1) * WINDOW_SUB + u
        edge = jnp.where(((n == 0) & (col < BLOCK)) | ((n == nb - 1) & (col >= 2 * BLOCK)), WINDOW_MASK, 0.0)
        q = q_all[u * BLOCK:(u + 1) * BLOCK]
        kv = kv_all[u * BLOCK:(u + 3) * BLOCK]
        q4 = jnp.concatenate([q[:, (j * GQA_GROUP + g) * HEAD_DIM:(j * GQA_GROUP + g + 1) * HEAD_DIM]
                              for g in range(GQA_GROUP)], axis=0)
        k = kv[:, j * HEAD_DIM:(j + 1) * HEAD_DIM]
        v_list.append(kv[:, KV_W + j * HEAD_DIM:KV_W + (j + 1) * HEAD_DIM].astype(bf16))
        s = lax.dot_general(q4.astype(bf16), k.astype(bf16), nt, preferred_element_type=jnp.float32)
        s_list.append(s + bias_ref[j] + edge)
    m_list = [jnp.maximum(jnp.max(s, axis=-1, keepdims=True), sink_ref[j]) for s, (u, j) in zip(s_list, chains)]
    p_list = [jnp.exp(s - m) for s, m in zip(s_list, m_list)]
    d_list = [jnp.sum(p, axis=-1, keepdims=True) + jnp.exp(sink_ref[j] - m)
              for p, m, (u, j) in zip(p_list, m_list, chains)]
    o_list = [jnp.dot(p.astype(bf16), v, preferred_element_type=jnp.float32) / d
              for p, v, d in zip(p_list, v_list, d_list)]
    for u in range(WINDOW_SUB):
        outs = []
        for j in range(N_KV_HEADS):
            o = o_list[u * N_KV_HEADS + j]
            outs += [o[g * BLOCK:(g + 1) * BLOCK] for g in range(GQA_GROUP)]
        o = jnp.concatenate(outs, axis=-1)
        o_ref[0, u * BLOCK:(u + 1) * BLOCK, :] = (
            o * lax.rsqrt(jnp.mean(o * o, axis=-1, keepdims=True) + EPS) * g_ref[...])


def window_mixer(proj, q_col, kv_col, sink, slopes, norm_g):
    b_, s_, _ = proj.shape
    assert q_col % GROUP_W == 0 and kv_col % (2 * KV_W) == 0
    nb = s_ // BLOCK
    sub = WINDOW_SUB
    n_steps = nb // sub
    rel = BLOCK + jnp.arange(BLOCK)[:, None] - jnp.arange(3 * BLOCK)[None, :]
    dist = jnp.abs(rel).astype(jnp.float32)
    bias = jnp.where(jnp.abs(rel) <= WINDOW, -slopes[:, None, None] * dist, WINDOW_MASK)
    bias = bias.reshape(N_KV_HEADS, GQA_GROUP * BLOCK, 3 * BLOCK)
    sink_rows = jnp.repeat(sink.astype(jnp.float32), BLOCK).reshape(N_KV_HEADS, GQA_GROUP * BLOCK, 1)
    kc = kv_col // (2 * KV_W)
    const = lambda shape: pl.BlockSpec(shape, lambda b, n: (0,) * len(shape))
    rows = pl.BlockSpec((1, sub * BLOCK, GROUP_W), lambda b, n: (b, n, q_col // GROUP_W))
    return pl.pallas_call(
        functools.partial(_window_kernel, nb=nb),
        grid=(b_, n_steps),
        in_specs=[rows,
                  pl.BlockSpec((1, BLOCK, 2 * KV_W), lambda b, n: (b, jnp.maximum(n * sub - 1, 0), kc)),
                  pl.BlockSpec((1, sub * BLOCK, 2 * KV_W), lambda b, n: (b, n, kc)),
                  pl.BlockSpec((1, BLOCK, 2 * KV_W), lambda b, n: (b, jnp.minimum((n + 1) * sub, nb - 1), kc)),
                  const(bias.shape), const(sink_rows.shape), const((1, GROUP_W))],
        out_specs=pl.BlockSpec((1, sub * BLOCK, GROUP_W), lambda b, n: (b, n, 0)),
        out_shape=jax.ShapeDtypeStruct((b_, s_, GROUP_W), jnp.float32),
        compiler_params=pltpu.CompilerParams(dimension_semantics=("arbitrary", "arbitrary")),
        name="window_attention",
    )(proj, proj, proj, proj, bias, sink_rows, norm_g.astype(jnp.float32).reshape(1, GROUP_W))


def _rope_tables(row_idx, col_idx, n_heads):
    half = HEAD_DIM // 2
    inv = ROPE_THETA ** (-jnp.arange(0, half, 2, dtype=jnp.float32) / half)

    def tabs(pos):
        ang = pos.astype(jnp.float32)[:, None] * inv[None, :]
        c, sn = jnp.cos(ang), jnp.sin(ang)
        z = jnp.zeros_like(sn)
        return (jnp.concatenate([c, c], -1), jnp.concatenate([-sn, z], -1), jnp.concatenate([z, sn], -1))

    per_head = [jnp.concatenate([r, c], -1) for r, c in zip(tabs(row_idx), tabs(col_idx))]
    return tuple(jnp.tile(t, (1, n_heads)) for t in per_head)


def _global_prep_kernel(q_ref, kv_ref, c_ref, s1_ref, s2_ref, gq_ref, gk_ref, bdq_ref, bdk_ref,
                        qa_ref, kt_ref, va_ref, kn2_ref):
    f32 = jnp.float32
    bf16 = jnp.bfloat16
    hd = HEAD_DIM
    quarter = hd // 4

    def norm_rope(x, g, bd):
        w = x.shape[1]
        y = x * lax.rsqrt(_group_sum(x * x, bd) * (1.0 / hd) + EPS) * g
        return (y * c_ref[:, :w] + pltpu.roll(y, w - quarter, 1) * s1_ref[:, :w]
                + pltpu.roll(y, quarter, 1) * s2_ref[:, :w])

    lane = lax.broadcasted_iota(jnp.int32, (1, 2 * hd), 1)

    def pair_slot(x2, j, tail):
        first = x2 if j == 0 else pltpu.roll(x2, hd, 1)
        return jnp.where(lane < hd, first, tail)

    q = norm_rope(q_ref[0], gq_ref[...], bdq_ref[...]) * (LOG2E * hd ** -0.5)
    qf = q.astype(bf16).astype(f32)
    qn = jnp.sqrt(_group_sum(qf * qf, bdq_ref[...]))
    for h in range(N_HEADS):
        p2 = slice((h // 2) * 2 * hd, (h // 2 + 1) * 2 * hd)
        n2 = qn[:, p2] if h % 2 == 1 else pltpu.roll(qn[:, p2], hd, 1)
        tail = jnp.where(lane == hd, -n2, 0.0)
        qa_ref[0, h // GQA_GROUP, h % GQA_GROUP] = pair_slot(qf[:, p2], h % 2, tail).astype(bf16)
    kv = kv_ref[0]
    k = norm_rope(kv[:, :KV_W], gk_ref[...], bdk_ref[...])
    kf = k.astype(bf16).astype(f32)
    kn2_ref[0] = _group_sum(kf * kf, bdk_ref[...])
    v = kv[:, KV_W:]
    one = jnp.where(lane == hd, 1.0, 0.0)
    for j in range(N_KV_HEADS):
        kt_ref[0, j] = pair_slot(kf, j, 0.0).T.astype(bf16)
        va_ref[0, j] = pair_slot(v, j, one).astype(bf16)


def _global_prep(proj, q_col, kv_col, tables, q_norm_g, k_norm_g):
    b_, s_, _ = proj.shape
    assert KV_W == 2 * HEAD_DIM and q_col % GROUP_W == 0 and kv_col % (2 * KV_W) == 0
    tm = CONV_TM
    wide = 2 * HEAD_DIM
    bf16 = jnp.bfloat16
    const = lambda a: pl.BlockSpec(a.shape, lambda b, i: (0,) * a.ndim)
    tab = pl.BlockSpec((tm, GROUP_W), lambda b, i: (i, 0))
    gq = jnp.tile(q_norm_g.astype(jnp.float32), N_HEADS).reshape(1, GROUP_W)
    gk = jnp.tile(k_norm_g.astype(jnp.float32), N_KV_HEADS).reshape(1, KV_W)
    bdq, bdk = _group_ones(GROUP_W), _group_ones(KV_W)
    return pl.pallas_call(
        _global_prep_kernel, grid=(b_, s_ // tm),
        in_specs=[pl.BlockSpec((1, tm, GROUP_W), lambda b, i: (b, i, q_col // GROUP_W)),
                  pl.BlockSpec((1, tm, 2 * KV_W), lambda b, i: (b, i, kv_col // (2 * KV_W))),
                  tab, tab, tab, const(gq), const(gk), const(bdq), const(bdk)],
        out_specs=[pl.BlockSpec((1, N_KV_HEADS, GQA_GROUP, tm, wide), lambda b, i: (b, 0, 0, i, 0)),
                   pl.BlockSpec((1, N_KV_HEADS, wide, tm), lambda b, i: (b, 0, 0, i)),
                   pl.BlockSpec((1, N_KV_HEADS, tm, wide), lambda b, i: (b, 0, i, 0)),
                   pl.BlockSpec((1, tm, KV_W), lambda b, i: (b, i, 0))],
        out_shape=[jax.ShapeDtypeStruct((b_, N_KV_HEADS, GQA_GROUP, s_, wide), bf16),
                   jax.ShapeDtypeStruct((b_, N_KV_HEADS, wide, s_), bf16),
                   jax.ShapeDtypeStruct((b_, N_KV_HEADS, s_, wide), bf16),
                   jax.ShapeDtypeStruct((b_, s_, KV_W), jnp.float32)],
        compiler_params=pltpu.CompilerParams(dimension_semantics=("arbitrary", "arbitrary")),
        name="global_prep",
    )(proj, proj, *tables, gq, gk, bdq, bdk)


def _flash_kernel(q_ref, kt_ref, v_ref, o_ref, *, tq, tk, n_kc):
    m_rows = GQA_GROUP * tq
    q = q_ref[0, 0].reshape(m_rows, 2 * HEAD_DIM)

    def body(c, acc):
        off = pl.multiple_of(c * tk, tk)
        s = jnp.dot(q, kt_ref[0, 0, :, pl.ds(off, tk)], preferred_element_type=jnp.float32)
        p = jnp.exp2(s).astype(jnp.bfloat16)
        return acc + jnp.dot(p, v_ref[0, 0, pl.ds(off, tk), :], preferred_element_type=jnp.float32)

    acc = lax.fori_loop(0, n_kc, body, jnp.zeros((m_rows, 2 * HEAD_DIM), jnp.float32))
    o = acc[:, :HEAD_DIM] / acc[:, HEAD_DIM:HEAD_DIM + 1]
    o_ref[0] = jnp.concatenate([o[g * tq:(g + 1) * tq] for g in range(GQA_GROUP)], axis=-1)


def _rowmax_kernel(q_ref, kt_ref, m_ref, *, tq, tk, n_kc):
    m_rows = GQA_GROUP * tq
    q = q_ref[0, 0].reshape(m_rows, 2 * HEAD_DIM)

    def body(c, mx):
        off = pl.multiple_of(c * tk, tk)
        s = jnp.dot(q, kt_ref[0, 0, :, pl.ds(off, tk)], preferred_element_type=jnp.float32)
        for j in range(tk // 128):
            mx = jnp.maximum(mx, s[:, j * 128:(j + 1) * 128])
        return mx

    mx = lax.fori_loop(0, n_kc, body, jnp.full((m_rows, 128), -jnp.inf, jnp.float32))
    m_ref[0, 0] = jnp.max(mx, axis=-1, keepdims=True).reshape(GQA_GROUP, tq, 1)


def _attn_call(body, q, kt, v, name, *, tq=ATTN_TQ, tk=ATTN_TK):
    b_, _, _, s_, _ = q.shape
    wide = 2 * HEAD_DIM
    q_spec = pl.BlockSpec((1, 1, GQA_GROUP, tq, wide), lambda b, h, i: (b, h, 0, i, 0))
    kt_spec = pl.BlockSpec((1, 1, wide, s_), lambda b, h, i: (b, h, 0, 0))
    v_spec = pl.BlockSpec((1, 1, s_, wide), lambda b, h, i: (b, h, 0, 0))
    if v is None:
        operands, in_specs = (q, kt), [q_spec, kt_spec]
        o_spec = pl.BlockSpec((1, 1, GQA_GROUP, tq, 1), lambda b, h, i: (b, h, 0, i, 0))
        o_shape = (b_, N_KV_HEADS, GQA_GROUP, s_, 1)
    else:
        operands, in_specs = (q, kt, v), [q_spec, kt_spec, v_spec]
        o_spec = pl.BlockSpec((1, tq, GQA_GROUP * HEAD_DIM), lambda b, h, i: (b, i, h))
        o_shape = (b_, s_, N_HEADS * HEAD_DIM)
    return pl.pallas_call(
        functools.partial(body, tq=tq, tk=tk, n_kc=s_ // tk),
        out_shape=jax.ShapeDtypeStruct(o_shape, jnp.float32),
        grid=(b_, N_KV_HEADS, s_ // tq),
        in_specs=in_specs, out_specs=o_spec,
        compiler_params=pltpu.CompilerParams(
            dimension_semantics=("arbitrary", "arbitrary", "arbitrary"),
            vmem_limit_bytes=ATTN_VMEM_LIMIT_BYTES),
        name=name,
    )(*operands)


def global_mixer(proj, tables, q_norm_g, k_norm_g):
    bf16 = jnp.bfloat16
    hd = HEAD_DIM
    qa, kt, va, kn2 = _global_prep(proj, PROJ_DEST["d_q"][1], PROJ_DEST["d_kv"][1], tables,
                                   q_norm_g, k_norm_g)
    kn = jnp.sqrt(jnp.max(kn2, axis=1)[:, ::hd])
    c = (kn * ATTN_ROUND_UP).astype(bf16)
    qmax = jnp.max(-qa[:, :, :, :, hd].astype(jnp.float32), axis=(2, 3))
    with_key_row = lambda row: kt.at[:, :, hd, :].set(jnp.broadcast_to(row[:, :, None], kt.shape[:2] + kt.shape[3:]))

    def exact_shift():
        m = _attn_call(_rowmax_kernel, qa, kt, None, "global_attention_rowmax")
        return qa.at[:, :, :, :, hd].set((-m[..., 0]).astype(bf16)), with_key_row(jnp.ones_like(c))

    qa, kta = lax.cond(jnp.max(qmax * c.astype(jnp.float32)) < ATTN_SAFE_SHIFT,
                       lambda: (qa, with_key_row(c)), exact_shift)
    return _attn_call(_flash_kernel, qa, kta, va, "global_flash_attention")


def _row_copy(src, src_row, dst, dst_row, sem):
    return pltpu.make_async_copy(src.at[pl.ds(src_row, 1)], dst.at[pl.ds(dst_row, 1)], sem)


def _moe_dispatch_kernel(pos_ref, x_ref, g_ref, zeros_hbm, xs_hbm, xn_scr, sem, *, tb, n_steps):
    del zeros_hbm
    i = pl.program_id(0)
    slot = i % 2

    def wait_slot(s):
        for _ in range(TOP_K):
            pltpu.make_async_copy(xn_scr.at[s], xn_scr.at[s], sem.at[s]).wait()

    @pl.when(i >= 2)
    def _():
        wait_slot(slot)

    x = x_ref[...]
    xn_scr[slot] = x * lax.rsqrt(jnp.mean(x * x, axis=-1, keepdims=True) + EPS) * g_ref[...]

    def issue(r, carry):
        for k in range(TOP_K):
            _row_copy(xn_scr.at[slot], r, xs_hbm, pos_ref[(i * tb + r) * TOP_K + k], sem.at[slot]).start()
        return carry

    lax.fori_loop(0, tb, issue, 0, unroll=8)

    @pl.when(i == n_steps - 1)
    def _():
        if n_steps >= 2:
            wait_slot(1 - slot)
        wait_slot(slot)


def _moe_expert_kernel(blk_e_ref, blk_rows_ref, xs_ref, wg_ref, wu_ref, wd_ref, o_ref, *, tm):
    del blk_e_ref
    rows = blk_rows_ref[pl.program_id(0)]

    @pl.when(rows > 0)
    def _():
        x = xs_ref[...].astype(jnp.bfloat16)
        g = jnp.dot(x, wg_ref[0], preferred_element_type=jnp.float32)
        u = jnp.dot(x, wu_ref[0], preferred_element_type=jnp.float32)
        h = (g * jax.nn.sigmoid(g) * u).astype(jnp.bfloat16)
        o_ref[...] = jnp.dot(h, wd_ref[0], preferred_element_type=jnp.float32)

    @pl.when(rows == 0)
    def _():
        o_ref[...] = jnp.zeros((tm, o_ref.shape[1]), jnp.float32)


def _moe_combine_kernel(pos_ref, x_ref, w_ref, g_ref, os_hbm, y_ref, buf, sem, *, tb, n_steps, out_norm):
    i = pl.program_id(0)
    slot = i % 2

    def fetch(step, s):
        def issue(r, carry):
            for k in range(TOP_K):
                _row_copy(os_hbm, pos_ref[(step * tb + r) * TOP_K + k], buf.at[s, k], r, sem.at[s]).start()
            return carry

        lax.fori_loop(0, tb, issue, 0, unroll=8)

    @pl.when(i == 0)
    def _():
        fetch(0, 0)

    @pl.when(i + 1 < n_steps)
    def _():
        fetch(i + 1, 1 - slot)

    for k in range(TOP_K):
        pltpu.make_async_copy(buf.at[slot, k], buf.at[slot, k], sem.at[slot]).wait()
    w = w_ref[...]
    y = x_ref[...]
    for k in range(TOP_K):
        y = y + w[:, k:k + 1] * buf[slot, k]
    if out_norm:
        y = y * lax.rsqrt(jnp.mean(y * y, axis=-1, keepdims=True) + EPS) * g_ref[...]
    y_ref[...] = y


def _router_kernel(x_ref, g_ref, w_ref, o_ref):
    x = x_ref[...]
    xn = x * lax.rsqrt(jnp.mean(x * x, axis=-1, keepdims=True) + EPS) * g_ref[...]
    o_ref[...] = jnp.dot(xn.astype(jnp.bfloat16), w_ref[...], preferred_element_type=jnp.float32)


def _router_logits(xt, norm_g, w_group, w_expert):
    n_tok, d_ = xt.shape
    n_log = N_EXPERT_GROUPS + N_EXPERTS
    w = jnp.concatenate([w_group, w_expert, jnp.zeros((d_, ROUTER_W - n_log), w_group.dtype)], axis=1)
    return pl.pallas_call(
        _router_kernel, grid=(n_tok // CONV_TM,),
        in_specs=[pl.BlockSpec((CONV_TM, d_), lambda i: (i, 0)), pl.BlockSpec((1, d_), lambda i: (0, 0)),
                  pl.BlockSpec((d_, ROUTER_W), lambda i: (0, 0))],
        out_specs=pl.BlockSpec((CONV_TM, ROUTER_W), lambda i: (i, 0)),
        out_shape=jax.ShapeDtypeStruct((n_tok, ROUTER_W), jnp.float32),
        compiler_params=pltpu.CompilerParams(dimension_semantics=("arbitrary",)),
        name="moe_router",
    )(xt, norm_g.astype(jnp.float32).reshape(1, d_), w.astype(jnp.bfloat16))


def _moe_route(logits, b_group, b_expert, tm):
    n_tok = logits.shape[0]
    gp = jax.nn.softmax(logits[:, :N_EXPERT_GROUPS] + b_group.astype(jnp.float32), axis=-1)
    g_idx = jnp.argmax(gp, axis=-1, keepdims=True)
    g_w = jnp.max(gp, axis=-1, keepdims=True)
    elog = logits[:, N_EXPERT_GROUPS:N_EXPERT_GROUPS + N_EXPERTS] + b_expert.astype(jnp.float32)
    elog = elog.reshape(n_tok, N_EXPERT_GROUPS, EXPERTS_PER_GROUP)
    elog_sel = jnp.take_along_axis(elog, g_idx[:, :, None], axis=1)[:, 0]
    e_w, e_idx = lax.top_k(jax.nn.softmax(elog_sel, axis=-1), TOP_K)
    e_w = e_w / jnp.sum(e_w, axis=-1, keepdims=True)
    weights = g_w * e_w
    experts = (g_idx * EXPERTS_PER_GROUP + e_idx).reshape(-1)
    onehot = (experts[:, None] == jnp.arange(N_EXPERTS)[None, :]).astype(jnp.int32)
    csum = jnp.cumsum(onehot, axis=0)
    counts = csum[-1]
    rank = jnp.sum(onehot * csum, axis=1) - 1
    padded = ((counts + tm - 1) // tm) * tm
    pend = jnp.cumsum(padded)
    pstart = pend - padded
    pos = (pstart[experts] + rank).astype(jnp.int32)
    n_blk = (n_tok * TOP_K) // tm + N_EXPERTS
    blk_start = jnp.arange(n_blk) * tm
    blk_e = jnp.minimum(jnp.sum(pend[None, :] <= blk_start[:, None], axis=1), N_EXPERTS - 1).astype(jnp.int32)
    blk_rows = jnp.clip(counts[blk_e] - (blk_start - pstart[blk_e]), 0, tm).astype(jnp.int32)
    return weights, pos, blk_e, blk_rows, n_blk


def hier_moe_residual(x, norm_g, w_group, b_group, w_expert, b_expert, w_gate, w_up, w_down,
                      out_norm_g=None, *, tm=MOE_TM, tb=MOE_TB):
    b_, s_, d_ = x.shape
    n_tok = b_ * s_
    xt = x.reshape(n_tok, d_)
    weights, pos, blk_e, blk_rows, n_blk = _moe_route(_router_logits(xt, norm_g, w_group, w_expert),
                                                      b_group, b_expert, tm)
    n_pad = n_blk * tm
    vmem = pltpu.CompilerParams(dimension_semantics=("arbitrary",),
                                vmem_limit_bytes=MOE_VMEM_LIMIT_BYTES)
    xs = pl.pallas_call(
        functools.partial(_moe_dispatch_kernel, tb=tb, n_steps=n_tok // tb),
        grid_spec=pltpu.PrefetchScalarGridSpec(
            num_scalar_prefetch=1, grid=(n_tok // tb,),
            in_specs=[pl.BlockSpec((tb, d_), lambda i, pos: (i, 0)),
                      pl.BlockSpec((1, d_), lambda i, pos: (0, 0)),
                      pl.BlockSpec(memory_space=pl.ANY)],
            out_specs=pl.BlockSpec(memory_space=pl.ANY),
            scratch_shapes=[pltpu.VMEM((2, tb, d_), jnp.float32), pltpu.SemaphoreType.DMA((2,))]),
        out_shape=jax.ShapeDtypeStruct((n_pad, d_), jnp.float32),
        input_output_aliases={3: 0},
        compiler_params=vmem, name="moe_dispatch",
    )(pos, xt, norm_g.astype(jnp.float32).reshape(1, d_), jnp.zeros((n_pad, d_), jnp.float32))
    bf16 = jnp.bfloat16
    outs = pl.pallas_call(
        functools.partial(_moe_expert_kernel, tm=tm),
        grid_spec=pltpu.PrefetchScalarGridSpec(
            num_scalar_prefetch=2, grid=(n_blk,),
            in_specs=[pl.BlockSpec((tm, d_), lambda i, be, br: (i, 0)),
                      pl.BlockSpec((1, d_, D_EXPERT), lambda i, be, br: (be[i], 0, 0)),
                      pl.BlockSpec((1, d_, D_EXPERT), lambda i, be, br: (be[i], 0, 0)),
                      pl.BlockSpec((1, D_EXPERT, d_), lambda i, be, br: (be[i], 0, 0))],
            out_specs=pl.BlockSpec((tm, d_), lambda i, be, br: (i, 0))),
        out_shape=jax.ShapeDtypeStruct((n_pad, d_), jnp.float32),
        compiler_params=vmem, name="moe_experts",
    )(blk_e, blk_rows, xs, w_gate.astype(bf16), w_up.astype(bf16), w_down.astype(bf16))
    y = pl.pallas_call(
        functools.partial(_moe_combine_kernel, tb=tb, n_steps=n_tok // tb, out_norm=out_norm_g is not None),
        grid_spec=pltpu.PrefetchScalarGridSpec(
            num_scalar_prefetch=1, grid=(n_tok // tb,),
            in_specs=[pl.BlockSpec((tb, d_), lambda i, pos: (i, 0)),
                      pl.BlockSpec((tb, TOP_K), lambda i, pos: (i, 0)),
                      pl.BlockSpec((1, d_), lambda i, pos: (0, 0)),
                      pl.BlockSpec(memory_space=pl.ANY)],
            out_specs=pl.BlockSpec((tb, d_), lambda i, pos: (i, 0)),
            scratch_shapes=[pltpu.VMEM((2, TOP_K, tb, d_), jnp.float32), pltpu.SemaphoreType.DMA((2,))]),
        out_shape=jax.ShapeDtypeStruct((n_tok, d_), jnp.float32),
        compiler_params=vmem, name="moe_combine",
    )(pos, xt, weights, (norm_g if out_norm_g is None else out_norm_g).astype(jnp.float32).reshape(1, d_), outs)
    return y.reshape(b_, s_, d_)


PROJ_DEST = {"a_qkv": (0, 0), "a_z": (1, 1536), "b_in": (4, 2048), "c_q": (5, 3584), "d_q": (7, 4096),
             "c_kv": (6, 4608), "d_kv": (8, 4864), "a_beta": (2, 5120), "a_alpha": (3, 5136)}
PROJ_WIDTH = 5632
assert PROJ_DEST["a_alpha"][1] == PROJ_DEST["a_beta"][1] + 2 * N_HEADS


def _in_proj_weight(w):
    starts = [0] + _split_points()
    cols = jnp.zeros((w.shape[0], PROJ_WIDTH), jnp.bfloat16)
    for seg, dest in PROJ_DEST.values():
        cols = lax.dynamic_update_slice(
            cols, w[:, starts[seg]:starts[seg] + IN_SPLIT_SIZES[seg]].astype(jnp.bfloat16), (0, dest))
    return cols


def _in_proj_kernel(x_ref, g_ref, w_ref, o_ref, xn_scr):
    @pl.when(pl.program_id(1) == 0)
    def _():
        x = x_ref[...]
        xn = x * lax.rsqrt(jnp.mean(x * x, axis=-1, keepdims=True) + EPS) * g_ref[...]
        xn_scr[...] = xn.astype(jnp.bfloat16)

    o_ref[...] = jnp.dot(xn_scr[...], w_ref[...], preferred_element_type=jnp.float32)


def in_proj(x, norm_g, w):
    n_tok, d_ = x.shape
    return pl.pallas_call(
        _in_proj_kernel,
        grid=(n_tok // PROJ_TM, PROJ_WIDTH // PROJ_TILE),
        in_specs=[pl.BlockSpec((PROJ_TM, d_), lambda i, j: (i, 0)),
                  pl.BlockSpec((1, d_), lambda i, j: (0, 0)),
                  pl.BlockSpec((d_, PROJ_TILE), lambda i, j: (0, j))],
        out_specs=pl.BlockSpec((PROJ_TM, PROJ_TILE), lambda i, j: (i, j)),
        out_shape=jax.ShapeDtypeStruct((n_tok, PROJ_WIDTH), jnp.float32),
        scratch_shapes=[pltpu.VMEM((PROJ_TM, d_), jnp.bfloat16)],
        compiler_params=pltpu.CompilerParams(dimension_semantics=("arbitrary", "arbitrary"),
                                             vmem_limit_bytes=PROJ_VMEM_LIMIT_BYTES),
        name="in_proj",
    )(x, norm_g.astype(jnp.float32).reshape(1, d_), _in_proj_weight(w))


def _out_proj_kernel(x_ref, *refs, normed):
    n = len(normed)
    y_refs, g_ref, w_ref, o_ref = refs[:n], refs[n], refs[n + 1], refs[n + 2]
    acc = x_ref[...]
    for k, y_ref in enumerate(y_refs):
        y = y_ref[...]
        if normed[k]:
            y = y * lax.rsqrt(jnp.mean(y * y, axis=-1, keepdims=True) + EPS) * g_ref[k:k + 1, :]
        acc = acc + jnp.dot(y.astype(jnp.bfloat16), w_ref[k * GROUP_W:(k + 1) * GROUP_W, :],
                            preferred_element_type=jnp.float32)
    o_ref[...] = acc


def out_proj_residual(x, ys, gains, w):
    n_tok, d_ = x.shape
    row = lambda width: pl.BlockSpec((OUT_TM, width), lambda i: (i, 0))
    g = jnp.stack([jnp.ones((GROUP_W,), jnp.float32) if gk is None else gk.astype(jnp.float32)
                   for gk in gains])
    return pl.pallas_call(
        functools.partial(_out_proj_kernel, normed=tuple(gk is not None for gk in gains)),
        grid=(n_tok // OUT_TM,),
        in_specs=[row(d_)] + [row(GROUP_W)] * len(ys) + [pl.BlockSpec(g.shape, lambda i: (0, 0)),
                                                          pl.BlockSpec(w.shape, lambda i: (0, 0))],
        out_specs=row(d_),
        out_shape=jax.ShapeDtypeStruct((n_tok, d_), jnp.float32),
        compiler_params=pltpu.CompilerParams(dimension_semantics=("arbitrary",),
                                             vmem_limit_bytes=PROJ_VMEM_LIMIT_BYTES),
        name="out_proj",
    )(x, *ys, g, w.astype(jnp.bfloat16))


def kernel(x, norm_mix, w_in, gdn_conv, gdn_a_log, gdn_dt_bias, gdn_norm, hy_conv, hy_w1, hy_b1, hy_freq1, hy_w2, hy_b2, hy_freq2, hy_w3, hy_deltas, hy_bias, hy_norm, swa_sink, swa_norm, ga_q_norm, ga_k_norm, ga_norm, w_out, norm_ffn, moe_w_group, moe_b_group, moe_w_expert, moe_b_expert, moe_w_gate, moe_w_up, moe_w_down, norm_final):
    b_, s_, _ = x.shape
    rows = s_ // GRID_W
    row_idx = jnp.repeat(jnp.arange(rows), GRID_W)
    col_idx = jnp.tile(jnp.arange(GRID_W), rows)
    rope_tables = _rope_tables(row_idx, col_idx, N_HEADS)
    pos_feat = hyena_pos_features(s_)
    slopes = alibi_slopes(N_HEADS)
    n_tok = b_ * s_
    for l in range(DEPTH):
        proj = in_proj(x.reshape(n_tok, D_MODEL), norm_mix[l], w_in[l]).reshape(b_, s_, PROJ_WIDTH)
        y_a = gdn_mixer(proj, gdn_conv[l], gdn_a_log[l], gdn_dt_bias[l], gdn_norm[l])
        kf = hyena_filters_f(pos_feat, hy_w1[l], hy_b1[l], hy_freq1[l], hy_w2[l], hy_b2[l],
                             hy_freq2[l], hy_w3[l], hy_deltas[l])
        y_b = hyena_mixer(proj, hy_conv[l], kf, hy_bias[l])
        y_c = window_mixer(proj, PROJ_DEST["c_q"][1], PROJ_DEST["c_kv"][1], swa_sink[l], slopes, swa_norm[l])
        y_d = global_mixer(proj, rope_tables, ga_q_norm[l], ga_k_norm[l])
        x = out_proj_residual(x.reshape(n_tok, D_MODEL),
                              [y.reshape(n_tok, GROUP_W) for y in (y_a, y_b, y_c, y_d)],
                              [None, hy_norm[l], None, ga_norm[l]], w_out[l]).reshape(b_, s_, D_MODEL)
        x = hier_moe_residual(x, norm_ffn[l], moe_w_group[l], moe_b_group[l], moe_w_expert[l],
                              moe_b_expert[l], moe_w_gate[l], moe_w_up[l], moe_w_down[l],
                              norm_final if l == DEPTH - 1 else None)
    return x
```

```python
import functools
import math

import jax
import jax.numpy as jnp
import numpy as np
from jax import lax
from jax.experimental import pallas as pl
from jax.experimental.pallas import tpu as pltpu

D_MODEL = 2048
DEPTH = 2
N_MIXERS = 4
GROUP_W = D_MODEL // N_MIXERS
HEAD_DIM = 64
N_HEADS = GROUP_W // HEAD_DIM
N_KV_HEADS = 2
GQA_GROUP = N_HEADS // N_KV_HEADS
KV_W = N_KV_HEADS * HEAD_DIM
SHORT_CONV = 3
GDN_CHUNK = 64
HY_ORDER = 2
HY_EMB = 33
HY_BANDS = (HY_EMB - 1) // 2
WINDOW = 128
BLOCK = 128
GRID_W = 64
ROPE_THETA = 10000.0
N_EXPERT_GROUPS = 4
EXPERTS_PER_GROUP = 8
N_EXPERTS = N_EXPERT_GROUPS * EXPERTS_PER_GROUP
TOP_K = 2
D_EXPERT = 512
MOE_BLOCK = 128
EPS = 1e-6
IN_SPLIT_SIZES = (3 * GROUP_W, GROUP_W, 2 * N_HEADS, 2 * N_HEADS, 3 * GROUP_W,
                  GROUP_W, 2 * KV_W, GROUP_W, 2 * KV_W)

ATTN_VMEM_LIMIT_BYTES = 48 * 1024 * 1024
ATTN_TQ = 256
ATTN_TK = 2048
WINDOW_SUB = 4
WINDOW_MASK = -1e30
GDN_VMEM_LIMIT_BYTES = 48 * 1024 * 1024
PROJ_VMEM_LIMIT_BYTES = 48 * 1024 * 1024
CONV_TM = 512
CONV_HALO = 8
PROJ_TM = 1024
PROJ_TILE = 512
OUT_TM = 256
GDN_SB = 256
GDN_INV_BASE = 8
GDN_KCHUNK = 64
GDN_HG = 4
FFT_VMEM_LIMIT_BYTES = 48 * 1024 * 1024
FFT_N2 = 256
FFT_COL_TILE = 4096
MOE_VMEM_LIMIT_BYTES = 48 * 1024 * 1024
MOE_TM = 512
MOE_TB = 256
ROUTER_W = 128
LOG2E = 1.4426950408889634
ATTN_SAFE_SHIFT = 50.0
ATTN_ROUND_UP = 1.01


def _split_points():
    return [int(v) for v in np.cumsum(IN_SPLIT_SIZES)[:-1]]


def _group_ones(width):
    r = np.arange(width) // HEAD_DIM
    return jnp.asarray(r[:, None] == r[None, :], jnp.bfloat16)


def _group_sum(x, bd):
    hi = x.astype(jnp.bfloat16)
    lo = (x - hi.astype(jnp.float32)).astype(jnp.bfloat16)
    return (jnp.dot(hi, bd, preferred_element_type=jnp.float32)
            + jnp.dot(lo, bd, preferred_element_type=jnp.float32))


def _conv_prep_kernel(*refs, modes, n_steps):
    n = len(modes)
    w_ref, bd_ref = refs[3 * n], refs[3 * n + 1]
    o_refs = refs[3 * n + 2:]
    i = pl.program_id(1)
    tm = refs[0].shape[1]
    row = lax.broadcasted_iota(jnp.int32, (tm, 1), 0)
    for t, mode in enumerate(modes):
        x_ref, prev_ref, next_ref = refs[3 * t:3 * t + 3]
        x = x_ref[0]
        prev = jnp.where(i > 0, prev_ref[0][CONV_HALO - 1:CONV_HALO], 0.0)
        nxt = jnp.where(i < n_steps - 1, next_ref[0][0:1], 0.0)
        x_prev = jnp.where(row == 0, prev, pltpu.roll(x, 1, 0))
        x_next = jnp.where(row == tm - 1, nxt, pltpu.roll(x, tm - 1, 0))
        w = w_ref[:, t * GROUP_W:(t + 1) * GROUP_W]
        y = x_prev * w[0:1] + x * w[1:2] + x_next * w[2:3]
        if mode != "plain":
            y = y * jax.nn.sigmoid(y)
        if mode.startswith("silu_l2"):
            y = y * lax.rsqrt(_group_sum(y * y, bd_ref[...]) + EPS)
        if mode == "silu_l2_scaled":
            y = y * (HEAD_DIM ** -0.5)
        o_refs[t][0] = y


def _conv_prep(proj, col, conv_w, modes):
    b_, s_, _ = proj.shape
    assert col % GROUP_W == 0
    tm = CONV_TM
    n_steps = s_ // tm
    hb = tm // CONV_HALO
    specs = []
    for t in range(len(modes)):
        c = col // GROUP_W + t
        specs += [pl.BlockSpec((1, tm, GROUP_W), lambda b, i, c=c: (b, i, c)),
                  pl.BlockSpec((1, CONV_HALO, GROUP_W), lambda b, i, c=c: (b, jnp.maximum(i * hb - 1, 0), c)),
                  pl.BlockSpec((1, CONV_HALO, GROUP_W),
                               lambda b, i, c=c: (b, jnp.minimum((i + 1) * hb, s_ // CONV_HALO - 1), c))]
    bd = _group_ones(GROUP_W)
    w = conv_w.astype(jnp.float32)
    specs += [pl.BlockSpec(w.shape, lambda b, i: (0, 0)), pl.BlockSpec(bd.shape, lambda b, i: (0, 0))]
    out = jax.ShapeDtypeStruct((b_, s_, GROUP_W), jnp.float32)
    return pl.pallas_call(
        functools.partial(_conv_prep_kernel, modes=modes, n_steps=n_steps),
        grid=(b_, n_steps), in_specs=specs,
        out_specs=[pl.BlockSpec((1, tm, GROUP_W), lambda b, i: (b, i, 0))] * len(modes),
        out_shape=[out] * len(modes),
        compiler_params=pltpu.CompilerParams(dimension_semantics=("arbitrary", "arbitrary")),
        name="conv_prep",
    )(*([proj] * (3 * len(modes))), w, bd)


def _gdn_post_kernel(of_ref, ob_ref, z_ref, g_ref, bd_ref, y_ref):
    o = of_ref[0] + ob_ref[0]
    z = z_ref[0]
    ms = _group_sum(o * o, bd_ref[...]) * (1.0 / HEAD_DIM)
    y_ref[0] = o * lax.rsqrt(ms + EPS) * g_ref[...] * (z * jax.nn.sigmoid(z))


def _gdn_post(o_fwd, o_bwd, proj, z_col, norm_g):
    b_, s_, w_ = o_fwd.shape
    tm = CONV_TM
    tok = pl.BlockSpec((1, tm, w_), lambda b, i: (b, i, 0))
    bd = _group_ones(w_)
    g = jnp.tile(norm_g.astype(jnp.float32), N_HEADS).reshape(1, w_)
    return pl.pallas_call(
        _gdn_post_kernel, grid=(b_, s_ // tm),
        in_specs=[tok, tok, pl.BlockSpec((1, tm, w_), lambda b, i: (b, i, z_col // w_)),
                  pl.BlockSpec(g.shape, lambda b, i: (0, 0)), pl.BlockSpec(bd.shape, lambda b, i: (0, 0))],
        out_specs=tok, out_shape=jax.ShapeDtypeStruct((b_, s_, w_), jnp.float32),
        compiler_params=pltpu.CompilerParams(dimension_semantics=("arbitrary", "arbitrary")),
        name="gdn_post",
    )(o_fwd, o_bwd, proj, g, bd)


def _gdn_masks(rev):
    r = np.arange(GDN_SB)
    i, j = r[:, None], r[None, :]
    same = lambda s: (i // s) == (j // s)
    before = (i < j) if rev else (i > j)
    chunk = same(GDN_KCHUNK)
    masks = [chunk & (before | (i == j)),
             chunk,
             chunk & before,
             i == j,
             same(GDN_INV_BASE),
             same(HEAD_DIM)]
    s = GDN_INV_BASE
    while s < GDN_KCHUNK:
        masks.append(same(2 * s) & ~same(s))
        s *= 2
    return np.stack(masks).astype(np.float32)


def _mm_bf16(a, b):
    return jnp.dot(a.astype(jnp.bfloat16), b.astype(jnp.bfloat16), preferred_element_type=jnp.float32)


def _unit_triangular_inverses(a_list, eye, m_base, m_offs):
    n_list = [-a * m_base for a in a_list]
    t_list = [eye + n for n in n_list]
    power = 2
    while power < GDN_INV_BASE:
        n_list = [_mm_bf16(n, n) for n in n_list]
        t_list = [t + _mm_bf16(t, n) for t, n in zip(t_list, n_list)]
        power *= 2
    for m_off in m_offs:
        u_list = [_mm_bf16(a * m_off, t) for a, t in zip(a_list, t_list)]
        t_list = [t - _mm_bf16(t, u) for t, u in zip(t_list, u_list)]
    return t_list


def _gdn_kernel(mask_ref, expand_ref, q_ref, k_ref, v_ref, beta_ref, g_ref, gt_ref, o_ref,
                state, vstack, val_s, kcd_s, qd_s, kd_s, attn_s, *, rev):
    bf16 = jnp.bfloat16
    f32 = jnp.float32
    hd = HEAD_DIM
    gw = GDN_HG * hd
    n_groups = N_HEADS // GDN_HG
    rr = GDN_SB
    nt = (((1,), (1,)), ((), ()))
    tn = (((0,), (0,)), ((), ()))

    @pl.when(pl.program_id(1) == 0)
    def _():
        state[...] = jnp.zeros(state.shape, state.dtype)
        vstack[...] = jnp.zeros(vstack.shape, vstack.dtype)

    incl, ones, strict, eye, m_base, head_blk = (mask_ref[t] for t in range(6))
    m_offs = [mask_ref[t] for t in range(6, mask_ref.shape[0])]
    g = g_ref[0, 0]
    gc = _dot_f32(incl, g)
    gl = _dot_f32(ones, g)
    gct = lax.dot_general(gt_ref[0, 0], incl, nt, precision=lax.Precision.HIGHEST,
                          preferred_element_type=jnp.float32)
    expand = expand_ref[...]
    beta_w = _dot_f32(beta_ref[0, 0], expand)
    eg_w = jnp.exp(_dot_f32(gc, expand))
    ekd_w = jnp.exp(_dot_f32(gl - gc, expand))
    cd_w = jnp.exp(_dot_f32(gl, expand))
    lane = lax.broadcasted_iota(jnp.int32, (1, gw), 1)
    head_mask = [(lane // hd == j).astype(f32) for j in range(GDN_HG)]
    a_list = []
    for grp in range(n_groups):
        sl = slice(grp * gw, (grp + 1) * gw)
        q, k = q_ref[0][:, sl], k_ref[0][:, sl]
        kb = k * beta_w[:, sl]
        kbf = k.astype(bf16)
        qd_s[grp] = (q * eg_w[:, sl]).astype(bf16)
        kd_s[grp] = (k * ekd_w[:, sl]).astype(bf16)
        for j in range(GDN_HG):
            h = grp * GDN_HG + j
            dec = jnp.exp(jnp.minimum(gc[:, h:h + 1] - gct[h:h + 1, :], 0.0)) * incl
            a_list.append(lax.dot_general((kb * head_mask[j]).astype(bf16), kbf, nt,
                                          preferred_element_type=f32) * dec * strict)
            attn = lax.dot_general((q * head_mask[j]).astype(bf16), kbf, nt, preferred_element_type=f32) * dec
            attn_s[h] = attn.astype(bf16)
    t_list = _unit_triangular_inverses(a_list, eye, m_base, m_offs)
    for grp in range(n_groups):
        sl = slice(grp * gw, (grp + 1) * gw)
        vb = v_ref[0][:, sl] * beta_w[:, sl]
        kbg = k_ref[0][:, sl] * beta_w[:, sl] * eg_w[:, sl]
        val = jnp.zeros((rr, gw), f32)
        kcd = jnp.zeros((rr, gw), f32)
        for j in range(GDN_HG):
            t = t_list[grp * GDN_HG + j]
            val = val + _mm_bf16(t, vb * head_mask[j])
            kcd = kcd + _mm_bf16(t, kbg * head_mask[j])
        val_s[grp] = val
        kcd_s[grp] = kcd.astype(bf16)
    n_chunks = GDN_SB // GDN_KCHUNK
    for c in (reversed(range(n_chunks)) if rev else range(n_chunks)):
        rows = slice(c * GDN_KCHUNK, (c + 1) * GDN_KCHUNK)
        for grp in range(n_groups):
            s_old = state[grp]
            s_bf = s_old.astype(bf16)
            both = jnp.dot(jnp.concatenate([kcd_s[grp, rows, :], qd_s[grp, rows, :]], axis=0), s_bf,
                           preferred_element_type=f32)
            v_new = val_s[grp, rows, :] - both[:GDN_KCHUNK]
            for j in range(GDN_HG):
                vstack[grp, j * rr + c * GDN_KCHUNK:j * rr + (c + 1) * GDN_KCHUNK, :] = (
                    v_new * head_mask[j]).astype(bf16)
            attn_cat = jnp.concatenate([attn_s[grp * GDN_HG + j, rows, :] for j in range(GDN_HG)], axis=1)
            o = both[GDN_KCHUNK:] + jnp.dot(attn_cat, vstack[grp], preferred_element_type=f32)
            upd = lax.dot_general(kd_s[grp, rows, :], v_new.astype(bf16), tn, preferred_element_type=f32)
            state[grp] = s_old * cd_w[c * GDN_KCHUNK:c * GDN_KCHUNK + 1, grp * gw:(grp + 1) * gw] + upd * head_blk
            o_ref[0, rows, grp * gw:(grp + 1) * gw] = o


def _gdn_scan(q, k, v, beta, g, rev):
    b_, s_, w_ = q.shape
    assert GDN_SB == GDN_HG * HEAD_DIM and GDN_SB % GDN_KCHUNK == 0 and GDN_KCHUNK % GDN_CHUNK == 0
    n_sb = s_ // GDN_SB
    n_groups = N_HEADS // GDN_HG
    gw = GDN_HG * HEAD_DIM
    masks = jnp.asarray(_gdn_masks(rev))
    expand = jnp.asarray(np.arange(N_HEADS)[:, None] == np.arange(w_)[None, :] // HEAD_DIM, jnp.float32)
    step = (lambda i: n_sb - 1 - i) if rev else (lambda i: i)
    tok = pl.BlockSpec((1, GDN_SB, w_), lambda b, i: (b, step(i), 0))
    gate = pl.BlockSpec((1, 1, GDN_SB, N_HEADS), lambda b, i: (b, 0, step(i), 0))
    gate_t = pl.BlockSpec((1, 1, N_HEADS, GDN_SB), lambda b, i: (b, 0, 0, step(i)))
    per_group = lambda dt: pltpu.VMEM((n_groups, GDN_SB, gw), dt)
    return pl.pallas_call(
        functools.partial(_gdn_kernel, rev=rev),
        grid=(b_, n_sb),
        in_specs=[pl.BlockSpec(masks.shape, lambda b, i: (0, 0, 0)),
                  pl.BlockSpec(expand.shape, lambda b, i: (0, 0)), tok, tok, tok, gate, gate, gate_t],
        out_specs=tok,
        out_shape=jax.ShapeDtypeStruct((b_, s_, w_), jnp.float32),
        scratch_shapes=[pltpu.VMEM((n_groups, gw, gw), jnp.float32),
                        pltpu.VMEM((n_groups, GDN_HG * GDN_SB, gw), jnp.bfloat16),
                        per_group(jnp.float32), per_group(jnp.bfloat16), per_group(jnp.bfloat16),
                        per_group(jnp.bfloat16),
                        pltpu.VMEM((N_HEADS, GDN_SB, GDN_SB), jnp.bfloat16)],
        compiler_params=pltpu.CompilerParams(dimension_semantics=("arbitrary", "arbitrary"),
                                             vmem_limit_bytes=GDN_VMEM_LIMIT_BYTES),
        name="gdn_scan_rev" if rev else "gdn_scan_fwd",
    )(masks, expand, q, k, v, beta, g, g.transpose(0, 1, 3, 2))


def gdn_mixer(proj, conv_w, a_log, dt_bias, norm_g):
    b_, s_, _ = proj.shape
    q, k, v = _conv_prep(proj, PROJ_DEST["a_qkv"][1], conv_w, ("silu_l2_scaled", "silu_l2", "silu"))
    gates = proj[:, :, PROJ_DEST["a_beta"][1]:PROJ_DEST["a_beta"][1] + 4 * N_HEADS]
    b_in = gates[:, :, :2 * N_HEADS].reshape(b_, s_, 2, N_HEADS)
    a_in = gates[:, :, 2 * N_HEADS:].reshape(b_, s_, 2, N_HEADS)
    beta = jax.nn.sigmoid(b_in).transpose(0, 2, 1, 3)
    g = -jnp.exp(a_log.astype(jnp.float32)) * jax.nn.softplus(a_in + dt_bias.astype(jnp.float32))
    g = g.transpose(0, 2, 1, 3)
    o_fwd = _gdn_scan(q, k, v, beta[:, 0:1], g[:, 0:1], False)
    o_bwd = _gdn_scan(q, k, v, beta[:, 1:2], g[:, 1:2], True)
    return _gdn_post(o_fwd, o_bwd, proj, PROJ_DEST["a_z"][1], norm_g)


def hyena_pos_features(length):
    t = jnp.linspace(0.0, 1.0, length, dtype=jnp.float32)[:, None]
    w = 2.0 * math.pi * jnp.arange(length, dtype=jnp.float32) / length
    f = jnp.linspace(1e-4, HY_BANDS - 1, HY_BANDS, dtype=jnp.float32)
    fw = w[:, None] * f[None, :]
    return jnp.concatenate([t, jnp.cos(fw), -jnp.sin(fw)], axis=-1)


def hyena_filters_f(z, w1, b1, f1, w2, b2, f2, w3, deltas):
    f32 = jnp.float32
    length = z.shape[0]
    t = z[:, :1]
    h = jnp.sin(f1.astype(f32) * (z @ w1.astype(f32) + b1.astype(f32)))
    h = jnp.sin(f2.astype(f32) * (h @ w2.astype(f32) + b2.astype(f32)))
    h = (h @ w3.astype(f32)) * jnp.exp(-t * jnp.abs(deltas.astype(f32)))
    h = h.reshape(length, HY_ORDER, 2, GROUP_W)
    kern = jnp.concatenate([h[:, :, 0], jnp.zeros((1, HY_ORDER, GROUP_W), f32),
                            h[:0:-1, :, 1]], axis=0)
    return kern / (jnp.sum(jnp.abs(kern), axis=0, keepdims=True) + EPS)


def _dft_tables(n_fft):
    n1 = n_fft // FFT_N2
    def dft(n):
        kk = (np.arange(n)[:, None] * np.arange(n)[None, :]) % n
        ang = -2.0 * np.pi * kk / n
        return np.cos(ang), np.sin(ang)
    f1r, f1i = dft(n1)
    f2r, f2i = dft(FFT_N2)
    kk = (np.arange(n1)[:, None] * np.arange(FFT_N2)[None, :]) % n_fft
    tw = -2.0 * np.pi * kk / n_fft
    f32 = np.float32
    return dict(
        f1=np.concatenate([f1r, f1i], axis=0).astype(f32),
        f1_inv=(np.concatenate([f1r[:n1 // 2], f1i[:n1 // 2]], axis=0) / n_fft).astype(f32),
        f2=np.concatenate([f2r, f2i], axis=0).astype(f32),
        twr=np.cos(tw).astype(f32)[:, :, None], twi=np.sin(tw).astype(f32)[:, :, None])


def _dot_f32(a, b):
    return jnp.dot(a, b, precision=lax.Precision.HIGHEST, preferred_element_type=jnp.float32)


def _split_lhs(f):
    f = jnp.asarray(f, jnp.float32)
    hi = f.astype(jnp.bfloat16)
    lo = (f - hi.astype(jnp.float32)).astype(jnp.bfloat16)
    return jnp.concatenate([hi, hi, lo], axis=1)


def _dot_split(f3, x):
    hi = x.astype(jnp.bfloat16)
    lo = (x - hi.astype(jnp.float32)).astype(jnp.bfloat16)
    return jnp.dot(f3, jnp.concatenate([hi, lo, hi], axis=0), preferred_element_type=jnp.float32)


def _fft_stage1_kernel(f_ref, zr_ref, zi_ref, yr_ref, yi_ref, *, n1):
    f = f_ref[...]
    p = _dot_split(f, zr_ref[0])
    if zi_ref is None:
        yr_ref[...] = p[:n1]
        yi_ref[...] = p[n1:]
    else:
        q = _dot_split(f, zi_ref[0])
        yr_ref[...] = p[:n1] - q[n1:]
        yi_ref[...] = q[:n1] + p[n1:]


def _fft_stage1_real_kernel(f_ref, zr_ref, yr_ref, yi_ref, *, n1):
    _fft_stage1_kernel(f_ref, zr_ref, None, yr_ref, yi_ref, n1=n1)


def _fft_mid_kernel(f_ref, twr_ref, twi_ref, yr_ref, yi_ref, kr_ref, ki_ref, qr_ref, qi_ref):
    n2 = FFT_N2
    twr, twi = twr_ref[0], twi_ref[0]
    yr, yi = yr_ref[0], yi_ref[0]
    f = f_ref[...]
    p = _dot_split(f, yr * twr - yi * twi)
    q = _dot_split(f, yr * twi + yi * twr)
    xr = p[:n2] - q[n2:]
    xi = q[:n2] + p[n2:]
    if kr_ref is None:
        qr_ref[0] = xr
        qi_ref[0] = xi
        return
    kr, ki = kr_ref[0], ki_ref[0]
    p = _dot_split(f, xr * kr - xi * ki)
    q = _dot_split(f, xr * ki + xi * kr)
    wr = p[:n2] + q[n2:]
    wi = q[:n2] - p[n2:]
    qr_ref[0] = wr * twr + wi * twi
    qi_ref[0] = wi * twr - wr * twi


def _fft_mid_spectrum_kernel(f_ref, twr_ref, twi_ref, yr_ref, yi_ref, qr_ref, qi_ref):
    _fft_mid_kernel(f_ref, twr_ref, twi_ref, yr_ref, yi_ref, None, None, qr_ref, qi_ref)


def _fft_last_kernel(f_ref, qr_ref, qi_ref, u_ref, gate_ref, bias_ref, o_ref, *, nh):
    f = f_ref[...]
    p = _dot_split(f, qr_ref[...])
    q = _dot_split(f, qi_ref[...])
    bias = bias_ref[...]
    o_ref[0] = gate_ref[0] * (p[:nh] + q[nh:] + u_ref[0] * bias)
    o_ref[1] = gate_ref[1] * (q[:nh] - p[nh:] + u_ref[1] * bias)


def _fft_params(n_axes):
    return pltpu.CompilerParams(dimension_semantics=("arbitrary",) * n_axes,
                                vmem_limit_bytes=FFT_VMEM_LIMIT_BYTES)


def _fft_forward(tab, z, n_ch):
    n1 = tab["f1"].shape[1]
    parts, rows, cols = z.shape
    tn = min(FFT_COL_TILE, cols)
    f1 = _split_lhs(tab["f1"][:, :rows])
    y_shape = jax.ShapeDtypeStruct((n1, cols), jnp.float32)
    col_spec = pl.BlockSpec((n1, tn), lambda j: (0, j))
    z_specs = [pl.BlockSpec((1, rows, tn), lambda j, p=p: (p, 0, j)) for p in range(parts)]
    body = _fft_stage1_kernel if parts == 2 else _fft_stage1_real_kernel
    yr, yi = pl.pallas_call(
        functools.partial(body, n1=n1), grid=(cols // tn,),
        in_specs=[pl.BlockSpec(f1.shape, lambda j: (0, 0))] + z_specs,
        out_specs=[col_spec, col_spec], out_shape=[y_shape, y_shape],
        compiler_params=_fft_params(1), name="fft_stage1",
    )(f1, *([z] * parts))
    return yr.reshape(n1, FFT_N2, n_ch), yi.reshape(n1, FFT_N2, n_ch)


def _fft_mid(tab, yr, yi, kr=None, ki=None, k_block=0):
    n1, n2, n_ch = yr.shape
    slab = pl.BlockSpec((1, n2, n_ch), lambda i: (i, 0, 0))
    tw_spec = pl.BlockSpec((1, n2, 1), lambda i: (i, 0, 0))
    f2 = _split_lhs(tab["f2"])
    ops = [f2, jnp.asarray(tab["twr"]), jnp.asarray(tab["twi"]), yr, yi]
    specs = [pl.BlockSpec(f2.shape, lambda i: (0, 0)), tw_spec, tw_spec, slab, slab]
    body = _fft_mid_spectrum_kernel
    if kr is not None:
        ops += [kr, ki]
        specs += [pl.BlockSpec((1, n2, n_ch), lambda i: (i, 0, k_block))] * 2
        body = _fft_mid_kernel
    shape = jax.ShapeDtypeStruct((n1, n2, n_ch), jnp.float32)
    return pl.pallas_call(
        body, grid=(n1,), in_specs=specs, out_specs=[slab, slab], out_shape=[shape, shape],
        compiler_params=_fft_params(1), name="fft_mid",
    )(*ops)


def _fft_conv_gate(tab, u, gate, bias, kr, ki, k_block):
    b_, length, n_ch = u.shape
    assert b_ == 2
    n1 = tab["f1"].shape[1]
    nh = n1 // 2
    cols = FFT_N2 * n_ch
    uv = u.reshape(b_, nh, cols)
    yr, yi = _fft_forward(tab, uv, n_ch)
    qr, qi = _fft_mid(tab, yr, yi, kr, ki, k_block)
    tn = min(FFT_COL_TILE, cols)
    f1_inv = _split_lhs(tab["f1_inv"])
    q_spec = pl.BlockSpec((n1, tn), lambda j: (0, j))
    u_spec = pl.BlockSpec((b_, nh, tn), lambda j: (0, 0, j))
    out = pl.pallas_call(
        functools.partial(_fft_last_kernel, nh=nh), grid=(cols // tn,),
        in_specs=[pl.BlockSpec(f1_inv.shape, lambda j: (0, 0)), q_spec, q_spec, u_spec, u_spec,
                  pl.BlockSpec((1, tn), lambda j: (0, 0))],
        out_specs=u_spec, out_shape=jax.ShapeDtypeStruct((b_, nh, cols), jnp.float32),
        compiler_params=_fft_params(1), name="fft_last",
    )(f1_inv, qr.reshape(n1, cols), qi.reshape(n1, cols), uv, gate.reshape(b_, nh, cols),
      jnp.tile(bias.astype(jnp.float32), tn // n_ch).reshape(1, tn))
    return out.reshape(b_, length, n_ch)


def hyena_mixer(proj, conv_w, kern, bias):
    dtype = proj.dtype
    length = proj.shape[1]
    x1, x2, v = _conv_prep(proj, PROJ_DEST["b_in"][1], conv_w, ("plain",) * 3)
    tab = _dft_tables(2 * length)
    n1 = tab["f1"].shape[1]
    n_filt = HY_ORDER * GROUP_W
    kr, ki = _fft_mid(tab, *_fft_forward(tab, kern.reshape(1, n1, FFT_N2 * n_filt), n_filt))
    y = _fft_conv_gate(tab, v, x1, bias[0], kr, ki, 0)
    y = _fft_conv_gate(tab, y, x2, bias[1], kr, ki, 1)
    return y.astype(dtype)


def alibi_slopes(n):
    return 2.0 ** (-8.0 * jnp.arange(1, n + 1, dtype=jnp.float32) / n)


def _window_kernel(q_ref, kp_ref, kc_ref, kn_ref, bias_ref, sink_ref, g_ref, o_ref, *, nb):
    bf16 = jnp.bfloat16
    nt = (((1,), (1,)), ((), ()))
    col = lax.broadcasted_iota(jnp.int32, (1, 3 * BLOCK), 1)
    q_all = q_ref[0] * (HEAD_DIM ** -0.5)
    kv_all = jnp.concatenate([kp_ref[0], kc_ref[0], kn_ref[0]], axis=0)
    chains = [(u, j) for u in range(WINDOW_SUB) for j in range(N_KV_HEADS)]
    s_list, v_list = [], []
    for u, j in chains:
        n = pl.program_id(1) * WINDOW_SUB + u
        edge = jnp.where(((n == 0) & (col < BLOCK)) | ((n == nb - 1) & (col >= 2 * BLOCK)), WINDOW_MASK, 0.0)
        q = q_all[u * BLOCK:(u + 1) * BLOCK]
        kv = kv_all[u * BLOCK:(u + 3) * BLOCK]
        q4 = jnp.concatenate([q[:, (j * GQA_GROUP + g) * HEAD_DIM:(j * GQA_GROUP + g + 1) * HEAD_DIM]
                              for g in range(GQA_GROUP)], axis=0)
        k = kv[:, j * HEAD_DIM:(j + 1) * HEAD_DIM]
        v_list.append(kv[:, KV_W + j * HEAD_DIM:KV_W + (j + 1) * HEAD_DIM].astype(bf16))
        s = lax.dot_general(q4.astype(bf16), k.astype(bf16), nt, preferred_element_type=jnp.float32)
        s_list.append(s + bias_ref[j] + edge)
    m_list = [jnp.maximum(jnp.max(s, axis=-1, keepdims=True), sink_ref[j]) for s, (u, j) in zip(s_list, chains)]
    p_list = [jnp.exp(s - m) for s, m in zip(s_list, m_list)]
    d_list = [jnp.sum(p, axis=-1, keepdims=True) + jnp.exp(sink_ref[j] - m)
              for p, m, (u, j) in zip(p_list, m_list, chains)]
    o_list = [jnp.dot(p.astype(bf16), v, preferred_element_type=jnp.float32) / d
              for p, v, d in zip(p_list, v_list, d_list)]
    for u in range(WINDOW_SUB):
        outs = []
        for j in range(N_KV_HEADS):
            o = o_list[u * N_KV_HEADS + j]
            outs += [o[g * BLOCK:(g + 1) * BLOCK] for g in range(GQA_GROUP)]
        o = jnp.concatenate(outs, axis=-1)
        o_ref[0, u * BLOCK:(u + 1) * BLOCK, :] = (
            o * lax.rsqrt(jnp.mean(o * o, axis=-1, keepdims=True) + EPS) * g_ref[...])


def window_mixer(proj, q_col, kv_col, sink, slopes, norm_g):
    b_, s_, _ = proj.shape
    assert q_col % GROUP_W == 0 and kv_col % (2 * KV_W) == 0
    nb = s_ // BLOCK
    sub = WINDOW_SUB
    n_steps = nb // sub
    rel = BLOCK + jnp.arange(BLOCK)[:, None] - jnp.arange(3 * BLOCK)[None, :]
    dist = jnp.abs(rel).astype(jnp.float32)
    bias = jnp.where(jnp.abs(rel) <= WINDOW, -slopes[:, None, None] * dist, WINDOW_MASK)
    bias = bias.reshape(N_KV_HEADS, GQA_GROUP * BLOCK, 3 * BLOCK)
    sink_rows = jnp.repeat(sink.astype(jnp.float32), BLOCK).reshape(N_KV_HEADS, GQA_GROUP * BLOCK, 1)
    kc = kv_col // (2 * KV_W)
    const = lambda shape: pl.BlockSpec(shape, lambda b, n: (0,) * len(shape))
    rows = pl.BlockSpec((1, sub * BLOCK, GROUP_W), lambda b, n: (b, n, q_col // GROUP_W))
    return pl.pallas_call(
        functools.partial(_window_kernel, nb=nb),
        grid=(b_, n_steps),
        in_specs=[rows,
                  pl.BlockSpec((1, BLOCK, 2 * KV_W), lambda b, n: (b, jnp.maximum(n * sub - 1, 0), kc)),
                  pl.BlockSpec((1, sub * BLOCK, 2 * KV_W), lambda b, n: (b, n, kc)),
                  pl.BlockSpec((1, BLOCK, 2 * KV_W), lambda b, n: (b, jnp.minimum((n + 1) * sub, nb - 1), kc)),
                  const(bias.shape), const(sink_rows.shape), const((1, GROUP_W))],
        out_specs=pl.BlockSpec((1, sub * BLOCK, GROUP_W), lambda b, n: (b, n, 0)),
        out_shape=jax.ShapeDtypeStruct((b_, s_, GROUP_W), jnp.float32),
        compiler_params=pltpu.CompilerParams(dimension_semantics=("arbitrary", "arbitrary")),
        name="window_attention",
    )(proj, proj, proj, proj, bias, sink_rows, norm_g.astype(jnp.float32).reshape(1, GROUP_W))


def _rope_tables(row_idx, col_idx, n_heads):
    half = HEAD_DIM // 2
    inv = ROPE_THETA ** (-jnp.arange(0, half, 2, dtype=jnp.float32) / half)

    def tabs(pos):
        ang = pos.astype(jnp.float32)[:, None] * inv[None, :]
        c, sn = jnp.cos(ang), jnp.sin(ang)
        z = jnp.zeros_like(sn)
        return (jnp.concatenate([c, c], -1), jnp.concatenate([-sn, z], -1), jnp.concatenate([z, sn], -1))

    per_head = [jnp.concatenate([r, c], -1) for r, c in zip(tabs(row_idx), tabs(col_idx))]
    return tuple(jnp.tile(t, (1, n_heads)) for t in per_head)


def _global_prep_kernel(q_ref, kv_ref, c_ref, s1_ref, s2_ref, gq_ref, gk_ref, bdq_ref, bdk_ref,
                        qa_ref, kt_ref, va_ref, kn2_ref):
    f32 = jnp.float32
    bf16 = jnp.bfloat16
    hd = HEAD_DIM
    quarter = hd // 4

    def norm_rope(x, g, bd):
        w = x.shape[1]
        y = x * lax.rsqrt(_group_sum(x * x, bd) * (1.0 / hd) + EPS) * g
        return (y * c_ref[:, :w] + pltpu.roll(y, w - quarter, 1) * s1_ref[:, :w]
                + pltpu.roll(y, quarter, 1) * s2_ref[:, :w])

    lane = lax.broadcasted_iota(jnp.int32, (1, 2 * hd), 1)

    def pair_slot(x2, j, tail):
        first = x2 if j == 0 else pltpu.roll(x2, hd, 1)
        return jnp.where(lane < hd, first, tail)

    q = norm_rope(q_ref[0], gq_ref[...], bdq_ref[...]) * (LOG2E * hd ** -0.5)
    qf = q.astype(bf16).astype(f32)
    qn = jnp.sqrt(_group_sum(qf * qf, bdq_ref[...]))
    for h in range(N_HEADS):
        p2 = slice((h // 2) * 2 * hd, (h // 2 + 1) * 2 * hd)
        n2 = qn[:, p2] if h % 2 == 1 else pltpu.roll(qn[:, p2], hd, 1)
        tail = jnp.where(lane == hd, -n2, 0.0)
        qa_ref[0, h // GQA_GROUP, h % GQA_GROUP] = pair_slot(qf[:, p2], h % 2, tail).astype(bf16)
    kv = kv_ref[0]
    k = norm_rope(kv[:, :KV_W], gk_ref[...], bdk_ref[...])
    kf = k.astype(bf16).astype(f32)
    kn2_ref[0] = _group_sum(kf * kf, bdk_ref[...])
    v = kv[:, KV_W:]
    one = jnp.where(lane == hd, 1.0, 0.0)
    for j in range(N_KV_HEADS):
        kt_ref[0, j] = pair_slot(kf, j, 0.0).T.astype(bf16)
        va_ref[0, j] = pair_slot(v, j, one).astype(bf16)


def _global_prep(proj, q_col, kv_col, tables, q_norm_g, k_norm_g):
    b_, s_, _ = proj.shape
    assert KV_W == 2 * HEAD_DIM and q_col % GROUP_W == 0 and kv_col % (2 * KV_W) == 0
    tm = CONV_TM
    wide = 2 * HEAD_DIM
    bf16 = jnp.bfloat16
    const = lambda a: pl.BlockSpec(a.shape, lambda b, i: (0,) * a.ndim)
    tab = pl.BlockSpec((tm, GROUP_W), lambda b, i: (i, 0))
    gq = jnp.tile(q_norm_g.astype(jnp.float32), N_HEADS).reshape(1, GROUP_W)
    gk = jnp.tile(k_norm_g.astype(jnp.float32), N_KV_HEADS).reshape(1, KV_W)
    bdq, bdk = _group_ones(GROUP_W), _group_ones(KV_W)
    return pl.pallas_call(
        _global_prep_kernel, grid=(b_, s_ // tm),
        in_specs=[pl.BlockSpec((1, tm, GROUP_W), lambda b, i: (b, i, q_col // GROUP_W)),
                  pl.BlockSpec((1, tm, 2 * KV_W), lambda b, i: (b, i, kv_col // (2 * KV_W))),
                  tab, tab, tab, const(gq), const(gk), const(bdq), const(bdk)],
        out_specs=[pl.BlockSpec((1, N_KV_HEADS, GQA_GROUP, tm, wide), lambda b, i: (b, 0, 0, i, 0)),
                   pl.BlockSpec((1, N_KV_HEADS, wide, tm), lambda b, i: (b, 0, 0, i)),
                   pl.BlockSpec((1, N_KV_HEADS, tm, wide), lambda b, i: (b, 0, i, 0)),
                   pl.BlockSpec((1, tm, KV_W), lambda b, i: (b, i, 0))],
        out_shape=[jax.ShapeDtypeStruct((b_, N_KV_HEADS, GQA_GROUP, s_, wide), bf16),
                   jax.ShapeDtypeStruct((b_, N_KV_HEADS, wide, s_), bf16),
                   jax.ShapeDtypeStruct((b_, N_KV_HEADS, s_, wide), bf16),
                   jax.ShapeDtypeStruct((b_, s_, KV_W), jnp.float32)],
        compiler_params=pltpu.CompilerParams(dimension_semantics=("arbitrary", "arbitrary")),
        name="global_prep",
    )(proj, proj, *tables, gq, gk, bdq, bdk)


def _flash_kernel(q_ref, kt_ref, v_ref, o_ref, *, tq, tk, n_kc):
    m_rows = GQA_GROUP * tq
    q = q_ref[0, 0].reshape(m_rows, 2 * HEAD_DIM)

    def body(c, acc):
        off = pl.multiple_of(c * tk, tk)
        s = jnp.dot(q, kt_ref[0, 0, :, pl.ds(off, tk)], preferred_element_type=jnp.float32)
        p = jnp.exp2(s).astype(jnp.bfloat16)
        return acc + jnp.dot(p, v_ref[0, 0, pl.ds(off, tk), :], preferred_element_type=jnp.float32)

    acc = lax.fori_loop(0, n_kc, body, jnp.zeros((m_rows, 2 * HEAD_DIM), jnp.float32))
    o = acc[:, :HEAD_DIM] / acc[:, HEAD_DIM:HEAD_DIM + 1]
    o_ref[0] = jnp.concatenate([o[g * tq:(g + 1) * tq] for g in range(GQA_GROUP)], axis=-1)


def _rowmax_kernel(q_ref, kt_ref, m_ref, *, tq, tk, n_kc):
    m_rows = GQA_GROUP * tq
    q = q_ref[0, 0].reshape(m_rows, 2 * HEAD_DIM)

    def body(c, mx):
        off = pl.multiple_of(c * tk, tk)
        s = jnp.dot(q, kt_ref[0, 0, :, pl.ds(off, tk)], preferred_element_type=jnp.float32)
        for j in range(tk // 128):
            mx = jnp.maximum(mx, s[:, j * 128:(j + 1) * 128])
        return mx

    mx = lax.fori_loop(0, n_kc, body, jnp.full((m_rows, 128), -jnp.inf, jnp.float32))
    m_ref[0, 0] = jnp.max(mx, axis=-1, keepdims=True).reshape(GQA_GROUP, tq, 1)


def _attn_call(body, q, kt, v, name, *, tq=ATTN_TQ, tk=ATTN_TK):
    b_, _, _, s_, _ = q.shape
    wide = 2 * HEAD_DIM
    q_spec = pl.BlockSpec((1, 1, GQA_GROUP, tq, wide), lambda b, h, i: (b, h, 0, i, 0))
    kt_spec = pl.BlockSpec((1, 1, wide, s_), lambda b, h, i: (b, h, 0, 0))
    v_spec = pl.BlockSpec((1, 1, s_, wide), lambda b, h, i: (b, h, 0, 0))
    if v is None:
        operands, in_specs = (q, kt), [q_spec, kt_spec]
        o_spec = pl.BlockSpec((1, 1, GQA_GROUP, tq, 1), lambda b, h, i: (b, h, 0, i, 0))
        o_shape = (b_, N_KV_HEADS, GQA_GROUP, s_, 1)
    else:
        operands, in_specs = (q, kt, v), [q_spec, kt_spec, v_spec]
        o_spec = pl.BlockSpec((1, tq, GQA_GROUP * HEAD_DIM), lambda b, h, i: (b, i, h))
        o_shape = (b_, s_, N_HEADS * HEAD_DIM)
    return pl.pallas_call(
        functools.partial(body, tq=tq, tk=tk, n_kc=s_ // tk),
        out_shape=jax.ShapeDtypeStruct(o_shape, jnp.float32),
        grid=(b_, N_KV_HEADS, s_ // tq),
        in_specs=in_specs, out_specs=o_spec,
        compiler_params=pltpu.CompilerParams(
            dimension_semantics=("arbitrary", "arbitrary", "arbitrary"),
            vmem_limit_bytes=ATTN_VMEM_LIMIT_BYTES),
        name=name,
    )(*operands)


def global_mixer(proj, tables, q_norm_g, k_norm_g):
    bf16 = jnp.bfloat16
    hd = HEAD_DIM
    qa, kt, va, kn2 = _global_prep(proj, PROJ_DEST["d_q"][1], PROJ_DEST["d_kv"][1], tables,
                                   q_norm_g, k_norm_g)
    kn = jnp.sqrt(jnp.max(kn2, axis=1)[:, ::hd])
    c = (kn * ATTN_ROUND_UP).astype(bf16)
    qmax = jnp.max(-qa[:, :, :, :, hd].astype(jnp.float32), axis=(2, 3))
    with_key_row = lambda row: kt.at[:, :, hd, :].set(jnp.broadcast_to(row[:, :, None], kt.shape[:2] + kt.shape[3:]))

    def exact_shift():
        m = _attn_call(_rowmax_kernel, qa, kt, None, "global_attention_rowmax")
        return qa.at[:, :, :, :, hd].set((-m[..., 0]).astype(bf16)), with_key_row(jnp.ones_like(c))

    qa, kta = lax.cond(jnp.max(qmax * c.astype(jnp.float32)) < ATTN_SAFE_SHIFT,
                       lambda: (qa, with_key_row(c)), exact_shift)
    return _attn_call(_flash_kernel, qa, kta, va, "global_flash_attention")


def _row_copy(src, src_row, dst, dst_row, sem):
    return pltpu.make_async_copy(src.at[pl.ds(src_row, 1)], dst.at[pl.ds(dst_row, 1)], sem)


def _moe_dispatch_kernel(pos_ref, x_ref, g_ref, zeros_hbm, xs_hbm, xn_scr, sem, *, tb, n_steps):
    del zeros_hbm
    i = pl.program_id(0)
    slot = i % 2

    def wait_slot(s):
        for _ in range(TOP_K):
            pltpu.make_async_copy(xn_scr.at[s], xn_scr.at[s], sem.at[s]).wait()

    @pl.when(i >= 2)
    def _():
        wait_slot(slot)

    x = x_ref[...]
    xn_scr[slot] = x * lax.rsqrt(jnp.mean(x * x, axis=-1, keepdims=True) + EPS) * g_ref[...]

    def issue(r, carry):
        for k in range(TOP_K):
            _row_copy(xn_scr.at[slot], r, xs_hbm, pos_ref[(i * tb + r) * TOP_K + k], sem.at[slot]).start()
        return carry

    lax.fori_loop(0, tb, issue, 0, unroll=8)

    @pl.when(i == n_steps - 1)
    def _():
        if n_steps >= 2:
            wait_slot(1 - slot)
        wait_slot(slot)


def _moe_expert_kernel(blk_e_ref, blk_rows_ref, xs_ref, wg_ref, wu_ref, wd_ref, o_ref, *, tm):
    del blk_e_ref
    rows = blk_rows_ref[pl.program_id(0)]

    @pl.when(rows > 0)
    def _():
        x = xs_ref[...].astype(jnp.bfloat16)
        g = jnp.dot(x, wg_ref[0], preferred_element_type=jnp.float32)
        u = jnp.dot(x, wu_ref[0], preferred_element_type=jnp.float32)
        h = (g * jax.nn.sigmoid(g) * u).astype(jnp.bfloat16)
        o_ref[...] = jnp.dot(h, wd_ref[0], preferred_element_type=jnp.float32)

    @pl.when(rows == 0)
    def _():
        o_ref[...] = jnp.zeros((tm, o_ref.shape[1]), jnp.float32)


def _moe_combine_kernel(pos_ref, x_ref, w_ref, g_ref, os_hbm, y_ref, buf, sem, *, tb, n_steps, out_norm):
    i = pl.program_id(0)
    slot = i % 2

    def fetch(step, s):
        def issue(r, carry):
            for k in range(TOP_K):
                _row_copy(os_hbm, pos_ref[(step * tb + r) * TOP_K + k], buf.at[s, k], r, sem.at[s]).start()
            return carry

        lax.fori_loop(0, tb, issue, 0, unroll=8)

    @pl.when(i == 0)
    def _():
        fetch(0, 0)

    @pl.when(i + 1 < n_steps)
    def _():
        fetch(i + 1, 1 - slot)

    for k in range(TOP_K):
        pltpu.make_async_copy(buf.at[slot, k], buf.at[slot, k], sem.at[slot]).wait()
    w = w_ref[...]
    y = x_ref[...]
    for k in range(TOP_K):
        y = y + w[:, k:k + 1] * buf[slot, k]
    if out_norm:
        y = y * lax.rsqrt(jnp.mean(y * y, axis=-1, keepdims=True) + EPS) * g_ref[...]
    y_ref[...] = y


def _router_kernel(x_ref, g_ref, w_ref, o_ref):
    x = x_ref[...]
    xn = x * lax.rsqrt(jnp.mean(x * x, axis=-1, keepdims=True) + EPS) * g_ref[...]
    o_ref[...] = jnp.dot(xn.astype(jnp.bfloat16), w_ref[...], preferred_element_type=jnp.float32)


def _router_logits(xt, norm_g, w_group, w_expert):
    n_tok, d_ = xt.shape
    n_log = N_EXPERT_GROUPS + N_EXPERTS
    w = jnp.concatenate([w_group, w_expert, jnp.zeros((d_, ROUTER_W - n_log), w_group.dtype)], axis=1)
    return pl.pallas_call(
        _router_kernel, grid=(n_tok // CONV_TM,),
        in_specs=[pl.BlockSpec((CONV_TM, d_), lambda i: (i, 0)), pl.BlockSpec((1, d_), lambda i: (0, 0)),
                  pl.BlockSpec((d_, ROUTER_W), lambda i: (0, 0))],
        out_specs=pl.BlockSpec((CONV_TM, ROUTER_W), lambda i: (i, 0)),
        out_shape=jax.ShapeDtypeStruct((n_tok, ROUTER_W), jnp.float32),
        compiler_params=pltpu.CompilerParams(dimension_semantics=("arbitrary",)),
        name="moe_router",
    )(xt, norm_g.astype(jnp.float32).reshape(1, d_), w.astype(jnp.bfloat16))


def _moe_route(logits, b_group, b_expert, tm):
    n_tok = logits.shape[0]
    gp = jax.nn.softmax(logits[:, :N_EXPERT_GROUPS] + b_group.astype(jnp.float32), axis=-1)
    g_idx = jnp.argmax(gp, axis=-1, keepdims=True)
    g_w = jnp.max(gp, axis=-1, keepdims=True)
    elog = logits[:, N_EXPERT_GROUPS:N_EXPERT_GROUPS + N_EXPERTS] + b_expert.astype(jnp.float32)
    elog = elog.reshape(n_tok, N_EXPERT_GROUPS, EXPERTS_PER_GROUP)
    elog_sel = jnp.take_along_axis(elog, g_idx[:, :, None], axis=1)[:, 0]
    e_w, e_idx = lax.top_k(jax.nn.softmax(elog_sel, axis=-1), TOP_K)
    e_w = e_w / jnp.sum(e_w, axis=-1, keepdims=True)
    weights = g_w * e_w
    experts = (g_idx * EXPERTS_PER_GROUP + e_idx).reshape(-1)
    onehot = (experts[:, None] == jnp.arange(N_EXPERTS)[None, :]).astype(jnp.int32)
    csum = jnp.cumsum(onehot, axis=0)
    counts = csum[-1]
    rank = jnp.sum(onehot * csum, axis=1) - 1
    padded = ((counts + tm - 1) // tm) * tm
    pend = jnp.cumsum(padded)
    pstart = pend - padded
    pos = (pstart[experts] + rank).astype(jnp.int32)
    n_blk = (n_tok * TOP_K) // tm + N_EXPERTS
    blk_start = jnp.arange(n_blk) * tm
    blk_e = jnp.minimum(jnp.sum(pend[None, :] <= blk_start[:, None], axis=1), N_EXPERTS - 1).astype(jnp.int32)
    blk_rows = jnp.clip(counts[blk_e] - (blk_start - pstart[blk_e]), 0, tm).astype(jnp.int32)
    return weights, pos, blk_e, blk_rows, n_blk


def hier_moe_residual(x, norm_g, w_group, b_group, w_expert, b_expert, w_gate, w_up, w_down,
                      out_norm_g=None, *, tm=MOE_TM, tb=MOE_TB):
    b_, s_, d_ = x.shape
    n_tok = b_ * s_
    xt = x.reshape(n_tok, d_)
    weights, pos, blk_e, blk_rows, n_blk = _moe_route(_router_logits(xt, norm_g, w_group, w_expert),
                                                      b_group, b_expert, tm)
    n_pad = n_blk * tm
    vmem = pltpu.CompilerParams(dimension_semantics=("arbitrary",),
                                vmem_limit_bytes=MOE_VMEM_LIMIT_BYTES)
    xs = pl.pallas_call(
        functools.partial(_moe_dispatch_kernel, tb=tb, n_steps=n_tok // tb),
        grid_spec=pltpu.PrefetchScalarGridSpec(
            num_scalar_prefetch=1, grid=(n_tok // tb,),
            in_specs=[pl.BlockSpec((tb, d_), lambda i, pos: (i, 0)),
                      pl.BlockSpec((1, d_), lambda i, pos: (0, 0)),
                      pl.BlockSpec(memory_space=pl.ANY)],
            out_specs=pl.BlockSpec(memory_space=pl.ANY),
            scratch_shapes=[pltpu.VMEM((2, tb, d_), jnp.float32), pltpu.SemaphoreType.DMA((2,))]),
        out_shape=jax.ShapeDtypeStruct((n_pad, d_), jnp.float32),
        input_output_aliases={3: 0},
        compiler_params=vmem, name="moe_dispatch",
    )(pos, xt, norm_g.astype(jnp.float32).reshape(1, d_), jnp.zeros((n_pad, d_), jnp.float32))
    bf16 = jnp.bfloat16
    outs = pl.pallas_call(
        functools.partial(_moe_expert_kernel, tm=tm),
        grid_spec=pltpu.PrefetchScalarGridSpec(
            num_scalar_prefetch=2, grid=(n_blk,),
            in_specs=[pl.BlockSpec((tm, d_), lambda i, be, br: (i, 0)),
                      pl.BlockSpec((1, d_, D_EXPERT), lambda i, be, br: (be[i], 0, 0)),
                      pl.BlockSpec((1, d_, D_EXPERT), lambda i, be, br: (be[i], 0, 0)),
                      pl.BlockSpec((1, D_EXPERT, d_), lambda i, be, br: (be[i], 0, 0))],
            out_specs=pl.BlockSpec((tm, d_), lambda i, be, br: (i, 0))),
        out_shape=jax.ShapeDtypeStruct((n_pad, d_), jnp.float32),
        compiler_params=vmem, name="moe_experts",
    )(blk_e, blk_rows, xs, w_gate.astype(bf16), w_up.astype(bf16), w_down.astype(bf16))
    y = pl.pallas_call(
        functools.partial(_moe_combine_kernel, tb=tb, n_steps=n_tok // tb, out_norm=out_norm_g is not None),
        grid_spec=pltpu.PrefetchScalarGridSpec(
            num_scalar_prefetch=1, grid=(n_tok // tb,),
            in_specs=[pl.BlockSpec((tb, d_), lambda i, pos: (i, 0)),
                      pl.BlockSpec((tb, TOP_K), lambda i, pos: (i, 0)),
                      pl.BlockSpec((1, d_), lambda i, pos: (0, 0)),
                      pl.BlockSpec(memory_space=pl.ANY)],
            out_specs=pl.BlockSpec((tb, d_), lambda i, pos: (i, 0)),
            scratch_shapes=[pltpu.VMEM((2, TOP_K, tb, d_), jnp.float32), pltpu.SemaphoreType.DMA((2,))]),
        out_shape=jax.ShapeDtypeStruct((n_tok, d_), jnp.float32),
        compiler_params=vmem, name="moe_combine",
    )(pos, xt, weights, (norm_g if out_norm_g is None else out_norm_g).astype(jnp.float32).reshape(1, d_), outs)
    return y.reshape(b_, s_, d_)


PROJ_DEST = {"a_qkv": (0, 0), "a_z": (1, 1536), "b_in": (4, 2048), "c_q": (5, 3584), "d_q": (7, 4096),
             "c_kv": (6, 4608), "d_kv": (8, 4864), "a_beta": (2, 5120), "a_alpha": (3, 5136)}
PROJ_WIDTH = 5632
assert PROJ_DEST["a_alpha"][1] == PROJ_DEST["a_beta"][1] + 2 * N_HEADS


def _in_proj_weight(w):
    starts = [0] + _split_points()
    cols = jnp.zeros((w.shape[0], PROJ_WIDTH), jnp.bfloat16)
    for seg, dest in PROJ_DEST.values():
        cols = lax.dynamic_update_slice(
            cols, w[:, starts[seg]:starts[seg] + IN_SPLIT_SIZES[seg]].astype(jnp.bfloat16), (0, dest))
    return cols


def _in_proj_kernel(x_ref, g_ref, w_ref, o_ref, xn_scr):
    @pl.when(pl.program_id(1) == 0)
    def _():
        x = x_ref[...]
        xn = x * lax.rsqrt(jnp.mean(x * x, axis=-1, keepdims=True) + EPS) * g_ref[...]
        xn_scr[...] = xn.astype(jnp.bfloat16)

    o_ref[...] = jnp.dot(xn_scr[...], w_ref[...], preferred_element_type=jnp.float32)


def in_proj(x, norm_g, w):
    n_tok, d_ = x.shape
    return pl.pallas_call(
        _in_proj_kernel,
        grid=(n_tok // PROJ_TM, PROJ_WIDTH // PROJ_TILE),
        in_specs=[pl.BlockSpec((PROJ_TM, d_), lambda i, j: (i, 0)),
                  pl.BlockSpec((1, d_), lambda i, j: (0, 0)),
                  pl.BlockSpec((d_, PROJ_TILE), lambda i, j: (0, j))],
        out_specs=pl.BlockSpec((PROJ_TM, PROJ_TILE), lambda i, j: (i, j)),
        out_shape=jax.ShapeDtypeStruct((n_tok, PROJ_WIDTH), jnp.float32),
        scratch_shapes=[pltpu.VMEM((PROJ_TM, d_), jnp.bfloat16)],
        compiler_params=pltpu.CompilerParams(dimension_semantics=("arbitrary", "arbitrary"),
                                             vmem_limit_bytes=PROJ_VMEM_LIMIT_BYTES),
        name="in_proj",
    )(x, norm_g.astype(jnp.float32).reshape(1, d_), _in_proj_weight(w))


def _out_proj_kernel(x_ref, *refs, normed):
    n = len(normed)
    y_refs, g_ref, w_ref, o_ref = refs[:n], refs[n], refs[n + 1], refs[n + 2]
    acc = x_ref[...]
    for k, y_ref in enumerate(y_refs):
        y = y_ref[...]
        if normed[k]:
            y = y * lax.rsqrt(jnp.mean(y * y, axis=-1, keepdims=True) + EPS) * g_ref[k:k + 1, :]
        acc = acc + jnp.dot(y.astype(jnp.bfloat16), w_ref[k * GROUP_W:(k + 1) * GROUP_W, :],
                            preferred_element_type=jnp.float32)
    o_ref[...] = acc


def out_proj_residual(x, ys, gains, w):
    n_tok, d_ = x.shape
    row = lambda width: pl.BlockSpec((OUT_TM, width), lambda i: (i, 0))
    g = jnp.stack([jnp.ones((GROUP_W,), jnp.float32) if gk is None else gk.astype(jnp.float32)
                   for gk in gains])
    return pl.pallas_call(
        functools.partial(_out_proj_kernel, normed=tuple(gk is not None for gk in gains)),
        grid=(n_tok // OUT_TM,),
        in_specs=[row(d_)] + [row(GROUP_W)] * len(ys) + [pl.BlockSpec(g.shape, lambda i: (0, 0)),
                                                          pl.BlockSpec(w.shape, lambda i: (0, 0))],
        out_specs=row(d_),
        out_shape=jax.ShapeDtypeStruct((n_tok, d_), jnp.float32),
        compiler_params=pltpu.CompilerParams(dimension_semantics=("arbitrary",),
                                             vmem_limit_bytes=PROJ_VMEM_LIMIT_BYTES),
        name="out_proj",
    )(x, *ys, g, w.astype(jnp.bfloat16))


def kernel(x, norm_mix, w_in, gdn_conv, gdn_a_log, gdn_dt_bias, gdn_norm, hy_conv, hy_w1, hy_b1, hy_freq1, hy_w2, hy_b2, hy_freq2, hy_w3, hy_deltas, hy_bias, hy_norm, swa_sink, swa_norm, ga_q_norm, ga_k_norm, ga_norm, w_out, norm_ffn, moe_w_group, moe_b_group, moe_w_expert, moe_b_expert, moe_w_gate, moe_w_up, moe_w_down, norm_final):
    b_, s_, _ = x.shape
    rows = s_ // GRID_W
    row_idx = jnp.repeat(jnp.arange(rows), GRID_W)
    col_idx = jnp.tile(jnp.arange(GRID_W), rows)
    rope_tables = _rope_tables(row_idx, col_idx, N_HEADS)
    pos_feat = hyena_pos_features(s_)
    slopes = alibi_slopes(N_HEADS)
    n_tok = b_ * s_
    for l in range(DEPTH):
        proj = in_proj(x.reshape(n_tok, D_MODEL), norm_mix[l], w_in[l]).reshape(b_, s_, PROJ_WIDTH)
        y_a = gdn_mixer(proj, gdn_conv[l], gdn_a_log[l], gdn_dt_bias[l], gdn_norm[l])
        kf = hyena_filters_f(pos_feat, hy_w1[l], hy_b1[l], hy_freq1[l], hy_w2[l], hy_b2[l],
                             hy_freq2[l], hy_w3[l], hy_deltas[l])
        y_b = hyena_mixer(proj, hy_conv[l], kf, hy_bias[l])
        y_c = window_mixer(proj, PROJ_DEST["c_q"][1], PROJ_DEST["c_kv"][1], swa_sink[l], slopes, swa_norm[l])
        y_d = global_mixer(proj, rope_tables, ga_q_norm[l], ga_k_norm[l])
        x = out_proj_residual(x.reshape(n_tok, D_MODEL),
                              [y.reshape(n_tok, GROUP_W) for y in (y_a, y_b, y_c, y_d)],
                              [None, hy_norm[l], None, ga_norm[l]], w_out[l]).reshape(b_, s_, D_MODEL)
        x = hier_moe_residual(x, norm_ffn[l], moe_w_group[l], moe_b_group[l], moe_w_expert[l],
                              moe_b_expert[l], moe_w_gate[l], moe_w_up[l], moe_w_down[l],
                              norm_final if l == DEPTH - 1 else None)
    return x
```
